```python
import math
import jax
import jax.numpy as jnp
from jax import lax
import numpy as np

D_MODEL = 1024
BATCH = 2
SEQ = 16384
DEPTH = 4

CTX_LEN = 256
GRID_W = 64
ROPE_THETA = 10000.0
NORM_EPS = 1e-6
NEG_INF = -1e30

DA_HEADS = 8
DA_DIM = 32
DA_BLOCK = 128
WA_HEADS = 8
WA_KV_HEADS = 2
WA_DIM = 64
WA_WINDOW = 128
WA_BLOCK = 128
LRU_WIDTH = 768
LRU_BLOCKS = 8
LRU_BLOCK_DIM = LRU_WIDTH // LRU_BLOCKS
LRU_CONV = 4
LRU_C = 8.0
S5_WIDTH = 256
S5_GROUP = 16
S5_GROUPS = S5_WIDTH // S5_GROUP
S5_STATE = 64
D_FF = 2816
FFN_CONV = 3

N_ATT = (DEPTH + 1) // 2
N_REC = DEPTH // 2

DA_QK = DA_HEADS * 2 * DA_DIM
DA_V = DA_HEADS * 2 * DA_DIM
WA_Q = WA_HEADS * WA_DIM
WA_KV = WA_KV_HEADS * WA_DIM
ATT_SPLIT = (DA_QK, 2 * DA_QK, 2 * DA_QK + DA_V, 2 * DA_QK + DA_V + WA_Q, 2 * DA_QK + DA_V + WA_Q + WA_KV)
ATT_IN = 2 * DA_QK + DA_V + WA_Q + 2 * WA_KV
ATT_MIX = DA_V + WA_Q
REC_IN = 2 * LRU_WIDTH + S5_WIDTH
REC_MIX = LRU_WIDTH + S5_WIDTH

kernel_name = 'hybrid_diffusion_prefix_trunk'


def rmsnorm(x, g):
    xf = x.astype(jnp.float32)
    y = xf * lax.rsqrt(jnp.mean(xf * xf, axis=-1, keepdims=True) + NORM_EPS)
    return (y * g.astype(jnp.float32)).astype(x.dtype)


def modulate(x, g, shift, scale):
    return rmsnorm(x, g) * (1.0 + scale) + shift


def ada_mod(cond, w, b):
    m = jax.nn.silu(cond) @ w + b
    return jnp.split(m[..., None, :], 6, axis=-1)


def dwconv(x, w, pad_left):
    k, ch = w.shape
    return lax.conv_general_dilated(x, w[:, None, :].astype(x.dtype), window_strides=(1,),
                                    padding=[(pad_left, k - 1 - pad_left)],
                                    dimension_numbers=('NWC', 'WIO', 'NWC'), feature_group_count=ch)


def axial_rope_tables(n, dim):
    rows = n // GRID_W
    row = jnp.repeat(jnp.arange(rows), GRID_W).astype(jnp.float32)
    col = jnp.tile(jnp.arange(GRID_W), rows).astype(jnp.float32)
    half = dim // 2
    freqs = ROPE_THETA ** (-jnp.arange(0, half, 2, dtype=jnp.float32) / half)
    def angles(pos):
        a = pos[:, None] * freqs[None, :]
        return jnp.concatenate([a, a], axis=-1)
    ang = jnp.concatenate([angles(row), angles(col)], axis=-1)
    return jnp.cos(ang), jnp.sin(ang)


def _rotate_half(z):
    z1, z2 = jnp.split(z, 2, axis=-1)
    return jnp.concatenate([-z2, z1], axis=-1)


def apply_rope(x, cos, sin):
    shape = (1, cos.shape[0]) + (1,) * (x.ndim - 3) + (cos.shape[1],)
    c = cos.reshape(shape).astype(x.dtype)
    s = sin.reshape(shape).astype(x.dtype)
    half = x.shape[-1] // 2
    rot = jnp.concatenate([_rotate_half(x[..., :half]), _rotate_half(x[..., half:])], axis=-1)
    return x * c + rot * s


def _lin_combine(e1, e2):
    a1, b1 = e1
    a2, b2 = e2
    return a1 * a2, a2 * b1 + b2


def linear_scan(a, b, h0, reverse):
    if h0 is not None:
        e = -1 if reverse else 0
        b = b.at[:, e].add(a[:, e] * h0)
    _, h = lax.associative_scan(_lin_combine, (a, b), reverse=reverse, axis=1)
    return h


def _cplx_combine(e1, e2):
    ar1, ai1, br1, bi1 = e1
    ar2, ai2, br2, bi2 = e2
    return (ar1 * ar2 - ai1 * ai2, ar1 * ai2 + ai1 * ar2,
            ar2 * br1 - ai2 * bi1 + br2, ar2 * bi1 + ai2 * br1 + bi2)


def complex_scan(ar, ai, br, bi, h0, reverse):
    if h0 is not None:
        h0r, h0i = h0
        e = -1 if reverse else 0
        br = br.at[:, e].add(ar[:, e] * h0r - ai[:, e] * h0i)
        bi = bi.at[:, e].add(ar[:, e] * h0i + ai[:, e] * h0r)
    _, _, hr, hi = lax.associative_scan(_cplx_combine, (ar, ai, br, bi), reverse=reverse, axis=1)
    return hr, hi


def att_heads(z):
    bsz, t, _ = z.shape
    aq, ak, av, bq, bk, bv = jnp.split(z, ATT_SPLIT, axis=-1)
    return (aq.reshape(bsz, t, DA_HEADS, 2, DA_DIM), ak.reshape(bsz, t, DA_HEADS, 2, DA_DIM),
            av.reshape(bsz, t, DA_HEADS, 2 * DA_DIM), bq.reshape(bsz, t, WA_HEADS, WA_DIM),
            bk.reshape(bsz, t, WA_KV_HEADS, WA_DIM), bv.reshape(bsz, t, WA_KV_HEADS, WA_DIM))


def diff_attention(ql, kl, vl, qc, kc, vc, lam, lam_init, subln_g, ctx_out):
    scale = DA_DIM ** -0.5
    lam = lam.astype(vl.dtype)
    k_all = jnp.concatenate([kc, kl], axis=1)
    v_all = jnp.concatenate([vc, vl], axis=1)
    def attend(q, k, v):
        s = jnp.einsum('bqhmd,bkhmd->bhmqk', q, k).astype(jnp.float32) * scale
        p = jax.nn.softmax(s, axis=-1).astype(v.dtype)
        o = jnp.einsum('bhmqk,bkhe->bqhme', p, v)
        o = o[:, :, :, 0] - lam * o[:, :, :, 1]
        o = rmsnorm(o, subln_g) * (1.0 - lam_init)
        return o.reshape(o.shape[0], o.shape[1], -1)
    bsz, t = ql.shape[:2]
    nb = t // DA_BLOCK
    qb = jnp.moveaxis(ql.reshape((bsz, nb, DA_BLOCK) + ql.shape[2:]), 1, 0)
    ob = lax.map(lambda q: attend(q, k_all, v_all), qb)
    o_lat = jnp.moveaxis(ob, 0, 1).reshape(bsz, t, -1)
    o_ctx = attend(qc, kc, vc) if ctx_out else None
    return o_lat, o_ctx


def window_attention(ql, kl, vl, qc, kc, vc, sink, ctx_out):
    g = WA_HEADS // WA_KV_HEADS
    scale = WA_DIM ** -0.5
    sink_g = sink.astype(jnp.float32).reshape(WA_KV_HEADS, g)
    bsz, t = ql.shape[:2]
    n_ctx = kc.shape[1]
    band = WA_BLOCK + 2 * WA_WINDOW
    def attend(q, k, v, mask):
        s = jnp.einsum('bqhgd,bkhd->bhgqk', q, k).astype(jnp.float32) * scale
        if mask is not None:
            s = jnp.where(mask, s, NEG_INF)
        sk = jnp.broadcast_to(sink_g[None, :, :, None, None], s.shape[:-1] + (1,))
        p = jax.nn.softmax(jnp.concatenate([s, sk], axis=-1), axis=-1)[..., :-1].astype(v.dtype)
        o = jnp.einsum('bhgqk,bkhd->bqhgd', p, v)
        return o.reshape(o.shape[0], o.shape[1], -1)
    kp = jnp.pad(kl, ((0, 0), (WA_WINDOW, WA_WINDOW), (0, 0), (0, 0)))
    vp = jnp.pad(vl, ((0, 0), (WA_WINDOW, WA_WINDOW), (0, 0), (0, 0)))
    nb = t // WA_BLOCK
    qb = jnp.moveaxis(ql.reshape(bsz, nb, WA_BLOCK, WA_KV_HEADS, g, WA_DIM), 1, 0)
    qpos = jnp.arange(WA_BLOCK)[:, None]
    kpos = jnp.arange(band)[None, :] - WA_WINDOW
    rel_ok = jnp.abs(kpos - qpos) <= WA_WINDOW
    ctx_ok = jnp.ones((WA_BLOCK, n_ctx), dtype=bool)
    def block(args):
        b, q = args
        start = b * WA_BLOCK
        kb = lax.dynamic_slice_in_dim(kp, start, band, axis=1)
        vb = lax.dynamic_slice_in_dim(vp, start, band, axis=1)
        absk = start + kpos
        mask = jnp.concatenate([ctx_ok, rel_ok & (absk >= 0) & (absk < t)], axis=-1)
        return attend(q, jnp.concatenate([kc, kb], axis=1), jnp.concatenate([vc, vb], axis=1), mask)
    ob = lax.map(block, (jnp.arange(nb), qb))
    o_lat = jnp.moveaxis(ob, 0, 1).reshape(bsz, t, -1)
    o_ctx = None
    if ctx_out:
        o_ctx = attend(qc.reshape(bsz, n_ctx, WA_KV_HEADS, g, WA_DIM), kc, vc, None)
    return o_lat, o_ctx


def attention_mixer(h_lat, h_ctx, rope_a, rope_b, w_in, w_out, lq1, lk1, lq2, lk2, subln_g, sink,
                    lam_init, ctx_out):
    aql, akl, avl, bql, bkl, bvl = att_heads(h_lat @ w_in)
    aqc, akc, avc, bqc, bkc, bvc = att_heads(h_ctx @ w_in)
    aql, akl = apply_rope(aql, *rope_a), apply_rope(akl, *rope_a)
    bql, bkl = apply_rope(bql, *rope_b), apply_rope(bkl, *rope_b)
    f32 = jnp.float32
    lam = (jnp.exp(jnp.sum(lq1.astype(f32) * lk1.astype(f32)))
           - jnp.exp(jnp.sum(lq2.astype(f32) * lk2.astype(f32))) + lam_init)
    oal, oac = diff_attention(aql, akl, avl, aqc, akc, avc, lam, lam_init, subln_g, ctx_out)
    obl, obc = window_attention(bql, bkl, bvl, bqc, bkc, bvc, sink, ctx_out)
    o_lat = jnp.concatenate([oal, obl], axis=-1) @ w_out
    o_ctx = jnp.concatenate([oac, obc], axis=-1) @ w_out if ctx_out else None
    return o_lat, o_ctx


def rglru_bidir(x_lat, x_ctx, w_a, b_a, w_x, b_x, lam, ctx_out):
    f32 = jnp.float32
    def gates(x):
        bsz, t, _ = x.shape
        xf = x.astype(f32)
        xb = xf.reshape(bsz, t, LRU_BLOCKS, LRU_BLOCK_DIM)
        r = jax.nn.sigmoid(jnp.einsum('btni,dnij->dbtnj', xb, w_a.astype(f32)).reshape(2, bsz, t, LRU_WIDTH)
                           + b_a.astype(f32)[:, None, None, :])
        i = jax.nn.sigmoid(jnp.einsum('btni,dnij->dbtnj', xb, w_x.astype(f32)).reshape(2, bsz, t, LRU_WIDTH)
                           + b_x.astype(f32)[:, None, None, :])
        log_a = -LRU_C * r * jax.nn.softplus(-lam.astype(f32))[:, None, None, :]
        return jnp.exp(log_a), jnp.sqrt(-jnp.expm1(2.0 * log_a)) * (i * xf[None])
    a_c, b_c = gates(x_ctx)
    a_l, b_l = gates(x_lat)
    hf_c = linear_scan(a_c[0], b_c[0], None, False)
    hb_c = linear_scan(a_c[1], b_c[1], None, True)
    hf_l = linear_scan(a_l[0], b_l[0], hf_c[:, -1], False)
    hb_l = linear_scan(a_l[1], b_l[1], hb_c[:, 0], True)
    h_lat = (hf_l + hb_l).astype(x_lat.dtype)
    h_ctx = (hf_c + hb_c).astype(x_ctx.dtype) if ctx_out else None
    return h_lat, h_ctx


def s5_bidir(u_lat, u_ctx, a_re, a_im, log_step, b_re, b_im, c_re, c_im, d_skip, ctx_out):
    f32 = jnp.float32
    a_re, a_im = a_re.astype(f32), a_im.astype(f32)
    step = jnp.exp(log_step.astype(f32))[..., None]
    mag = jnp.exp(a_re * step)
    abr, abi = mag * jnp.cos(a_im * step), mag * jnp.sin(a_im * step)
    den = a_re * a_re + a_im * a_im
    qr = ((abr - 1.0) * a_re + abi * a_im) / den
    qi = (abi * a_re - (abr - 1.0) * a_im) / den
    br_, bi_ = b_re.astype(f32), b_im.astype(f32)
    bbr = qr[..., None] * br_ - qi[..., None] * bi_
    bbi = qr[..., None] * bi_ + qi[..., None] * br_
    cr, ci = c_re.astype(f32), c_im.astype(f32)
    def group(u):
        return u.astype(f32).reshape(u.shape[0], u.shape[1], S5_GROUPS, S5_GROUP)
    ul, uc = group(u_lat), group(u_ctx)
    def scan_dir(u, d, h0, reverse):
        br = jnp.einsum('btgh,gph->btgp', u, bbr[d])
        bi = jnp.einsum('btgh,gph->btgp', u, bbi[d])
        ar = jnp.broadcast_to(abr[d], br.shape)
        ai = jnp.broadcast_to(abi[d], br.shape)
        return complex_scan(ar, ai, br, bi, h0, reverse)
    def readout(h, d):
        return jnp.einsum('btgp,ghp->btgh', h[0], cr[d]) - jnp.einsum('btgp,ghp->btgh', h[1], ci[d])
    dsk = d_skip.astype(f32).reshape(S5_GROUPS, S5_GROUP)
    y_lat = dsk * ul
    y_ctx = dsk * uc if ctx_out else None
    for d, reverse in ((0, False), (1, True)):
        edge = 0 if reverse else -1
        hc = scan_dir(uc, d, None, reverse)
        hl = scan_dir(ul, d, (hc[0][:, edge], hc[1][:, edge]), reverse)
        y_lat = y_lat + readout(hl, d)
        if ctx_out:
            y_ctx = y_ctx + readout(hc, d)
    y_lat = y_lat.reshape(u_lat.shape).astype(u_lat.dtype)
    if ctx_out:
        y_ctx = y_ctx.reshape(u_ctx.shape).astype(u_ctx.dtype)
    return y_lat, y_ctx


def recurrent_mixer(h_lat, h_ctx, w_in, w_out, conv_w, conv_b, w_a, b_a, w_x, b_x, lam,
                    a_re, a_im, log_step, b_re, b_im, c_re, c_im, d_skip, w_glu, ctx_out):
    split = (LRU_WIDTH, 2 * LRU_WIDTH)
    gl, rl, ul = jnp.split(h_lat @ w_in, split, axis=-1)
    gc, rc, uc = jnp.split(h_ctx @ w_in, split, axis=-1)
    rl = dwconv(rl, conv_w, LRU_CONV // 2) + conv_b
    rc = dwconv(rc, conv_w, LRU_CONV // 2) + conv_b
    hl, hc = rglru_bidir(rl, rc, w_a, b_a, w_x, b_x, lam, ctx_out)
    yl, yc = s5_bidir(ul, uc, a_re, a_im, log_step, b_re, b_im, c_re, c_im, d_skip, ctx_out)
    def merge(gate, h, y):
        yg = jax.nn.gelu(y)
        return jnp.concatenate([h * jax.nn.gelu(gate), yg * jax.nn.sigmoid(yg @ w_glu)], axis=-1) @ w_out
    o_lat = merge(gl, hl, yl)
    o_ctx = merge(gc, hc, yc) if ctx_out else None
    return o_lat, o_ctx


def conv_ffn(h, w_g, w_u, conv_w, w_down):
    g = dwconv(h @ w_g, conv_w, FFN_CONV // 2)
    return (jax.nn.silu(g) * (h @ w_u)) @ w_down


def setup_inputs(seed: int = 0) -> dict:
    key = jax.random.key(seed)
    ks = iter(jax.random.split(key, 64))
    f32 = jnp.float32
    def nrm(shape, s):
        return jax.random.normal(next(ks), shape, f32) * s
    def gain(shape):
        return 1.0 + nrm(shape, 0.01)
    D = D_MODEL
    x = nrm((BATCH, SEQ, D), 1.0)
    c = nrm((BATCH, D), 1.0)
    ctx = nrm((BATCH, CTX_LEN, D), 1.0)
    c_ctx = nrm((D,), 1.0)
    ada_w = nrm((DEPTH, D, 6 * D), 0.5 * D ** -0.5)
    ada_b = nrm((DEPTH, 6 * D), 0.01)
    norm1_g = gain((DEPTH, D))
    norm2_g = gain((DEPTH, D))
    att_w_in = nrm((N_ATT, D, ATT_IN), D ** -0.5)
    att_w_out = nrm((N_ATT, ATT_MIX, D), ATT_MIX ** -0.5)
    da_lam_q1 = nrm((N_ATT, DA_DIM), 0.1)
    da_lam_k1 = nrm((N_ATT, DA_DIM), 0.1)
    da_lam_q2 = nrm((N_ATT, DA_DIM), 0.1)
    da_lam_k2 = nrm((N_ATT, DA_DIM), 0.1)
    da_subln_g = gain((N_ATT, 2 * DA_DIM))
    wa_sink = nrm((N_ATT, WA_HEADS), 0.1)
    rec_w_in = nrm((N_REC, D, REC_IN), D ** -0.5)
    rec_w_out = nrm((N_REC, REC_MIX, D), REC_MIX ** -0.5)
    lru_conv_w = nrm((N_REC, LRU_CONV, LRU_WIDTH), LRU_CONV ** -0.5)
    lru_conv_b = nrm((N_REC, LRU_WIDTH), 0.01)
    lru_w_a = nrm((N_REC, 2, LRU_BLOCKS, LRU_BLOCK_DIM, LRU_BLOCK_DIM), LRU_BLOCK_DIM ** -0.5)
    lru_b_a = nrm((N_REC, 2, LRU_WIDTH), 0.01)
    lru_w_x = nrm((N_REC, 2, LRU_BLOCKS, LRU_BLOCK_DIM, LRU_BLOCK_DIM), LRU_BLOCK_DIM ** -0.5)
    lru_b_x = nrm((N_REC, 2, LRU_WIDTH), 0.01)
    u = jax.random.uniform(next(ks), (N_REC, 2, LRU_WIDTH), f32, minval=0.9, maxval=0.999)
    a_root = u ** (1.0 / LRU_C)
    lru_lam = jnp.log(a_root) - jnp.log1p(-a_root)
    s5_a_re = -0.5 + nrm((N_REC, 2, S5_GROUPS, S5_STATE), 0.01)
    s5_a_im = math.pi * jnp.arange(S5_STATE, dtype=f32) + nrm((N_REC, 2, S5_GROUPS, S5_STATE), 0.01)
    s5_log_step = jax.random.uniform(next(ks), (N_REC, 2, S5_GROUPS), f32,
                                     minval=math.log(0.001), maxval=math.log(0.1))
    s5_b_re = nrm((N_REC, 2, S5_GROUPS, S5_STATE, S5_GROUP), (2.0 * S5_GROUP) ** -0.5)
    s5_b_im = nrm((N_REC, 2, S5_GROUPS, S5_STATE, S5_GROUP), (2.0 * S5_GROUP) ** -0.5)
    s5_c_re = nrm((N_REC, 2, S5_GROUPS, S5_GROUP, S5_STATE), (2.0 * S5_STATE) ** -0.5)
    s5_c_im = nrm((N_REC, 2, S5_GROUPS, S5_GROUP, S5_STATE), (2.0 * S5_STATE) ** -0.5)
    s5_d = nrm((N_REC, S5_WIDTH), 1.0)
    s5_w_glu = nrm((N_REC, S5_WIDTH, S5_WIDTH), S5_WIDTH ** -0.5)
    ffn_w_g = nrm((DEPTH, D, D_FF), D ** -0.5)
    ffn_w_u = nrm((DEPTH, D, D_FF), D ** -0.5)
    ffn_conv_w = nrm((DEPTH, FFN_CONV, D_FF), FFN_CONV ** -0.5)
    ffn_w_down = nrm((DEPTH, D_FF, D), D_FF ** -0.5)
    final_g = gain((D,))
    return {'x': x, 'c': c, 'ctx': ctx, 'c_ctx': c_ctx, 'ada_w': ada_w, 'ada_b': ada_b,
            'norm1_g': norm1_g, 'norm2_g': norm2_g, 'att_w_in': att_w_in, 'att_w_out': att_w_out,
            'da_lam_q1': da_lam_q1, 'da_lam_k1': da_lam_k1, 'da_lam_q2': da_lam_q2, 'da_lam_k2': da_lam_k2,
            'da_subln_g': da_subln_g, 'wa_sink': wa_sink, 'rec_w_in': rec_w_in, 'rec_w_out': rec_w_out,
            'lru_conv_w': lru_conv_w, 'lru_conv_b': lru_conv_b, 'lru_w_a': lru_w_a, 'lru_b_a': lru_b_a,
            'lru_w_x': lru_w_x, 'lru_b_x': lru_b_x, 'lru_lam': lru_lam, 's5_a_re': s5_a_re,
            's5_a_im': s5_a_im, 's5_log_step': s5_log_step, 's5_b_re': s5_b_re, 's5_b_im': s5_b_im,
            's5_c_re': s5_c_re, 's5_c_im': s5_c_im, 's5_d': s5_d, 's5_w_glu': s5_w_glu,
            'ffn_w_g': ffn_w_g, 'ffn_w_u': ffn_w_u, 'ffn_conv_w': ffn_conv_w, 'ffn_w_down': ffn_w_down,
            'final_g': final_g}


def reference(x, c, ctx, c_ctx, ada_w, ada_b, norm1_g, norm2_g, att_w_in, att_w_out,
              da_lam_q1, da_lam_k1, da_lam_q2, da_lam_k2, da_subln_g, wa_sink, rec_w_in, rec_w_out,
              lru_conv_w, lru_conv_b, lru_w_a, lru_b_a, lru_w_x, lru_b_x, lru_lam, s5_a_re, s5_a_im,
              s5_log_step, s5_b_re, s5_b_im, s5_c_re, s5_c_im, s5_d, s5_w_glu, ffn_w_g, ffn_w_u,
              ffn_conv_w, ffn_w_down, final_g):
    n_lat = x.shape[1]
    rope_a = axial_rope_tables(n_lat, DA_DIM)
    rope_b = axial_rope_tables(n_lat, WA_DIM)
    for l in range(DEPTH):
        ctx_out = l < DEPTH - 1
        sh1, sc1, g1, sh2, sc2, g2 = ada_mod(c, ada_w[l], ada_b[l])
        csh1, csc1, cg1, csh2, csc2, cg2 = ada_mod(c_ctx[None, :], ada_w[l], ada_b[l])
        h_lat = modulate(x, norm1_g[l], sh1, sc1)
        h_ctx = modulate(ctx, norm1_g[l], csh1, csc1)
        j = l // 2
        if l % 2 == 0:
            lam_init = 0.8 - 0.6 * math.exp(-0.3 * l)
            o_lat, o_ctx = attention_mixer(h_lat, h_ctx, rope_a, rope_b, att_w_in[j], att_w_out[j],
                                           da_lam_q1[j], da_lam_k1[j], da_lam_q2[j], da_lam_k2[j],
                                           da_subln_g[j], wa_sink[j], lam_init, ctx_out)
        else:
            o_lat, o_ctx = recurrent_mixer(h_lat, h_ctx, rec_w_in[j], rec_w_out[j], lru_conv_w[j],
                                           lru_conv_b[j], lru_w_a[j], lru_b_a[j], lru_w_x[j], lru_b_x[j],
                                           lru_lam[j], s5_a_re[j], s5_a_im[j], s5_log_step[j], s5_b_re[j],
                                           s5_b_im[j], s5_c_re[j], s5_c_im[j], s5_d[j], s5_w_glu[j], ctx_out)
        x = x + g1 * o_lat
        x = x + g2 * conv_ffn(modulate(x, norm2_g[l], sh2, sc2), ffn_w_g[l], ffn_w_u[l], ffn_conv_w[l],
                              ffn_w_down[l])
        if ctx_out:
            ctx = ctx + cg1 * o_ctx
            ctx = ctx + cg2 * conv_ffn(modulate(ctx, norm2_g[l], csh2, csc2), ffn_w_g[l], ffn_w_u[l],
                                       ffn_conv_w[l], ffn_w_down[l])
    return rmsnorm(x, final_g)
```

```python
import functools
import math

import jax
import jax.numpy as jnp
from jax import lax
from jax.experimental import pallas as pl
from jax.experimental.pallas import tpu as pltpu

F32 = jnp.float32
BF16 = jnp.bfloat16

D_MODEL = 1024
DEPTH = 4
GRID_W = 64
ROPE_THETA = 10000.0
NORM_EPS = 1e-6
NEG_INF = -1e30

DA_HEADS = 8
DA_DIM = 32
WA_HEADS = 8
WA_KV_HEADS = 2
WA_DIM = 64
WA_WINDOW = 128
WA_GROUP = WA_HEADS // WA_KV_HEADS
LRU_WIDTH = 768
LRU_BLOCKS = 8
LRU_BLOCK_DIM = LRU_WIDTH // LRU_BLOCKS
LRU_CONV = 4
LRU_C = 8.0
S5_WIDTH = 256
S5_GROUP = 16
S5_GROUPS = S5_WIDTH // S5_GROUP
S5_STATE = 64
S5_CHUNK = 16
S5_LANES = S5_GROUPS * S5_STATE
D_FF = 2816
FFN_CHUNK = 1408

DA_QK = DA_HEADS * 2 * DA_DIM
DA_V = DA_HEADS * 2 * DA_DIM
WA_Q = WA_HEADS * WA_DIM
WA_KV = WA_KV_HEADS * WA_DIM
ATT_IN = 2 * DA_QK + DA_V + WA_Q + 2 * WA_KV

SUBLANES = 8
LANES = 128
VMEM_LIMIT_BYTES = 56 * 1024 * 1024
ROW_TILE = 512
DA_TQ = 512
DA_TK = 256


def _params(*sem):
    return pltpu.CompilerParams(dimension_semantics=sem, vmem_limit_bytes=VMEM_LIMIT_BYTES)


def _resident(shape):
    nd = len(shape)
    return pl.BlockSpec(shape, lambda *_: (0,) * nd, pipeline_mode=pl.Buffered(1))


def _sigmoid(x):
    return 1.0 / (1.0 + jnp.exp(-x))


def _gelu(x):
    return 0.5 * x * (1.0 + jnp.tanh(math.sqrt(2.0 / math.pi) * (x + 0.044715 * (x * x * x))))


def _modulated(x, g, shift, scale):
    y = x * lax.rsqrt(jnp.mean(x * x, axis=-1, keepdims=True) + NORM_EPS)
    return (y * g * (1.0 + scale) + shift).astype(BF16)


def _ada_kernel(c_ref, w_ref, b_ref, o_ref):
    c = c_ref[...]
    s = c * _sigmoid(c)
    o_ref[...] = jnp.dot(s, w_ref[...], precision=lax.Precision.HIGHEST,
                         preferred_element_type=F32) + b_ref[...]


def _ada_all(cond, ada_w, ada_b):
    depth, d, n = ada_w.shape
    tn = 1536
    return pl.pallas_call(
        _ada_kernel,
        grid=(depth, n // tn),
        in_specs=[pl.BlockSpec((SUBLANES, d), lambda l, j: (0, 0)),
                  pl.BlockSpec((None, d, tn), lambda l, j: (l, 0, j)),
                  pl.BlockSpec((None, 1, tn), lambda l, j: (l, 0, j))],
        out_specs=pl.BlockSpec((None, SUBLANES, tn), lambda l, j: (l, 0, j)),
        out_shape=jax.ShapeDtypeStruct((depth, SUBLANES, n), F32),
        compiler_params=_params("parallel", "parallel"),
        name="ada_mod",
    )(cond, ada_w, ada_b.reshape(depth, 1, n))


def _att_inproj_kernel(x_ref, g_ref, sh_ref, sc_ref, w_ref, ca_ref, sa_ref, cb_ref, sb_ref,
                       aq_ref, ak_ref, av_ref, bq_ref, bk_ref, bv_ref):
    h = _modulated(x_ref[...], g_ref[...], sh_ref[...], sc_ref[...])
    z = jnp.dot(h, w_ref[...], preferred_element_type=F32)
    ca = jnp.tile(ca_ref[...], (1, 4))
    sa = jnp.tile(sa_ref[...], (1, 4))
    cb = cb_ref[...]
    sb = sb_ref[...]
    cb4 = jnp.tile(cb, (1, 4))
    sb4 = jnp.tile(sb, (1, 4))
    o = ATT_IN
    aq = z[:, 0:512] * ca + z[:, o:o + 512] * sa
    ak = z[:, 512:1024] * ca + z[:, o + 512:o + 1024] * sa
    bq = z[:, 1536:2048] * cb4 + z[:, o + 1024:o + 1536] * sb4
    bk = z[:, 2048:2176] * cb + z[:, o + 1536:o + 1664] * sb
    aq_ref[...] = (aq * (DA_DIM ** -0.5)).astype(BF16)
    ak_ref[...] = ak.astype(BF16)
    av_ref[...] = z[:, 1024:1536].astype(BF16)
    bq_ref[...] = (bq * (WA_DIM ** -0.5)).astype(BF16)
    bk_ref[...] = bk.astype(BF16)
    bv_ref[...] = z[:, 2176:2304].astype(BF16)


def _att_inproj(x, g, sh, sc, w_ext, ca, sa, cb, sb):
    bsz, t, d = x.shape
    tm = min(ROW_TILE, t)
    n = w_ext.shape[1]
    row = lambda w: pl.BlockSpec((None, tm, w), lambda b, i: (b, i, 0))
    vec = pl.BlockSpec((None, 1, d), lambda b, i: (b, 0, 0))
    tab = pl.BlockSpec((tm, LANES), lambda b, i: (i, 0))
    outs = [512, 512, 512, 512, 128, 128]
    return pl.pallas_call(
        _att_inproj_kernel,
        grid=(bsz, t // tm),
        in_specs=[row(d), pl.BlockSpec((1, d), lambda b, i: (0, 0)), vec, vec, _resident((d, n)),
                  tab, tab, tab, tab],
        out_specs=[row(w) for w in outs],
        out_shape=[jax.ShapeDtypeStruct((bsz, t, w), BF16) for w in outs],
        compiler_params=_params("parallel", "parallel"),
        name="att_inproj",
    )(x, g, sh, sc, w_ext, ca, sa, cb, sb)


def _da_kernel(lam_ref, qt_ref, k_ref, vt_ref, g_ref, o_ref, acc_ref, *, tk, nkt, out_scale):
    tq = qt_ref.shape[-1]
    qf = qt_ref[...].astype(F32)
    row = lax.broadcasted_iota(jnp.int32, qf.shape, 0)
    w = [jnp.where(jnp.right_shift(row, 5) == s, qf, 0.0).astype(BF16) for s in range(4)]
    acc_ref[...] = jnp.zeros_like(acc_ref)

    def body(kt, carry):
        ms, ls = carry
        off = pl.multiple_of(kt * tk, tk)
        k_tile = k_ref[pl.ds(off, tk), :]
        vt = vt_ref[kt]
        new_m, new_l = [], []
        for s in range(4):
            st = jnp.dot(k_tile, w[s], preferred_element_type=F32)
            m_new = jnp.maximum(ms[s], jnp.max(st, axis=0, keepdims=True))
            alpha = jnp.exp(ms[s] - m_new)
            p = jnp.exp(st - m_new)
            new_l.append(alpha * ls[s] + jnp.sum(p, axis=0, keepdims=True))
            new_m.append(m_new)
            h = s // 2
            pv = jnp.dot(vt[64 * h:64 * h + 64, :], p.astype(BF16), preferred_element_type=F32)
            acc_ref[s] = alpha * acc_ref[s] + pv
        return tuple(new_m), tuple(new_l)

    m0 = tuple(jnp.full((1, tq), NEG_INF, F32) for _ in range(4))
    l0 = tuple(jnp.zeros((1, tq), F32) for _ in range(4))
    _, ls = lax.fori_loop(0, nkt, body, (m0, l0))
    lam = lam_ref[0]
    for h in range(2):
        o = acc_ref[2 * h] / ls[2 * h] - lam * (acc_ref[2 * h + 1] / ls[2 * h + 1])
        y = o * lax.rsqrt(jnp.mean(o * o, axis=0, keepdims=True) + NORM_EPS)
        y = y * g_ref[64 * h:64 * h + 64, :] * out_scale
        o_ref[64 * h:64 * h + 64, :] = y.astype(o_ref.dtype)


def _diff_attention(q, k_all, v_all, lam, subln_g, out_scale):
    bsz, tq_all, _ = q.shape
    tk_all = k_all.shape[1]
    tq = min(DA_TQ, tq_all)
    tk = DA_TK
    nkt = tk_all // tk
    qt = q.reshape(bsz, tq_all, 4, LANES).transpose(0, 2, 3, 1)
    vt = v_all.reshape(bsz, nkt, tk, 4, LANES).transpose(0, 3, 1, 4, 2)
    gcol = jnp.tile(subln_g.astype(F32), 2).reshape(LANES, 1)
    ot = pl.pallas_call(
        functools.partial(_da_kernel, tk=tk, nkt=nkt, out_scale=out_scale),
        grid=(bsz, 4, tq_all // tq),
        in_specs=[pl.BlockSpec(memory_space=pltpu.SMEM),
                  pl.BlockSpec((None, None, LANES, tq), lambda b, hp, i: (b, hp, 0, i)),
                  pl.BlockSpec((None, tk_all, LANES), lambda b, hp, i: (b, 0, hp)),
                  pl.BlockSpec((None, None, nkt, LANES, tk), lambda b, hp, i: (b, hp, 0, 0, 0)),
                  pl.BlockSpec((LANES, 1), lambda b, hp, i: (0, 0))],
        out_specs=pl.BlockSpec((None, None, LANES, tq), lambda b, hp, i: (b, hp, 0, i)),
        out_shape=jax.ShapeDtypeStruct((bsz, 4, LANES, tq_all), BF16),
        scratch_shapes=[pltpu.VMEM((4, 64, tq), F32)],
        compiler_params=_params("parallel", "parallel", "arbitrary"),
        name="diff_attention",
    )(lam.reshape(1).astype(F32), qt, k_all, vt, gcol)
    return ot.transpose(0, 3, 1, 2).reshape(bsz, tq_all, 512)


def _wa_kernel(*refs, has_window):
    if has_window:
        (qt_ref, kc_ref, vct_ref, sink_ref, kp_ref, km_ref, kn_ref, vp_ref, vm_ref, vn_ref, o_ref) = refs
    else:
        (qt_ref, kc_ref, vct_ref, sink_ref, o_ref) = refs
    i = pl.program_id(1)
    nblk = pl.num_programs(1)
    zeros = jnp.zeros((WA_DIM, 4 * LANES), BF16)
    if has_window:
        r = lax.broadcasted_iota(jnp.int32, (LANES, 4 * LANES), 0)
        c = jnp.bitwise_and(lax.broadcasted_iota(jnp.int32, (LANES, 4 * LANES), 1), LANES - 1)
        off_p = jnp.where(i > 0, 0, 2 * LANES)
        off_n = jnp.where(i < nblk - 1, 0, 2 * LANES)
        mask_p = r >= c + off_p
        mask_n = r + off_n <= c
    for h in range(WA_KV_HEADS):
        q = qt_ref[h]
        w = jnp.concatenate([q, zeros] if h == 0 else [zeros, q], axis=0)
        hs = slice(WA_DIM * h, WA_DIM * h + WA_DIM)
        sink = sink_ref[h]
        parts = [jnp.dot(kc_ref[...], w, preferred_element_type=F32)]
        vts = [vct_ref[hs, :]]
        if has_window:
            sp = jnp.dot(kp_ref[...], w, preferred_element_type=F32)
            sm = jnp.dot(km_ref[...], w, preferred_element_type=F32)
            sn = jnp.dot(kn_ref[...], w, preferred_element_type=F32)
            parts += [jnp.where(mask_p, sp, NEG_INF), sm, jnp.where(mask_n, sn, NEG_INF)]
            vts += [vp_ref[hs, :], vm_ref[hs, :], vn_ref[hs, :]]
        m = sink
        for s in parts:
            m = jnp.maximum(m, jnp.max(s, axis=0, keepdims=True))
        den = jnp.exp(sink - m)
        o = jnp.zeros((WA_DIM, 4 * LANES), F32)
        for s, vt in zip(parts, vts):
            p = jnp.exp(s - m)
            den = den + jnp.sum(p, axis=0, keepdims=True)
            o = o + jnp.dot(vt, p.astype(BF16), preferred_element_type=F32)
        o_ref[h] = (o / den).astype(o_ref.dtype)


def _window_attention(q, kc, vc, sink, kl=None, vl=None):
    bsz, t, _ = q.shape
    nblk = t // LANES
    n_ctx = kc.shape[1]
    has_window = kl is not None
    qt = q.reshape(bsz, nblk, LANES, WA_KV_HEADS, WA_GROUP, WA_DIM).transpose(0, 3, 1, 5, 4, 2)
    qt = qt.reshape(bsz, WA_KV_HEADS, nblk, WA_DIM, WA_GROUP * LANES)
    vct = vc.transpose(0, 2, 1)
    sink_rows = jnp.repeat(sink.astype(F32).reshape(WA_KV_HEADS, 1, WA_GROUP), LANES, axis=-1)
    qspec = pl.BlockSpec((None, WA_KV_HEADS, None, WA_DIM, WA_GROUP * LANES),
                         lambda b, i: (b, 0, i, 0, 0))
    in_specs = [qspec,
                pl.BlockSpec((None, n_ctx, LANES), lambda b, i: (b, 0, 0)),
                pl.BlockSpec((None, LANES, n_ctx), lambda b, i: (b, 0, 0)),
                pl.BlockSpec((WA_KV_HEADS, 1, WA_GROUP * LANES), lambda b, i: (0, 0, 0))]
    args = [qt, kc, vct, sink_rows]
    if has_window:
        prev = lambda b, i: (b, jnp.maximum(i - 1, 0), 0)
        here = lambda b, i: (b, i, 0)
        nxt = lambda b, i: (b, jnp.minimum(i + 1, nblk - 1), 0)
        in_specs += [pl.BlockSpec((None, LANES, LANES), f) for f in (prev, here, nxt)]
        tprev = lambda b, i: (b, 0, jnp.maximum(i - 1, 0))
        there = lambda b, i: (b, 0, i)
        tnxt = lambda b, i: (b, 0, jnp.minimum(i + 1, nblk - 1))
        in_specs += [pl.BlockSpec((None, LANES, LANES), f) for f in (tprev, there, tnxt)]
        vlt = vl.transpose(0, 2, 1)
        args += [kl, kl, kl, vlt, vlt, vlt]
    ot = pl.pallas_call(
        functools.partial(_wa_kernel, has_window=has_window),
        grid=(bsz, nblk),
        in_specs=in_specs,
        out_specs=qspec,
        out_shape=jax.ShapeDtypeStruct(qt.shape, BF16),
        compiler_params=_params("parallel", "parallel"),
        name="window_attention" if has_window else "context_gqa",
    )(*args)
    o = ot.reshape(bsz, WA_KV_HEADS, nblk, WA_DIM, WA_GROUP, LANES).transpose(0, 2, 5, 1, 4, 3)
    return o.reshape(bsz, t, 512)


def _att_out_kernel(x_ref, oa_ref, ob_ref, w_ref, gate_ref, o_ref):
    o = jnp.dot(oa_ref[...], w_ref[0:512, :], preferred_element_type=F32)
    o = o + jnp.dot(ob_ref[...], w_ref[512:1024, :], preferred_element_type=F32)
    o_ref[...] = x_ref[...] + gate_ref[...] * o


def _att_out(x, oa, ob, w_out, gate):
    bsz, t, d = x.shape
    tm = min(ROW_TILE, t)
    row = lambda w: pl.BlockSpec((None, tm, w), lambda b, i: (b, i, 0))
    return pl.pallas_call(
        _att_out_kernel,
        grid=(bsz, t // tm),
        in_specs=[row(d), row(512), row(512), _resident(w_out.shape),
                  pl.BlockSpec((None, 1, d), lambda b, i: (b, 0, 0))],
        out_specs=row(d),
        out_shape=jax.ShapeDtypeStruct(x.shape, F32),
        compiler_params=_params("parallel", "parallel"),
        name="att_out",
    )(x, oa, ob, w_out, gate)


def _ffn_kernel(xm_ref, xp_ref, xn_ref, g_ref, sh_ref, sc_ref, gate_ref, wg_ref, wu_ref, cw_ref,
                wd_ref, o_ref, gs_ref):
    i = pl.program_id(1)
    first = i == 0
    last = i == pl.num_programs(1) - 1
    tm = xm_ref.shape[0]
    xm = xm_ref[...]
    g, sh, sc = g_ref[...], sh_ref[...], sc_ref[...]
    hm = _modulated(xm, g, sh, sc)
    hh = _modulated(jnp.concatenate([xp_ref[...], xn_ref[...]], axis=0), g, sh, sc)
    acc = jnp.zeros((tm, xm.shape[1]), F32)
    for c in range(D_FF // FFN_CHUNK):
        cs = slice(c * FFN_CHUNK, (c + 1) * FFN_CHUNK)
        gm = jnp.dot(hm, wg_ref[:, cs], preferred_element_type=F32)
        gh = jnp.dot(hh, wg_ref[:, cs], preferred_element_type=F32)
        gs_ref[0:SUBLANES, :] = jnp.where(first, 0.0, gh[0:SUBLANES, :])
        gs_ref[SUBLANES:SUBLANES + tm, :] = gm
        gs_ref[SUBLANES + tm:2 * SUBLANES + tm, :] = jnp.where(last, 0.0, gh[SUBLANES:, :])
        cw = cw_ref[:, cs]
        gc = (cw[0:1, :] * gs_ref[SUBLANES - 1:SUBLANES - 1 + tm, :] + cw[1:2, :] * gm
              + cw[2:3, :] * gs_ref[SUBLANES + 1:SUBLANES + 1 + tm, :])
        u = jnp.dot(hm, wu_ref[:, cs], preferred_element_type=F32)
        act = (gc * _sigmoid(gc) * u).astype(BF16)
        acc = acc + jnp.dot(act, wd_ref[cs, :], preferred_element_type=F32)
    o_ref[...] = xm + gate_ref[...] * acc


def _ffn(x, g, sh, sc, gate, w_g, w_u, conv_w, w_d):
    bsz, t, d = x.shape
    tm = min(ROW_TILE, t)
    nb = tm // SUBLANES
    last_blk = t // SUBLANES - 1
    vec = pl.BlockSpec((None, 1, d), lambda b, i: (b, 0, 0))
    return pl.pallas_call(
        _ffn_kernel,
        grid=(bsz, t // tm),
        in_specs=[pl.BlockSpec((None, tm, d), lambda b, i: (b, i, 0)),
                  pl.BlockSpec((None, SUBLANES, d), lambda b, i: (b, jnp.maximum(i * nb - 1, 0), 0)),
                  pl.BlockSpec((None, SUBLANES, d),
                               lambda b, i: (b, jnp.minimum((i + 1) * nb, last_blk), 0)),
                  pl.BlockSpec((1, d), lambda b, i: (0, 0)), vec, vec, vec,
                  _resident(w_g.shape), _resident(w_u.shape), _resident(conv_w.shape),
                  _resident(w_d.shape)],
        out_specs=pl.BlockSpec((None, tm, d), lambda b, i: (b, i, 0)),
        out_shape=jax.ShapeDtypeStruct(x.shape, F32),
        scratch_shapes=[pltpu.VMEM((tm + 2 * SUBLANES, FFN_CHUNK), F32)],
        compiler_params=_params("parallel", "parallel"),
        name="conv_ffn",
    )(x, x, x, g, sh, sc, gate, w_g, w_u, conv_w, w_d)


def _rec_inproj_kernel(x_ref, g_ref, sh_ref, sc_ref, w_ref, gl_ref, rl_ref, ul_ref):
    h = _modulated(x_ref[...], g_ref[...], sh_ref[...], sc_ref[...])
    z = jnp.dot(h, w_ref[...], preferred_element_type=F32)
    gl_ref[...] = z[:, 0:LRU_WIDTH]
    rl_ref[...] = z[:, LRU_WIDTH:2 * LRU_WIDTH]
    ul_ref[...] = z[:, 2 * LRU_WIDTH:]


def _rec_inproj(x, g, sh, sc, w_in):
    bsz, t, d = x.shape
    tm = min(ROW_TILE, t)
    row = lambda w: pl.BlockSpec((None, tm, w), lambda b, i: (b, i, 0))
    vec = pl.BlockSpec((None, 1, d), lambda b, i: (b, 0, 0))
    outs = [LRU_WIDTH, LRU_WIDTH, S5_WIDTH]
    return pl.pallas_call(
        _rec_inproj_kernel,
        grid=(bsz, t // tm),
        in_specs=[row(d), pl.BlockSpec((1, d), lambda b, i: (0, 0)), vec, vec, _resident(w_in.shape)],
        out_specs=[row(w) for w in outs],
        out_shape=[jax.ShapeDtypeStruct((bsz, t, w), F32) for w in outs],
        compiler_params=_params("parallel", "parallel"),
        name="rec_inproj",
    )(x, g, sh, sc, w_in)


def _lru_direction(xm_ref, xp_ref, xn_ref, at_start, at_end, cw_ref, cb_ref, w_ref, bias_ref, nsp,
                   out_ref, carry_ref, xs_ref, a_ref, b_ref, reverse):
    tm = xm_ref.shape[0]
    xs_ref[0:SUBLANES, :] = jnp.where(at_start, 0.0, xp_ref[...])
    xs_ref[SUBLANES:SUBLANES + tm, :] = xm_ref[...]
    xs_ref[SUBLANES + tm:2 * SUBLANES + tm, :] = jnp.where(at_end, 0.0, xn_ref[...])
    cw = cw_ref[...]
    rc = cb_ref[...] + cw[2:3, :] * xm_ref[...]
    for k in (0, 1, 3):
        rc = rc + cw[k:k + 1, :] * xs_ref[SUBLANES - 2 + k:SUBLANES - 2 + k + tm, :]
    gz = jnp.dot(rc.astype(BF16), w_ref[...], preferred_element_type=F32) + bias_ref[...]
    r = _sigmoid(gz[:, 0:LRU_WIDTH])
    ig = _sigmoid(gz[:, LRU_WIDTH:])
    a = jnp.exp(nsp * r)
    b = jnp.sqrt(1.0 - a * a) * (ig * rc)
    row8 = jnp.bitwise_and(lax.broadcasted_iota(jnp.int32, (tm, LRU_WIDTH), 0), SUBLANES - 1)
    for s in (1, 2, 4):
        if reverse:
            keep = row8 <= SUBLANES - 1 - s
            shift = tm - s
        else:
            keep = row8 >= s
            shift = s
        a_sh = jnp.where(keep, pltpu.roll(a, shift, 0), 1.0)
        b_sh = jnp.where(keep, pltpu.roll(b, shift, 0), 0.0)
        b = b + a * b_sh
        a = a * a_sh
    a_ref[...] = a
    b_ref[...] = b
    ngroups = tm // SUBLANES
    edge = 0 if reverse else SUBLANES - 1

    def body(j, carry):
        grp = (ngroups - 1 - j) if reverse else j
        off = pl.multiple_of(grp * SUBLANES, SUBLANES)
        h = b_ref[pl.ds(off, SUBLANES), :] + a_ref[pl.ds(off, SUBLANES), :] * carry
        out_ref[pl.ds(off, SUBLANES), :] = h
        return h[edge:edge + 1, :]

    carry_ref[...] = lax.fori_loop(0, ngroups, body, carry_ref[...])


def _lru_kernel(fm_ref, fp_ref, fn_ref, bm_ref, bp_ref, bn_ref, cw_ref, cb_ref, wf_ref, wb_ref,
                biasf_ref, biasb_ref, nsp_ref, h0_ref, hf_ref, hb_ref, hlast_ref,
                carry_f, carry_b, xs_ref, a_ref, b_ref):
    i = pl.program_id(1)
    first = i == 0
    last = i == pl.num_programs(1) - 1

    @pl.when(first)
    def _():
        carry_f[...] = h0_ref[0]
        carry_b[...] = h0_ref[1]

    _lru_direction(fm_ref, fp_ref, fn_ref, first, last, cw_ref, cb_ref, wf_ref, biasf_ref, nsp_ref[0],
                   hf_ref, carry_f, xs_ref, a_ref, b_ref, reverse=False)
    _lru_direction(bm_ref, bp_ref, bn_ref, last, first, cw_ref, cb_ref, wb_ref, biasb_ref, nsp_ref[1],
                   hb_ref, carry_b, xs_ref, a_ref, b_ref, reverse=True)
    hlast_ref[0] = carry_f[...]
    hlast_ref[1] = carry_b[...]


def _lru(rl, conv_w, conv_b, w_f, w_b, bias_f, bias_b, nsp, h0):
    bsz, t, c = rl.shape
    tm = min(ROW_TILE, t)
    nt = t // tm
    nb = tm // SUBLANES
    last_blk = t // SUBLANES - 1
    fwd = lambda b, i: (b, i, 0)
    bwd = lambda b, i: (b, nt - 1 - i, 0)
    fprev = lambda b, i: (b, jnp.maximum(i * nb - 1, 0), 0)
    fnext = lambda b, i: (b, jnp.minimum((i + 1) * nb, last_blk), 0)
    bprev = lambda b, i: (b, jnp.maximum((nt - 1 - i) * nb - 1, 0), 0)
    bnext = lambda b, i: (b, jnp.minimum((nt - i) * nb, last_blk), 0)
    main = lambda f: pl.BlockSpec((None, tm, c), f)
    halo = lambda f: pl.BlockSpec((None, SUBLANES, c), f)
    state = pl.BlockSpec((None, 2, 1, c), lambda b, i: (b, 0, 0, 0))
    return pl.pallas_call(
        _lru_kernel,
        grid=(bsz, nt),
        in_specs=[main(fwd), halo(fprev), halo(fnext), main(bwd), halo(bprev), halo(bnext),
                  _resident(conv_w.shape), _resident(conv_b.shape), _resident(w_f.shape),
                  _resident(w_b.shape), _resident(bias_f.shape), _resident(bias_b.shape),
                  _resident(nsp.shape), state],
        out_specs=[main(fwd), main(bwd), state],
        out_shape=[jax.ShapeDtypeStruct(rl.shape, F32), jax.ShapeDtypeStruct(rl.shape, F32),
                   jax.ShapeDtypeStruct((bsz, 2, 1, c), F32)],
        scratch_shapes=[pltpu.VMEM((1, c), F32), pltpu.VMEM((1, c), F32),
                        pltpu.VMEM((tm + 2 * SUBLANES, c), F32),
                        pltpu.VMEM((tm, c), F32), pltpu.VMEM((tm, c), F32)],
        compiler_params=_params("parallel", "arbitrary"),
        name="rglru",
    )(rl, rl, rl, rl, rl, rl, conv_w, conv_b, w_f, w_b, bias_f, bias_b, nsp, h0)


def _s5_in_kernel(u_ref, t_ref, bre_ref, bim_ref, yi_ref, xre_ref, xim_ref):
    xre = jnp.zeros(xre_ref.shape, F32)
    xim = jnp.zeros(xim_ref.shape, F32)
    for q in range(2):
        ub = u_ref[q].astype(BF16)
        yi_ref[q] = jnp.dot(ub, t_ref[q], preferred_element_type=F32)
        xre = xre + jnp.dot(ub, bre_ref[q], preferred_element_type=F32)
        xim = xim + jnp.dot(ub, bim_ref[q], preferred_element_type=F32)
    xre_ref[...] = xre
    xim_ref[...] = xim


def _s5_scan_kernel(xre_ref, xim_ref, h0re_ref, h0im_ref, shr_ref, shi_ref, cwr_ref, cwi_ref,
                    sre_ref, sim_ref, lre_ref, lim_ref, pre_ref, pim_ref):
    nc, n = xre_ref.shape
    xr = xre_ref[...]
    xi = xim_ref[...]
    row8 = jnp.bitwise_and(lax.broadcasted_iota(jnp.int32, (nc, n), 0), SUBLANES - 1)
    for k, s in enumerate((1, 2, 4)):
        keep = row8 >= s
        xr_sh = jnp.where(keep, pltpu.roll(xr, s, 0), 0.0)
        xi_sh = jnp.where(keep, pltpu.roll(xi, s, 0), 0.0)
        mr = shr_ref[k]
        mi = shi_ref[k]
        xr, xi = xr + (mr * xr_sh - mi * xi_sh), xi + (mr * xi_sh + mi * xr_sh)
    pre_ref[...] = xr
    pim_ref[...] = xi
    cwr = cwr_ref[...]
    cwi = cwi_ref[...]
    row = lax.broadcasted_iota(jnp.int32, (SUBLANES, n), 0)

    def body(j, carry):
        cr, ci = carry
        off = pl.multiple_of(j * SUBLANES, SUBLANES)
        sr = pre_ref[pl.ds(off, SUBLANES), :] + (cwr * cr - cwi * ci)
        si = pim_ref[pl.ds(off, SUBLANES), :] + (cwr * ci + cwi * cr)
        pre_ref[pl.ds(off, SUBLANES), :] = jnp.where(row == 0, cr, pltpu.roll(sr, 1, 0))
        pim_ref[pl.ds(off, SUBLANES), :] = jnp.where(row == 0, ci, pltpu.roll(si, 1, 0))
        return sr[SUBLANES - 1:SUBLANES, :], si[SUBLANES - 1:SUBLANES, :]

    cr, ci = lax.fori_loop(0, nc // SUBLANES, body, (h0re_ref[...], h0im_ref[...]))
    lre_ref[...] = cr
    lim_ref[...] = ci
    sre_ref[...] = pre_ref[...].astype(BF16)
    sim_ref[...] = pim_ref[...].astype(BF16)


def _s5_out_kernel(yi_ref, u_ref, sre_ref, sim_ref, cre_ref, cim_ref, d_ref, y_ref):
    sre = sre_ref[...]
    sim = sim_ref[...]
    for q in range(2):
        y = yi_ref[q] + d_ref[q] * u_ref[q]
        y = y + jnp.dot(sre, cre_ref[q], preferred_element_type=F32)
        y = y + jnp.dot(sim, cim_ref[q], preferred_element_type=F32)
        y_ref[q] = y


def _s5(u, prm, h0re, h0im):
    bsz, t, _ = u.shape
    nc = t // S5_CHUNK
    ug = u.reshape(bsz, nc, S5_CHUNK, S5_GROUPS, S5_GROUP).transpose(0, 3, 1, 2, 4)
    ug = ug.reshape(bsz, S5_GROUPS, nc, 256)
    ub = jnp.flip(u, axis=1).reshape(bsz, nc, S5_CHUNK, S5_GROUPS, S5_GROUP).transpose(0, 3, 1, 2, 4)
    ub = ub.reshape(bsz, S5_GROUPS, nc, 256)
    ud = jnp.stack([ug, ub], axis=1)
    npair = S5_GROUPS // 2
    pair5 = lambda w: pl.BlockSpec((None, None, 2, nc, w), lambda b, d, p: (b, d, p, 0, 0))
    wpair = lambda r, w: pl.BlockSpec((None, 2, r, w), lambda b, d, p: (d, p, 0, 0))
    lanes = pl.BlockSpec((None, None, nc, LANES), lambda b, d, p: (b, d, 0, p))
    yi, xre, xim = pl.pallas_call(
        _s5_in_kernel,
        grid=(bsz, 2, npair),
        in_specs=[pair5(256), wpair(256, 256), wpair(256, LANES), wpair(256, LANES)],
        out_specs=[pair5(256), lanes, lanes],
        out_shape=[jax.ShapeDtypeStruct(ud.shape, F32),
                   jax.ShapeDtypeStruct((bsz, 2, nc, S5_LANES), F32),
                   jax.ShapeDtypeStruct((bsz, 2, nc, S5_LANES), F32)],
        compiler_params=_params("parallel", "parallel", "parallel"),
        name="s5_in",
    )(ud, prm["toep"], prm["bs_re"], prm["bs_im"])

    full = pl.BlockSpec((None, None, nc, S5_LANES), lambda b, d: (b, d, 0, 0))
    one = pl.BlockSpec((None, None, 1, S5_LANES), lambda b, d: (b, d, 0, 0))
    tab = lambda r: pl.BlockSpec((None, r, 1, S5_LANES), lambda b, d: (d, 0, 0, 0))
    tab8 = pl.BlockSpec((None, SUBLANES, S5_LANES), lambda b, d: (d, 0, 0))
    sre, sim, lre, lim = pl.pallas_call(
        _s5_scan_kernel,
        grid=(bsz, 2),
        in_specs=[full, full, one, one, tab(3), tab(3), tab8, tab8],
        out_specs=[full, full, one, one],
        out_shape=[jax.ShapeDtypeStruct((bsz, 2, nc, S5_LANES), BF16),
                   jax.ShapeDtypeStruct((bsz, 2, nc, S5_LANES), BF16),
                   jax.ShapeDtypeStruct((bsz, 2, 1, S5_LANES), F32),
                   jax.ShapeDtypeStruct((bsz, 2, 1, S5_LANES), F32)],
        scratch_shapes=[pltpu.VMEM((nc, S5_LANES), F32), pltpu.VMEM((nc, S5_LANES), F32)],
        compiler_params=_params("parallel", "parallel"),
        name="s5_scan",
    )(xre, xim, h0re, h0im, prm["sh_r"], prm["sh_i"], prm["cw_r"], prm["cw_i"])

    y = pl.pallas_call(
        _s5_out_kernel,
        grid=(bsz, 2, npair),
        in_specs=[pair5(256), pair5(256), lanes, lanes, wpair(LANES, 256), wpair(LANES, 256),
                  pl.BlockSpec((None, 2, 1, 256), lambda b, d, p: (d, p, 0, 0))],
        out_specs=pair5(256),
        out_shape=jax.ShapeDtypeStruct(ud.shape, F32),
        compiler_params=_params("parallel", "parallel", "parallel"),
        name="s5_out",
    )(yi, ud, sre, sim, prm["cs_re"], prm["cs_im"], prm["dskip"])
    y = y.reshape(bsz, 2, S5_GROUPS, nc, S5_CHUNK, S5_GROUP).transpose(0, 1, 3, 4, 2, 5)
    y = y.reshape(bsz, 2, t, S5_WIDTH)
    return y[:, 0], jnp.flip(y[:, 1], axis=1), lre, lim


def _s5_prepare(a_re, a_im, log_step, b_re, b_im, c_re, c_im, d_skip):
    a_re, a_im = a_re.astype(F32), a_im.astype(F32)
    step = jnp.exp(log_step.astype(F32))[..., None]
    mag = jnp.exp(a_re * step)
    abr, abi = mag * jnp.cos(a_im * step), mag * jnp.sin(a_im * step)
    den = a_re * a_re + a_im * a_im
    qr = ((abr - 1.0) * a_re + abi * a_im) / den
    qi = (abi * a_re - (abr - 1.0) * a_im) / den
    br_, bi_ = b_re.astype(F32), b_im.astype(F32)
    bbr = qr[..., None] * br_ - qi[..., None] * bi_
    bbi = qr[..., None] * bi_ + qi[..., None] * br_
    cr, ci = c_re.astype(F32), c_im.astype(F32)

    def power(n):
        n = jnp.asarray(n, F32)[..., None, None, None]
        m = jnp.exp(n * a_re * step)
        return m * jnp.cos(n * a_im * step), m * jnp.sin(n * a_im * step)

    lc = S5_CHUNK
    pr, pi = power(jnp.arange(lc + 1))
    mr = pr[..., None] * bbr - pi[..., None] * bbi
    mi = pr[..., None] * bbi + pi[..., None] * bbr
    hp = lax.Precision.HIGHEST
    kern = (jnp.einsum('dgop,jdgpi->jdgoi', cr, mr[:lc], precision=hp)
            - jnp.einsum('dgop,jdgpi->jdgoi', ci, mi[:lc], precision=hp))
    s_idx = jnp.arange(lc)[:, None]
    r_idx = jnp.arange(lc)[None, :]
    lag = jnp.clip(r_idx - s_idx, 0, lc - 1)
    toep = jnp.where((r_idx >= s_idx)[:, :, None, None, None, None], kern[lag], 0.0)
    toep = toep.transpose(2, 3, 0, 5, 1, 4).reshape(2, S5_GROUPS, lc * S5_GROUP, lc * S5_GROUP)
    rev = jnp.arange(lc - 1, -1, -1)
    bs_re = mr[rev].transpose(1, 2, 0, 4, 3).reshape(2, S5_GROUPS, lc * S5_GROUP, S5_STATE)
    bs_im = mi[rev].transpose(1, 2, 0, 4, 3).reshape(2, S5_GROUPS, lc * S5_GROUP, S5_STATE)
    pr1, pi1 = pr[1:], pi[1:]
    cs_re = cr[None] * pr1[:, :, :, None, :] - ci[None] * pi1[:, :, :, None, :]
    cs_im = -(cr[None] * pi1[:, :, :, None, :] + ci[None] * pr1[:, :, :, None, :])
    cs_re = cs_re.transpose(1, 2, 4, 0, 3).reshape(2, S5_GROUPS, S5_STATE, lc * S5_GROUP)
    cs_im = cs_im.transpose(1, 2, 4, 0, 3).reshape(2, S5_GROUPS, S5_STATE, lc * S5_GROUP)
    half = (jnp.arange(S5_GROUPS) % 2)[None, :, None, None]

    def pad_cols(m):
        z = jnp.zeros_like(m)
        return jnp.where(half == 0, jnp.concatenate([m, z], -1), jnp.concatenate([z, m], -1))

    def pad_rows(m):
        z = jnp.zeros_like(m)
        return jnp.where(half == 0, jnp.concatenate([m, z], -2), jnp.concatenate([z, m], -2))

    flat = lambda x: x.reshape(x.shape[:-2] + (S5_LANES,))
    shr, shi = power(jnp.asarray([lc, 2 * lc, 4 * lc]))
    cwr, cwi = power(lc * jnp.arange(1, SUBLANES + 1))
    dsk = jnp.tile(d_skip.astype(F32).reshape(S5_GROUPS, 1, S5_GROUP), (1, 1, lc))
    return {
        "toep": toep.astype(BF16),
        "bs_re": pad_cols(bs_re).astype(BF16), "bs_im": pad_cols(bs_im).astype(BF16),
        "cs_re": pad_rows(cs_re).astype(BF16), "cs_im": pad_rows(cs_im).astype(BF16),
        "sh_r": flat(shr).transpose(1, 0, 2)[:, :, None, :], "sh_i": flat(shi).transpose(1, 0, 2)[:, :, None, :],
        "cw_r": flat(cwr).transpose(1, 0, 2), "cw_i": flat(cwi).transpose(1, 0, 2),
        "dskip": jnp.stack([dsk, jnp.zeros_like(dsk)], axis=0),
    }


def _rec_out_kernel(x_ref, gl_ref, hf_ref, hb_ref, yf_ref, yb_ref, wglu_ref, w_ref, gate_ref, o_ref):
    t1 = ((hf_ref[...] + hb_ref[...]) * _gelu(gl_ref[...])).astype(BF16)
    yg = _gelu(yf_ref[...] + yb_ref[...])
    t2 = (yg * _sigmoid(jnp.dot(yg.astype(BF16), wglu_ref[...], preferred_element_type=F32))).astype(BF16)
    o = jnp.dot(t1, w_ref[0:LRU_WIDTH, :], preferred_element_type=F32)
    o = o + jnp.dot(t2, w_ref[LRU_WIDTH:, :], preferred_element_type=F32)
    o_ref[...] = x_ref[...] + gate_ref[...] * o


def _rec_out(x, gl, hf, hb, yf, yb, w_glu, w_out, gate):
    bsz, t, d = x.shape
    tm = min(ROW_TILE, t)
    row = lambda w: pl.BlockSpec((None, tm, w), lambda b, i: (b, i, 0))
    return pl.pallas_call(
        _rec_out_kernel,
        grid=(bsz, t // tm),
        in_specs=[row(d), row(LRU_WIDTH), row(LRU_WIDTH), row(LRU_WIDTH), row(S5_WIDTH), row(S5_WIDTH),
                  _resident(w_glu.shape), _resident(w_out.shape),
                  pl.BlockSpec((None, 1, d), lambda b, i: (b, 0, 0))],
        out_specs=row(d),
        out_shape=jax.ShapeDtypeStruct(x.shape, F32),
        compiler_params=_params("parallel", "parallel"),
        name="rec_out",
    )(x, gl, hf, hb, yf, yb, w_glu, w_out, gate)


def _final_norm_kernel(x_ref, g_ref, o_ref):
    x = x_ref[...]
    o_ref[...] = x * lax.rsqrt(jnp.mean(x * x, axis=-1, keepdims=True) + NORM_EPS) * g_ref[...]


def _final_norm(x, g):
    bsz, t, d = x.shape
    tm = min(ROW_TILE, t)
    return pl.pallas_call(
        _final_norm_kernel,
        grid=(bsz, t // tm),
        in_specs=[pl.BlockSpec((None, tm, d), lambda b, i: (b, i, 0)),
                  pl.BlockSpec((1, d), lambda b, i: (0, 0))],
        out_specs=pl.BlockSpec((None, tm, d), lambda b, i: (b, i, 0)),
        out_shape=jax.ShapeDtypeStruct(x.shape, F32),
        compiler_params=_params("parallel", "parallel"),
        name="final_norm",
    )(x, g.reshape(1, d).astype(F32))


def _rope_tables(n, dim):
    rows = n // GRID_W
    row = jnp.repeat(jnp.arange(rows), GRID_W).astype(F32)
    col = jnp.tile(jnp.arange(GRID_W), rows).astype(F32)
    half = dim // 2
    freqs = ROPE_THETA ** (-jnp.arange(0, half, 2, dtype=F32) / half)

    def angles(pos):
        a = pos[:, None] * freqs[None, :]
        return jnp.concatenate([a, a], axis=-1)

    ang = jnp.concatenate([angles(row), angles(col)], axis=-1)
    reps = LANES // dim
    return jnp.tile(jnp.cos(ang), (1, reps)), jnp.tile(jnp.sin(ang), (1, reps))


def _rotated_columns(w, dim):
    quarter = dim // 4
    idx = jnp.arange(w.shape[1])
    low = (idx % (2 * quarter)) < quarter
    src = jnp.where(low, idx + quarter, idx - quarter)
    return jnp.where(low[None, :], -1.0, 1.0) * w[:, src]


def _att_weights(w_in):
    aq, ak = w_in[:, 0:512], w_in[:, 512:1024]
    bq, bk = w_in[:, 1536:2048], w_in[:, 2048:2176]
    ext = [w_in, _rotated_columns(aq, DA_DIM), _rotated_columns(ak, DA_DIM),
           _rotated_columns(bq, WA_DIM), _rotated_columns(bk, WA_DIM)]
    return jnp.concatenate(ext, axis=1).astype(BF16)


def _block_diag(w):
    eye = jnp.eye(LRU_BLOCKS, dtype=w.dtype)
    return jnp.einsum('nij,nm->nimj', w, eye).reshape(LRU_WIDTH, LRU_WIDTH)


def _vec3(v, bsz):
    return v.reshape(-1, 1, v.shape[-1]) if v.shape[0] == bsz else jnp.broadcast_to(v[None], (bsz, 1, v.shape[-1]))


def kernel(x, c, ctx, c_ctx, ada_w, ada_b, norm1_g, norm2_g, att_w_in, att_w_out, da_lam_q1, da_lam_k1, da_lam_q2, da_lam_k2, da_subln_g, wa_sink, rec_w_in, rec_w_out, lru_conv_w, lru_conv_b, lru_w_a, lru_b_a, lru_w_x, lru_b_x, lru_lam, s5_a_re, s5_a_im, s5_log_step, s5_b_re, s5_b_im, s5_c_re, s5_c_im, s5_d, s5_w_glu, ffn_w_g, ffn_w_u, ffn_conv_w, ffn_w_down, final_g):
    bsz, n_lat, d = x.shape
    n_ctx = ctx.shape[1]
    assert d == D_MODEL and bsz + 1 <= SUBLANES
    assert n_lat % ROW_TILE == 0 and n_ctx % DA_TK == 0 and n_ctx <= ROW_TILE
    x = x.astype(F32)
    ctx = ctx.astype(F32)

    cond = jnp.zeros((SUBLANES, d), F32).at[:bsz].set(c.astype(F32)).at[bsz].set(c_ctx.astype(F32))
    mods = _ada_all(cond, ada_w.astype(F32), ada_b.astype(F32))

    ca, sa = _rope_tables(n_lat, DA_DIM)
    cb, sb = _rope_tables(n_lat, WA_DIM)
    ones = jnp.ones((n_ctx, LANES), F32)
    zeros = jnp.zeros((n_ctx, LANES), F32)

    for l in range(DEPTH):
        ctx_out = l < DEPTH - 1
        m = mods[l]
        sh1, sc1, g1, sh2, sc2, g2 = [_vec3(v, bsz) for v in jnp.split(m[:bsz], 6, axis=-1)]
        csh1, csc1, cg1, csh2, csc2, cg2 = [_vec3(v, bsz) for v in jnp.split(m[bsz:bsz + 1], 6, axis=-1)]
        n1 = norm1_g[l].reshape(1, d).astype(F32)
        n2 = norm2_g[l].reshape(1, d).astype(F32)
        j = l // 2
        if l % 2 == 0:
            lam_init = 0.8 - 0.6 * math.exp(-0.3 * l)
            w_ext = _att_weights(att_w_in[j].astype(F32))
            w_out = att_w_out[j].astype(BF16)
            lam = (jnp.exp(jnp.sum(da_lam_q1[j].astype(F32) * da_lam_k1[j].astype(F32)))
                   - jnp.exp(jnp.sum(da_lam_q2[j].astype(F32) * da_lam_k2[j].astype(F32))) + lam_init)
            aql, akl, avl, bql, bkl, bvl = _att_inproj(x, n1, sh1, sc1, w_ext, ca, sa, cb, sb)
            aqc, akc, avc, bqc, bkc, bvc = _att_inproj(ctx, n1, csh1, csc1, w_ext, ones, zeros, ones, zeros)
            k_all = jnp.concatenate([akc, akl], axis=1)
            v_all = jnp.concatenate([avc, avl], axis=1)
            oal = _diff_attention(aql, k_all, v_all, lam, da_subln_g[j], 1.0 - lam_init)
            obl = _window_attention(bql, bkc, bvc, wa_sink[j], bkl, bvl)
            x = _att_out(x, oal, obl, w_out, g1)
            if ctx_out:
                oac = _diff_attention(aqc, akc, avc, lam, da_subln_g[j], 1.0 - lam_init)
                obc = _window_attention(bqc, bkc, bvc, wa_sink[j])
                ctx = _att_out(ctx, oac, obc, w_out, cg1)
        else:
            w_in = rec_w_in[j].astype(BF16)
            w_f = jnp.concatenate([_block_diag(lru_w_a[j, 0]), _block_diag(lru_w_x[j, 0])], axis=1).astype(BF16)
            w_b = jnp.concatenate([_block_diag(lru_w_a[j, 1]), _block_diag(lru_w_x[j, 1])], axis=1).astype(BF16)
            bias_f = jnp.concatenate([lru_b_a[j, 0], lru_b_x[j, 0]]).reshape(1, -1).astype(F32)
            bias_b = jnp.concatenate([lru_b_a[j, 1], lru_b_x[j, 1]]).reshape(1, -1).astype(F32)
            nsp = (-LRU_C * jax.nn.softplus(-lru_lam[j].astype(F32))).reshape(2, 1, LRU_WIDTH)
            conv_w = lru_conv_w[j].astype(F32)
            conv_b = lru_conv_b[j].reshape(1, -1).astype(F32)
            s5p = _s5_prepare(s5_a_re[j], s5_a_im[j], s5_log_step[j], s5_b_re[j], s5_b_im[j],
                              s5_c_re[j], s5_c_im[j], s5_d[j])
            gc_, rc_, uc_ = _rec_inproj(ctx, n1, csh1, csc1, w_in)
            gl_, rl_, ul_ = _rec_inproj(x, n1, sh1, sc1, w_in)
            h0 = jnp.zeros((bsz, 2, 1, LRU_WIDTH), F32)
            hfc, hbc, hlast = _lru(rc_, conv_w, conv_b, w_f, w_b, bias_f, bias_b, nsp, h0)
            hfl, hbl, _ = _lru(rl_, conv_w, conv_b, w_f, w_b, bias_f, bias_b, nsp, hlast)
            s0 = jnp.zeros((bsz, 2, 1, S5_LANES), F32)
            yfc, ybc, lre, lim = _s5(uc_, s5p, s0, s0)
            yfl, ybl, _, _ = _s5(ul_, s5p, lre, lim)
            w_glu = s5_w_glu[j].astype(BF16)
            w_out = rec_w_out[j].astype(BF16)
            x = _rec_out(x, gl_, hfl, hbl, yfl, ybl, w_glu, w_out, g1)
            if ctx_out:
                ctx = _rec_out(ctx, gc_, hfc, hbc, yfc, ybc, w_glu, w_out, cg1)
        wg, wu, wd = ffn_w_g[l].astype(BF16), ffn_w_u[l].astype(BF16), ffn_w_down[l].astype(BF16)
        cw = ffn_conv_w[l].astype(F32)
        x = _ffn(x, n2, sh2, sc2, g2, wg, wu, cw, wd)
        if ctx_out:
            ctx = _ffn(ctx, n2, csh2, csc2, cg2, wg, wu, cw, wd)
    return _final_norm(x, final_g)
```

```python
import functools
import math

import jax
import jax.numpy as jnp
from jax import lax
from jax.experimental import pallas as pl
from jax.experimental.pallas import tpu as pltpu

F32 = jnp.float32
BF16 = jnp.bfloat16

D_MODEL = 1024
DEPTH = 4
GRID_W = 64
ROPE_THETA = 10000.0
NORM_EPS = 1e-6
NEG_INF = -1e30
LOG2_E = math.log2(math.e)

DA_HEADS = 8
DA_DIM = 32
WA_HEADS = 8
WA_KV_HEADS = 2
WA_DIM = 64
WA_WINDOW = 128
WA_GROUP = WA_HEADS // WA_KV_HEADS
LRU_WIDTH = 768
LRU_BLOCKS = 8
LRU_BLOCK_DIM = LRU_WIDTH // LRU_BLOCKS
LRU_CONV = 4
LRU_C = 8.0
S5_WIDTH = 256
S5_GROUP = 16
S5_GROUPS = S5_WIDTH // S5_GROUP
S5_STATE = 64
S5_CHUNK = 16
S5_LANES = S5_GROUPS * S5_STATE
D_FF = 2816
FFN_CHUNK = 1408

DA_QK = DA_HEADS * 2 * DA_DIM
DA_V = DA_HEADS * 2 * DA_DIM
WA_Q = WA_HEADS * WA_DIM
WA_KV = WA_KV_HEADS * WA_DIM
ATT_IN = 2 * DA_QK + DA_V + WA_Q + 2 * WA_KV

SUBLANES = 8
LANES = 128
VMEM_LIMIT_BYTES = 56 * 1024 * 1024
ROW_TILE = 512
DA_TQ = 512
DA_TK = 256
DA_UNROLL = 5
DA_VROWS = 80


def _params(*sem):
    return pltpu.CompilerParams(dimension_semantics=sem, vmem_limit_bytes=VMEM_LIMIT_BYTES)


def _resident(shape):
    nd = len(shape)
    return pl.BlockSpec(shape, lambda *_: (0,) * nd, pipeline_mode=pl.Buffered(1))


def _sigmoid(x):
    return 1.0 / (1.0 + jnp.exp(-x))


def _gelu(x):
    return 0.5 * x * (1.0 + jnp.tanh(math.sqrt(2.0 / math.pi) * (x + 0.044715 * (x * x * x))))


def _modulated(x, g, shift, scale):
    y = x * lax.rsqrt(jnp.mean(x * x, axis=-1, keepdims=True) + NORM_EPS)
    return (y * g * (1.0 + scale) + shift).astype(BF16)


def _ada_kernel(c_ref, w_ref, b_ref, o_ref):
    c = c_ref[...]
    s = c * _sigmoid(c)
    o_ref[...] = jnp.dot(s, w_ref[...], precision=lax.Precision.HIGHEST,
                         preferred_element_type=F32) + b_ref[...]


def _ada_all(cond, ada_w, ada_b):
    depth, d, n = ada_w.shape
    tn = 1536
    return pl.pallas_call(
        _ada_kernel,
        grid=(depth, n // tn),
        in_specs=[pl.BlockSpec((SUBLANES, d), lambda l, j: (0, 0)),
                  pl.BlockSpec((None, d, tn), lambda l, j: (l, 0, j)),
                  pl.BlockSpec((None, 1, tn), lambda l, j: (l, 0, j))],
        out_specs=pl.BlockSpec((None, SUBLANES, tn), lambda l, j: (l, 0, j)),
        out_shape=jax.ShapeDtypeStruct((depth, SUBLANES, n), F32),
        compiler_params=_params("parallel", "parallel"),
        name="ada_mod",
    )(cond, ada_w, ada_b.reshape(depth, 1, n))


def _att_inproj_kernel(x_ref, g_ref, sh_ref, sc_ref, w_ref, ca_ref, sa_ref, cb_ref, sb_ref,
                       aq_ref, ak_ref, av_ref, bq_ref, bk_ref, bv_ref):
    h = _modulated(x_ref[...], g_ref[...], sh_ref[...], sc_ref[...])
    z = jnp.dot(h, w_ref[...], preferred_element_type=F32)
    ca = jnp.tile(ca_ref[...], (1, 4))
    sa = jnp.tile(sa_ref[...], (1, 4))
    cb = cb_ref[...]
    sb = sb_ref[...]
    cb4 = jnp.tile(cb, (1, 4))
    sb4 = jnp.tile(sb, (1, 4))
    o = ATT_IN
    aq = z[:, 0:512] * ca + z[:, o:o + 512] * sa
    ak = z[:, 512:1024] * ca + z[:, o + 512:o + 1024] * sa
    bq = z[:, 1536:2048] * cb4 + z[:, o + 1024:o + 1536] * sb4
    bk = z[:, 2048:2176] * cb + z[:, o + 1536:o + 1664] * sb
    aq_ref[...] = (aq * (DA_DIM ** -0.5 * LOG2_E)).astype(BF16)
    ak_ref[...] = ak.astype(BF16)
    av_ref[...] = z[:, 1024:1536].astype(BF16)
    bq_ref[...] = (bq * (WA_DIM ** -0.5)).astype(BF16)
    bk_ref[...] = bk.astype(BF16)
    bv_ref[...] = z[:, 2176:2304].astype(BF16)


def _att_inproj(x, g, sh, sc, w_ext, ca, sa, cb, sb):
    bsz, t, d = x.shape
    tm = min(ROW_TILE, t)
    n = w_ext.shape[1]
    row = lambda w: pl.BlockSpec((None, tm, w), lambda b, i: (b, i, 0))
    vec = pl.BlockSpec((None, 1, d), lambda b, i: (b, 0, 0))
    tab = pl.BlockSpec((tm, LANES), lambda b, i: (i, 0))
    outs = [512, 512, 512, 512, 128, 128]
    return pl.pallas_call(
        _att_inproj_kernel,
        grid=(bsz, t // tm),
        in_specs=[row(d), pl.BlockSpec((1, d), lambda b, i: (0, 0)), vec, vec, _resident((d, n)),
                  tab, tab, tab, tab],
        out_specs=[row(w) for w in outs],
        out_shape=[jax.ShapeDtypeStruct((bsz, t, w), BF16) for w in outs],
        compiler_params=_params("parallel", "parallel"),
        name="att_inproj",
    )(x, g, sh, sc, w_ext, ca, sa, cb, sb)


def _da_kernel(lam_ref, qt_ref, k_ref, vt_ref, g_ref, o_ref, acc_ref, s_ref, *, tk, nkt, out_scale):
    tq = qt_ref.shape[-1]
    qf = qt_ref[...].astype(F32)
    row = lax.broadcasted_iota(jnp.int32, qf.shape, 0)
    w = [jnp.where(jnp.right_shift(row, 5) == s, qf, 0.0).astype(BF16) for s in range(4)]
    acc_ref[...] = jnp.zeros_like(acc_ref)

    def scores(kt, s):
        off = pl.multiple_of(kt * tk, tk)
        return jnp.dot(k_ref[pl.ds(off, tk), :], w[s], preferred_element_type=F32)

    def absorb(st, s, vt, m):
        m_new = jnp.maximum(m, jnp.max(st, axis=0, keepdims=True))
        p = jnp.exp2(st - m_new).astype(BF16)
        h = s // 2
        pv = jnp.dot(vt[DA_VROWS * h:DA_VROWS * (h + 1), :], p, preferred_element_type=F32)
        acc_ref[s] = jnp.exp2(m - m_new) * acc_ref[s] + pv
        return m_new

    s_ref[0] = scores(0, 0)
    s_ref[1] = scores(0, 1)

    def body(kt, ms):
        vt = vt_ref[kt]
        nxt = jnp.minimum(kt + 1, nkt - 1)
        st2 = scores(kt, 2)
        m0 = absorb(s_ref[0], 0, vt, ms[0])
        st3 = scores(kt, 3)
        m1 = absorb(s_ref[1], 1, vt, ms[1])
        s_ref[0] = scores(nxt, 0)
        m2 = absorb(st2, 2, vt, ms[2])
        s_ref[1] = scores(nxt, 1)
        m3 = absorb(st3, 3, vt, ms[3])
        return m0, m1, m2, m3

    lax.fori_loop(0, nkt, body, tuple(jnp.full((1, tq), NEG_INF, F32) for _ in range(4)),
                  unroll=DA_UNROLL if nkt % DA_UNROLL == 0 else 1)
    lam = lam_ref[0]
    for h in range(2):
        a0 = acc_ref[2 * h]
        a1 = acc_ref[2 * h + 1]
        o = a0[0:64, :] / a0[64:65, :] - lam * (a1[0:64, :] / a1[64:65, :])
        y = o * lax.rsqrt(jnp.mean(o * o, axis=0, keepdims=True) + NORM_EPS)
        y = y * g_ref[64 * h:64 * h + 64, :] * out_scale
        o_ref[64 * h:64 * h + 64, :] = y.astype(o_ref.dtype)


def _diff_attention(q, k_all, v_all, lam, subln_g, out_scale):
    bsz, tq_all, _ = q.shape
    tk_all = k_all.shape[1]
    tq = min(DA_TQ, tq_all)
    tk = DA_TK
    nkt = tk_all // tk
    qt = q.reshape(bsz, tq_all, 4, LANES).transpose(0, 2, 3, 1)
    vt = v_all.reshape(bsz, nkt, tk, 4, 2, 64).transpose(0, 3, 1, 4, 5, 2)
    extra = jnp.zeros((bsz, 4, nkt, 2, DA_VROWS - 64, tk), BF16).at[:, :, :, :, 0, :].set(1.0)
    vt = jnp.concatenate([vt, extra], axis=4).reshape(bsz, 4, nkt, 2 * DA_VROWS, tk)
    gcol = jnp.tile(subln_g.astype(F32), 2).reshape(LANES, 1)
    ot = pl.pallas_call(
        functools.partial(_da_kernel, tk=tk, nkt=nkt, out_scale=out_scale),
        grid=(bsz, 4, tq_all // tq),
        in_specs=[pl.BlockSpec(memory_space=pltpu.SMEM),
                  pl.BlockSpec((None, None, LANES, tq), lambda b, hp, i: (b, hp, 0, i)),
                  pl.BlockSpec((None, tk_all, LANES), lambda b, hp, i: (b, 0, hp)),
                  pl.BlockSpec((None, None, nkt, 2 * DA_VROWS, tk), lambda b, hp, i: (b, hp, 0, 0, 0)),
                  pl.BlockSpec((LANES, 1), lambda b, hp, i: (0, 0))],
        out_specs=pl.BlockSpec((None, None, LANES, tq), lambda b, hp, i: (b, hp, 0, i)),
        out_shape=jax.ShapeDtypeStruct((bsz, 4, LANES, tq_all), BF16),
        scratch_shapes=[pltpu.VMEM((4, DA_VROWS, tq), F32), pltpu.VMEM((2, tk, tq), F32)],
        compiler_params=_params("parallel", "parallel", "arbitrary"),
        name="diff_attention",
    )(lam.reshape(1).astype(F32), qt, k_all, vt, gcol)
    return ot.transpose(0, 3, 1, 2).reshape(bsz, tq_all, 512)


def _wa_kernel(*refs, has_window):
    if has_window:
        (qt_ref, kc_ref, vct_ref, sink_ref, kp_ref, km_ref, kn_ref, vp_ref, vm_ref, vn_ref, o_ref) = refs
    else:
        (qt_ref, kc_ref, vct_ref, sink_ref, o_ref) = refs
    i = pl.program_id(1)
    nblk = pl.num_programs(1)
    zeros = jnp.zeros((WA_DIM, 4 * LANES), BF16)
    if has_window:
        r = lax.broadcasted_iota(jnp.int32, (LANES, 4 * LANES), 0)
        c = jnp.bitwise_and(lax.broadcasted_iota(jnp.int32, (LANES, 4 * LANES), 1), LANES - 1)
        off_p = jnp.where(i > 0, 0, 2 * LANES)
        off_n = jnp.where(i < nblk - 1, 0, 2 * LANES)
        mask_p = r >= c + off_p
        mask_n = r + off_n <= c
    for h in range(WA_KV_HEADS):
        q = qt_ref[h]
        w = jnp.concatenate([q, zeros] if h == 0 else [zeros, q], axis=0)
        hs = slice(WA_DIM * h, WA_DIM * h + WA_DIM)
        sink = sink_ref[h]
        parts = [jnp.dot(kc_ref[...], w, preferred_element_type=F32)]
        vts = [vct_ref[hs, :]]
        if has_window:
            sp = jnp.dot(kp_ref[...], w, preferred_element_type=F32)
            sm = jnp.dot(km_ref[...], w, preferred_element_type=F32)
            sn = jnp.dot(kn_ref[...], w, preferred_element_type=F32)
            parts += [jnp.where(mask_p, sp, NEG_INF), sm, jnp.where(mask_n, sn, NEG_INF)]
            vts += [vp_ref[hs, :], vm_ref[hs, :], vn_ref[hs, :]]
        m = sink
        for s in parts:
            m = jnp.maximum(m, jnp.max(s, axis=0, keepdims=True))
        den = jnp.exp(sink - m)
        o = jnp.zeros((WA_DIM, 4 * LANES), F32)
        for s, vt in zip(parts, vts):
            p = jnp.exp(s - m)
            den = den + jnp.sum(p, axis=0, keepdims=True)
            o = o + jnp.dot(vt, p.astype(BF16), preferred_element_type=F32)
        o_ref[h] = (o / den).astype(o_ref.dtype)


def _window_attention(q, kc, vc, sink, kl=None, vl=None):
    bsz, t, _ = q.shape
    nblk = t // LANES
    n_ctx = kc.shape[1]
    has_window = kl is not None
    qt = q.reshape(bsz, nblk, LANES, WA_KV_HEADS, WA_GROUP, WA_DIM).transpose(0, 3, 1, 5, 4, 2)
    qt = qt.reshape(bsz, WA_KV_HEADS, nblk, WA_DIM, WA_GROUP * LANES)
    vct = vc.transpose(0, 2, 1)
    sink_rows = jnp.repeat(sink.astype(F32).reshape(WA_KV_HEADS, 1, WA_GROUP), LANES, axis=-1)
    qspec = pl.BlockSpec((None, WA_KV_HEADS, None, WA_DIM, WA_GROUP * LANES),
                         lambda b, i: (b, 0, i, 0, 0))
    in_specs = [qspec,
                pl.BlockSpec((None, n_ctx, LANES), lambda b, i: (b, 0, 0)),
                pl.BlockSpec((None, LANES, n_ctx), lambda b, i: (b, 0, 0)),
                pl.BlockSpec((WA_KV_HEADS, 1, WA_GROUP * LANES), lambda b, i: (0, 0, 0))]
    args = [qt, kc, vct, sink_rows]
    if has_window:
        prev = lambda b, i: (b, jnp.maximum(i - 1, 0), 0)
        here = lambda b, i: (b, i, 0)
        nxt = lambda b, i: (b, jnp.minimum(i + 1, nblk - 1), 0)
        in_specs += [pl.BlockSpec((None, LANES, LANES), f) for f in (prev, here, nxt)]
        tprev = lambda b, i: (b, 0, jnp.maximum(i - 1, 0))
        there = lambda b, i: (b, 0, i)
        tnxt = lambda b, i: (b, 0, jnp.minimum(i + 1, nblk - 1))
        in_specs += [pl.BlockSpec((None, LANES, LANES), f) for f in (tprev, there, tnxt)]
        vlt = vl.transpose(0, 2, 1)
        args += [kl, kl, kl, vlt, vlt, vlt]
    ot = pl.pallas_call(
        functools.partial(_wa_kernel, has_window=has_window),
        grid=(bsz, nblk),
        in_specs=in_specs,
        out_specs=qspec,
        out_shape=jax.ShapeDtypeStruct(qt.shape, BF16),
        compiler_params=_params("parallel", "parallel"),
        name="window_attention" if has_window else "context_gqa",
    )(*args)
    o = ot.reshape(bsz, WA_KV_HEADS, nblk, WA_DIM, WA_GROUP, LANES).transpose(0, 2, 5, 1, 4, 3)
    return o.reshape(bsz, t, 512)


def _att_out_kernel(x_ref, oa_ref, ob_ref, w_ref, gate_ref, o_ref):
    o = jnp.dot(oa_ref[...], w_ref[0:512, :], preferred_element_type=F32)
    o = o + jnp.dot(ob_ref[...], w_ref[512:1024, :], preferred_element_type=F32)
    o_ref[...] = x_ref[...] + gate_ref[...] * o


def _att_out(x, oa, ob, w_out, gate):
    bsz, t, d = x.shape
    tm = min(ROW_TILE, t)
    row = lambda w: pl.BlockSpec((None, tm, w), lambda b, i: (b, i, 0))
    return pl.pallas_call(
        _att_out_kernel,
        grid=(bsz, t // tm),
        in_specs=[row(d), row(512), row(512), _resident(w_out.shape),
                  pl.BlockSpec((None, 1, d), lambda b, i: (b, 0, 0))],
        out_specs=row(d),
        out_shape=jax.ShapeDtypeStruct(x.shape, F32),
        compiler_params=_params("parallel", "parallel"),
        name="att_out",
    )(x, oa, ob, w_out, gate)


def _ffn_kernel(xm_ref, xp_ref, xn_ref, g_ref, sh_ref, sc_ref, gate_ref, wg_ref, wu_ref, cw_ref,
                wd_ref, o_ref, gs_ref):
    i = pl.program_id(1)
    first = i == 0
    last = i == pl.num_programs(1) - 1
    tm = xm_ref.shape[0]
    xm = xm_ref[...]
    g, sh, sc = g_ref[...], sh_ref[...], sc_ref[...]
    hm = _modulated(xm, g, sh, sc)
    hh = _modulated(jnp.concatenate([xp_ref[...], xn_ref[...]], axis=0), g, sh, sc)
    acc = jnp.zeros((tm, xm.shape[1]), F32)
    for c in range(D_FF // FFN_CHUNK):
        cs = slice(c * FFN_CHUNK, (c + 1) * FFN_CHUNK)
        gm = jnp.dot(hm, wg_ref[:, cs], preferred_element_type=F32)
        gh = jnp.dot(hh, wg_ref[:, cs], preferred_element_type=F32)
        gs_ref[0:SUBLANES, :] = jnp.where(first, 0.0, gh[0:SUBLANES, :])
        gs_ref[SUBLANES:SUBLANES + tm, :] = gm
        gs_ref[SUBLANES + tm:2 * SUBLANES + tm, :] = jnp.where(last, 0.0, gh[SUBLANES:, :])
        cw = cw_ref[:, cs]
        gc = (cw[0:1, :] * gs_ref[SUBLANES - 1:SUBLANES - 1 + tm, :] + cw[1:2, :] * gm
              + cw[2:3, :] * gs_ref[SUBLANES + 1:SUBLANES + 1 + tm, :])
        u = jnp.dot(hm, wu_ref[:, cs], preferred_element_type=F32)
        act = (gc * _sigmoid(gc) * u).astype(BF16)
        acc = acc + jnp.dot(act, wd_ref[cs, :], preferred_element_type=F32)
    o_ref[...] = xm + gate_ref[...] * acc


def _ffn(x, g, sh, sc, gate, w_g, w_u, conv_w, w_d):
    bsz, t, d = x.shape
    tm = min(ROW_TILE, t)
    nb = tm // SUBLANES
    last_blk = t // SUBLANES - 1
    vec = pl.BlockSpec((None, 1, d), lambda b, i: (b, 0, 0))
    return pl.pallas_call(
        _ffn_kernel,
        grid=(bsz, t // tm),
        in_specs=[pl.BlockSpec((None, tm, d), lambda b, i: (b, i, 0)),
                  pl.BlockSpec((None, SUBLANES, d), lambda b, i: (b, jnp.maximum(i * nb - 1, 0), 0)),
                  pl.BlockSpec((None, SUBLANES, d),
                               lambda b, i: (b, jnp.minimum((i + 1) * nb, last_blk), 0)),
                  pl.BlockSpec((1, d), lambda b, i: (0, 0)), vec, vec, vec,
                  _resident(w_g.shape), _resident(w_u.shape), _resident(conv_w.shape),
                  _resident(w_d.shape)],
        out_specs=pl.BlockSpec((None, tm, d), lambda b, i: (b, i, 0)),
        out_shape=jax.ShapeDtypeStruct(x.shape, F32),
        scratch_shapes=[pltpu.VMEM((tm + 2 * SUBLANES, FFN_CHUNK), F32)],
        compiler_params=_params("parallel", "parallel"),
        name="conv_ffn",
    )(x, x, x, g, sh, sc, gate, w_g, w_u, conv_w, w_d)


def _rec_inproj_kernel(x_ref, g_ref, sh_ref, sc_ref, w_ref, gl_ref, rl_ref, ul_ref):
    h = _modulated(x_ref[...], g_ref[...], sh_ref[...], sc_ref[...])
    z = jnp.dot(h, w_ref[...], preferred_element_type=F32)
    gl_ref[...] = z[:, 0:LRU_WIDTH]
    rl_ref[...] = z[:, LRU_WIDTH:2 * LRU_WIDTH]
    ul_ref[...] = z[:, 2 * LRU_WIDTH:]


def _rec_inproj(x, g, sh, sc, w_in):
    bsz, t, d = x.shape
    tm = min(ROW_TILE, t)
    row = lambda w: pl.BlockSpec((None, tm, w), lambda b, i: (b, i, 0))
    vec = pl.BlockSpec((None, 1, d), lambda b, i: (b, 0, 0))
    outs = [LRU_WIDTH, LRU_WIDTH, S5_WIDTH]
    return pl.pallas_call(
        _rec_inproj_kernel,
        grid=(bsz, t // tm),
        in_specs=[row(d), pl.BlockSpec((1, d), lambda b, i: (0, 0)), vec, vec, _resident(w_in.shape)],
        out_specs=[row(w) for w in outs],
        out_shape=[jax.ShapeDtypeStruct((bsz, t, w), F32) for w in outs],
        compiler_params=_params("parallel", "parallel"),
        name="rec_inproj",
    )(x, g, sh, sc, w_in)


def _lru_direction(xm_ref, xp_ref, xn_ref, at_start, at_end, cw_ref, cb_ref, w_ref, bias_ref, nsp,
                   out_ref, carry_ref, xs_ref, a_ref, b_ref, reverse):
    tm = xm_ref.shape[0]
    xs_ref[0:SUBLANES, :] = jnp.where(at_start, 0.0, xp_ref[...])
    xs_ref[SUBLANES:SUBLANES + tm, :] = xm_ref[...]
    xs_ref[SUBLANES + tm:2 * SUBLANES + tm, :] = jnp.where(at_end, 0.0, xn_ref[...])
    cw = cw_ref[...]
    rc = cb_ref[...] + cw[2:3, :] * xm_ref[...]
    for k in (0, 1, 3):
        rc = rc + cw[k:k + 1, :] * xs_ref[SUBLANES - 2 + k:SUBLANES - 2 + k + tm, :]
    gz = jnp.dot(rc.astype(BF16), w_ref[...], preferred_element_type=F32) + bias_ref[...]
    r = _sigmoid(gz[:, 0:LRU_WIDTH])
    ig = _sigmoid(gz[:, LRU_WIDTH:])
    a = jnp.exp(nsp * r)
    b = jnp.sqrt(1.0 - a * a) * (ig * rc)
    row8 = jnp.bitwise_and(lax.broadcasted_iota(jnp.int32, (tm, LRU_WIDTH), 0), SUBLANES - 1)
    for s in (1, 2, 4):
        if reverse:
            keep = row8 <= SUBLANES - 1 - s
            shift = tm - s
        else:
            keep = row8 >= s
            shift = s
        a_sh = jnp.where(keep, pltpu.roll(a, shift, 0), 1.0)
        b_sh = jnp.where(keep, pltpu.roll(b, shift, 0), 0.0)
        b = b + a * b_sh
        a = a * a_sh
    a_ref[...] = a
    b_ref[...] = b
    ngroups = tm // SUBLANES
    edge = 0 if reverse else SUBLANES - 1

    def body(j, carry):
        grp = (ngroups - 1 - j) if reverse else j
        off = pl.multiple_of(grp * SUBLANES, SUBLANES)
        h = b_ref[pl.ds(off, SUBLANES), :] + a_ref[pl.ds(off, SUBLANES), :] * carry
        out_ref[pl.ds(off, SUBLANES), :] = h
        return h[edge:edge + 1, :]

    carry_ref[...] = lax.fori_loop(0, ngroups, body, carry_ref[...])


def _lru_kernel(fm_ref, fp_ref, fn_ref, bm_ref, bp_ref, bn_ref, cw_ref, cb_ref, wf_ref, wb_ref,
                biasf_ref, biasb_ref, nsp_ref, h0_ref, hf_ref, hb_ref, hlast_ref,
                carry_f, carry_b, xs_ref, a_ref, b_ref):
    i = pl.program_id(1)
    first = i == 0
    last = i == pl.num_programs(1) - 1

    @pl.when(first)
    def _():
        carry_f[...] = h0_ref[0]
        carry_b[...] = h0_ref[1]

    _lru_direction(fm_ref, fp_ref, fn_ref, first, last, cw_ref, cb_ref, wf_ref, biasf_ref, nsp_ref[0],
                   hf_ref, carry_f, xs_ref, a_ref, b_ref, reverse=False)
    _lru_direction(bm_ref, bp_ref, bn_ref, last, first, cw_ref, cb_ref, wb_ref, biasb_ref, nsp_ref[1],
                   hb_ref, carry_b, xs_ref, a_ref, b_ref, reverse=True)
    hlast_ref[0] = carry_f[...]
    hlast_ref[1] = carry_b[...]


def _lru(rl, conv_w, conv_b, w_f, w_b, bias_f, bias_b, nsp, h0):
    bsz, t, c = rl.shape
    tm = min(ROW_TILE, t)
    nt = t // tm
    nb = tm // SUBLANES
    last_blk = t // SUBLANES - 1
    fwd = lambda b, i: (b, i, 0)
    bwd = lambda b, i: (b, nt - 1 - i, 0)
    fprev = lambda b, i: (b, jnp.maximum(i * nb - 1, 0), 0)
    fnext = lambda b, i: (b, jnp.minimum((i + 1) * nb, last_blk), 0)
    bprev = lambda b, i: (b, jnp.maximum((nt - 1 - i) * nb - 1, 0), 0)
    bnext = lambda b, i: (b, jnp.minimum((nt - i) * nb, last_blk), 0)
    main = lambda f: pl.BlockSpec((None, tm, c), f)
    halo = lambda f: pl.BlockSpec((None, SUBLANES, c), f)
    state = pl.BlockSpec((None, 2, 1, c), lambda b, i: (b, 0, 0, 0))
    return pl.pallas_call(
        _lru_kernel,
        grid=(bsz, nt),
        in_specs=[main(fwd), halo(fprev), halo(fnext), main(bwd), halo(bprev), halo(bnext),
                  _resident(conv_w.shape), _resident(conv_b.shape), _resident(w_f.shape),
                  _resident(w_b.shape), _resident(bias_f.shape), _resident(bias_b.shape),
                  _resident(nsp.shape), state],
        out_specs=[main(fwd), main(bwd), state],
        out_shape=[jax.ShapeDtypeStruct(rl.shape, F32), jax.ShapeDtypeStruct(rl.shape, F32),
                   jax.ShapeDtypeStruct((bsz, 2, 1, c), F32)],
        scratch_shapes=[pltpu.VMEM((1, c), F32), pltpu.VMEM((1, c), F32),
                        pltpu.VMEM((tm + 2 * SUBLANES, c), F32),
                        pltpu.VMEM((tm, c), F32), pltpu.VMEM((tm, c), F32)],
        compiler_params=_params("parallel", "arbitrary"),
        name="rglru",
    )(rl, rl, rl, rl, rl, rl, conv_w, conv_b, w_f, w_b, bias_f, bias_b, nsp, h0)


def _s5_in_kernel(u_ref, t_ref, bs_ref, yi_ref, xre_ref, xim_ref):
    x = jnp.zeros((u_ref.shape[1], 4 * LANES), F32)
    for q in range(2):
        ub = u_ref[q].astype(BF16)
        yi_ref[q] = jnp.dot(ub, t_ref[q], preferred_element_type=F32)
        x = x + jnp.dot(ub, bs_ref[q], preferred_element_type=F32)
    for d in range(2):
        xre_ref[d] = x[:, 2 * d * LANES:(2 * d + 1) * LANES]
        xim_ref[d] = x[:, (2 * d + 1) * LANES:(2 * d + 2) * LANES]


def _s5_scan_kernel(xre_ref, xim_ref, h0re_ref, h0im_ref, shr_ref, shi_ref, cwr_ref, cwi_ref,
                    sre_ref, sim_ref, lre_ref, lim_ref, pre_ref, pim_ref, *, reverse):
    nc, n = xre_ref.shape
    xr = xre_ref[...]
    xi = xim_ref[...]
    row8 = jnp.bitwise_and(lax.broadcasted_iota(jnp.int32, (nc, n), 0), SUBLANES - 1)
    for k, s in enumerate((1, 2, 4)):
        keep = (row8 <= SUBLANES - 1 - s) if reverse else (row8 >= s)
        shift = nc - s if reverse else s
        xr_sh = jnp.where(keep, pltpu.roll(xr, shift, 0), 0.0)
        xi_sh = jnp.where(keep, pltpu.roll(xi, shift, 0), 0.0)
        mr = shr_ref[k]
        mi = shi_ref[k]
        xr, xi = xr + (mr * xr_sh - mi * xi_sh), xi + (mr * xi_sh + mi * xr_sh)
    pre_ref[...] = xr
    pim_ref[...] = xi
    cwr = cwr_ref[...]
    cwi = cwi_ref[...]
    row = lax.broadcasted_iota(jnp.int32, (SUBLANES, n), 0)
    ngroups = nc // SUBLANES
    edge = 0 if reverse else SUBLANES - 1
    entry = SUBLANES - 1 - edge
    step = SUBLANES - 1 if reverse else 1

    def body(j, carry):
        cr, ci = carry
        grp = (ngroups - 1 - j) if reverse else j
        off = pl.multiple_of(grp * SUBLANES, SUBLANES)
        sr = pre_ref[pl.ds(off, SUBLANES), :] + (cwr * cr - cwi * ci)
        si = pim_ref[pl.ds(off, SUBLANES), :] + (cwr * ci + cwi * cr)
        pre_ref[pl.ds(off, SUBLANES), :] = jnp.where(row == entry, cr, pltpu.roll(sr, step, 0))
        pim_ref[pl.ds(off, SUBLANES), :] = jnp.where(row == entry, ci, pltpu.roll(si, step, 0))
        return sr[edge:edge + 1, :], si[edge:edge + 1, :]

    cr, ci = lax.fori_loop(0, ngroups, body, (h0re_ref[...], h0im_ref[...]))
    lre_ref[...] = cr
    lim_ref[...] = ci
    sre_ref[...] = pre_ref[...].astype(BF16)
    sim_ref[...] = pim_ref[...].astype(BF16)


def _s5_out_kernel(yi_ref, u_ref, sre_ref, sim_ref, cs_ref, d_ref, y_ref):
    s = jnp.concatenate([sre_ref[0], sim_ref[0], sre_ref[1], sim_ref[1]], axis=-1)
    for q in range(2):
        y = yi_ref[q] + d_ref[q] * u_ref[q]
        y_ref[q] = y + jnp.dot(s, cs_ref[q], preferred_element_type=F32)


def _s5(u, prm, h0re, h0im):
    bsz, t, _ = u.shape
    nc = t // S5_CHUNK
    ug = u.reshape(bsz, nc, S5_CHUNK, S5_GROUPS, S5_GROUP).transpose(0, 3, 1, 2, 4)
    ug = ug.reshape(bsz, S5_GROUPS, nc, 256)
    npair = S5_GROUPS // 2
    pair = pl.BlockSpec((None, 2, nc, 256), lambda b, p: (b, p, 0, 0))
    wpair = lambda r, w: pl.BlockSpec((2, r, w), lambda b, p: (p, 0, 0))
    lanes = pl.BlockSpec((None, 2, nc, LANES), lambda b, p: (b, 0, 0, p))
    state_shape = (bsz, 2, nc, S5_LANES)
    yi, xre, xim = pl.pallas_call(
        _s5_in_kernel,
        grid=(bsz, npair),
        in_specs=[pair, wpair(256, 256), wpair(256, 4 * LANES)],
        out_specs=[pair, lanes, lanes],
        out_shape=[jax.ShapeDtypeStruct(ug.shape, F32), jax.ShapeDtypeStruct(state_shape, F32),
                   jax.ShapeDtypeStruct(state_shape, F32)],
        compiler_params=_params("parallel", "parallel"),
        name="s5_in",
    )(ug, prm["toep"], prm["bs"])

    states = []
    for d in range(2):
        full = pl.BlockSpec((None, None, nc, S5_LANES), lambda b, d=d: (b, d, 0, 0))
        one = pl.BlockSpec((None, None, 1, S5_LANES), lambda b, d=d: (b, d, 0, 0))
        tab = pl.BlockSpec((None, 3, 1, S5_LANES), lambda b, d=d: (d, 0, 0, 0))
        tab8 = pl.BlockSpec((None, SUBLANES, S5_LANES), lambda b, d=d: (d, 0, 0))
        plain = pl.BlockSpec((None, nc, S5_LANES), lambda b: (b, 0, 0))
        plain1 = pl.BlockSpec((None, 1, S5_LANES), lambda b: (b, 0, 0))
        states.append(pl.pallas_call(
            functools.partial(_s5_scan_kernel, reverse=bool(d)),
            grid=(bsz,),
            in_specs=[full, full, one, one, tab, tab, tab8, tab8],
            out_specs=[plain, plain, plain1, plain1],
            out_shape=[jax.ShapeDtypeStruct((bsz, nc, S5_LANES), BF16),
                       jax.ShapeDtypeStruct((bsz, nc, S5_LANES), BF16),
                       jax.ShapeDtypeStruct((bsz, 1, S5_LANES), F32),
                       jax.ShapeDtypeStruct((bsz, 1, S5_LANES), F32)],
            scratch_shapes=[pltpu.VMEM((nc, S5_LANES), F32), pltpu.VMEM((nc, S5_LANES), F32)],
            compiler_params=_params("parallel"),
            name="s5_scan_bwd" if d else "s5_scan_fwd",
        )(xre, xim, h0re, h0im, prm["sh_r"], prm["sh_i"], prm["cw_r"], prm["cw_i"]))
    sre, sim, lre, lim = [jnp.stack([states[0][k], states[1][k]], axis=1) for k in range(4)]

    y = pl.pallas_call(
        _s5_out_kernel,
        grid=(bsz, npair),
        in_specs=[pair, pair, lanes, lanes, wpair(4 * LANES, 256), wpair(1, 256)],
        out_specs=pair,
        out_shape=jax.ShapeDtypeStruct(ug.shape, F32),
        compiler_params=_params("parallel", "parallel"),
        name="s5_out",
    )(yi, ug, sre, sim, prm["cs"], prm["dskip"])
    y = y.reshape(bsz, S5_GROUPS, nc, S5_CHUNK, S5_GROUP).transpose(0, 2, 3, 1, 4)
    return y.reshape(bsz, t, S5_WIDTH), lre, lim


def _s5_prepare(a_re, a_im, log_step, b_re, b_im, c_re, c_im, d_skip):
    a_re, a_im = a_re.astype(F32), a_im.astype(F32)
    step = jnp.exp(log_step.astype(F32))[..., None]
    mag = jnp.exp(a_re * step)
    abr, abi = mag * jnp.cos(a_im * step), mag * jnp.sin(a_im * step)
    den = a_re * a_re + a_im * a_im
    qr = ((abr - 1.0) * a_re + abi * a_im) / den
    qi = (abi * a_re - (abr - 1.0) * a_im) / den
    br_, bi_ = b_re.astype(F32), b_im.astype(F32)
    bbr = qr[..., None] * br_ - qi[..., None] * bi_
    bbi = qr[..., None] * bi_ + qi[..., None] * br_
    cr, ci = c_re.astype(F32), c_im.astype(F32)

    def power(n):
        n = jnp.asarray(n, F32)[..., None, None, None]
        m = jnp.exp(n * a_re * step)
        return m * jnp.cos(n * a_im * step), m * jnp.sin(n * a_im * step)

    lc = S5_CHUNK
    pr, pi = power(jnp.arange(lc + 1))
    mr = pr[..., None] * bbr - pi[..., None] * bbi
    mi = pr[..., None] * bbi + pi[..., None] * bbr
    hp = lax.Precision.HIGHEST
    kern = (jnp.einsum('dgop,jdgpi->jdgoi', cr, mr[:lc], precision=hp)
            - jnp.einsum('dgop,jdgpi->jdgoi', ci, mi[:lc], precision=hp))
    s_idx = jnp.arange(lc)[:, None]
    r_idx = jnp.arange(lc)[None, :]
    lag = jnp.clip(r_idx - s_idx, 0, lc - 1)
    toep = jnp.where((r_idx >= s_idx)[:, :, None, None, None, None], kern[lag], 0.0)
    toep = toep.transpose(2, 3, 0, 5, 1, 4).reshape(2, S5_GROUPS, lc * S5_GROUP, lc * S5_GROUP)
    rev = jnp.arange(lc - 1, -1, -1)
    bs_re = mr[rev].transpose(1, 2, 0, 4, 3).reshape(2, S5_GROUPS, lc * S5_GROUP, S5_STATE)
    bs_im = mi[rev].transpose(1, 2, 0, 4, 3).reshape(2, S5_GROUPS, lc * S5_GROUP, S5_STATE)
    pr1, pi1 = pr[1:], pi[1:]
    cs_re = cr[None] * pr1[:, :, :, None, :] - ci[None] * pi1[:, :, :, None, :]
    cs_im = -(cr[None] * pi1[:, :, :, None, :] + ci[None] * pr1[:, :, :, None, :])
    cs_re = cs_re.transpose(1, 2, 4, 0, 3).reshape(2, S5_GROUPS, S5_STATE, lc * S5_GROUP)
    cs_im = cs_im.transpose(1, 2, 4, 0, 3).reshape(2, S5_GROUPS, S5_STATE, lc * S5_GROUP)
    half = (jnp.arange(S5_GROUPS) % 2)[None, :, None, None]

    def pad_cols(m):
        z = jnp.zeros_like(m)
        return jnp.where(half == 0, jnp.concatenate([m, z], -1), jnp.concatenate([z, m], -1))

    def pad_rows(m):
        z = jnp.zeros_like(m)
        return jnp.where(half == 0, jnp.concatenate([m, z], -2), jnp.concatenate([z, m], -2))

    def mirror_rows(m):
        shp = m.shape
        return m.reshape(shp[0], lc, shp[1] // lc, shp[2])[:, ::-1].reshape(shp)

    def mirror_cols(m):
        shp = m.shape
        return m.reshape(shp[0], shp[1], lc, shp[2] // lc)[:, :, ::-1].reshape(shp)

    flat = lambda x: x.reshape(x.shape[:-2] + (S5_LANES,))
    shr, shi = power(jnp.asarray([lc, 2 * lc, 4 * lc]))
    cwr, cwi = power(lc * jnp.arange(1, SUBLANES + 1))
    cwr, cwi = flat(cwr).transpose(1, 0, 2), flat(cwi).transpose(1, 0, 2)
    bs_re, bs_im = pad_cols(bs_re), pad_cols(bs_im)
    cs_re, cs_im = pad_rows(cs_re), pad_rows(cs_im)
    return {
        "toep": (toep[0] + mirror_cols(mirror_rows(toep[1]))).astype(BF16),
        "bs": jnp.concatenate([bs_re[0], bs_im[0], mirror_rows(bs_re[1]), mirror_rows(bs_im[1])],
                              axis=-1).astype(BF16),
        "cs": jnp.concatenate([cs_re[0], cs_im[0], mirror_cols(cs_re[1]), mirror_cols(cs_im[1])],
                              axis=-2).astype(BF16),
        "sh_r": flat(shr).transpose(1, 0, 2)[:, :, None, :], "sh_i": flat(shi).transpose(1, 0, 2)[:, :, None, :],
        "cw_r": jnp.stack([cwr[0], cwr[1, ::-1]]), "cw_i": jnp.stack([cwi[0], cwi[1, ::-1]]),
        "dskip": jnp.tile(d_skip.astype(F32).reshape(S5_GROUPS, 1, S5_GROUP), (1, 1, lc)),
    }


def _rec_out_kernel(x_ref, gl_ref, hf_ref, hb_ref, y_ref, wglu_ref, w_ref, gate_ref, o_ref):
    t1 = ((hf_ref[...] + hb_ref[...]) * _gelu(gl_ref[...])).astype(BF16)
    yg = _gelu(y_ref[...])
    t2 = (yg * _sigmoid(jnp.dot(yg.astype(BF16), wglu_ref[...], preferred_element_type=F32))).astype(BF16)
    o = jnp.dot(t1, w_ref[0:LRU_WIDTH, :], preferred_element_type=F32)
    o = o + jnp.dot(t2, w_ref[LRU_WIDTH:, :], preferred_element_type=F32)
    o_ref[...] = x_ref[...] + gate_ref[...] * o


def _rec_out(x, gl, hf, hb, y, w_glu, w_out, gate):
    bsz, t, d = x.shape
    tm = min(ROW_TILE, t)
    row = lambda w: pl.BlockSpec((None, tm, w), lambda b, i: (b, i, 0))
    return pl.pallas_call(
        _rec_out_kernel,
        grid=(bsz, t // tm),
        in_specs=[row(d), row(LRU_WIDTH), row(LRU_WIDTH), row(LRU_WIDTH), row(S5_WIDTH),
                  _resident(w_glu.shape), _resident(w_out.shape),
                  pl.BlockSpec((None, 1, d), lambda b, i: (b, 0, 0))],
        out_specs=row(d),
        out_shape=jax.ShapeDtypeStruct(x.shape, F32),
        compiler_params=_params("parallel", "parallel"),
        name="rec_out",
    )(x, gl, hf, hb, y, w_glu, w_out, gate)


def _final_norm_kernel(x_ref, g_ref, o_ref):
    x = x_ref[...]
    o_ref[...] = x * lax.rsqrt(jnp.mean(x * x, axis=-1, keepdims=True) + NORM_EPS) * g_ref[...]


def _final_norm(x, g):
    bsz, t, d = x.shape
    tm = min(ROW_TILE, t)
    return pl.pallas_call(
        _final_norm_kernel,
        grid=(bsz, t // tm),
        in_specs=[pl.BlockSpec((None, tm, d), lambda b, i: (b, i, 0)),
                  pl.BlockSpec((1, d), lambda b, i: (0, 0))],
        out_specs=pl.BlockSpec((None, tm, d), lambda b, i: (b, i, 0)),
        out_shape=jax.ShapeDtypeStruct(x.shape, F32),
        compiler_params=_params("parallel", "parallel"),
        name="final_norm",
    )(x, g.reshape(1, d).astype(F32))


def _rope_tables(n, dim):
    rows = n // GRID_W
    row = jnp.repeat(jnp.arange(rows), GRID_W).astype(F32)
    col = jnp.tile(jnp.arange(GRID_W), rows).astype(F32)
    half = dim // 2
    freqs = ROPE_THETA ** (-jnp.arange(0, half, 2, dtype=F32) / half)

    def angles(pos):
        a = pos[:, None] * freqs[None, :]
        return jnp.concatenate([a, a], axis=-1)

    ang = jnp.concatenate([angles(row), angles(col)], axis=-1)
    reps = LANES // dim
    return jnp.tile(jnp.cos(ang), (1, reps)), jnp.tile(jnp.sin(ang), (1, reps))


def _rotated_columns(w, dim):
    quarter = dim // 4
    idx = jnp.arange(w.shape[1])
    low = (idx % (2 * quarter)) < quarter
    src = jnp.where(low, idx + quarter, idx - quarter)
    return jnp.where(low[None, :], -1.0, 1.0) * w[:, src]


def _att_weights(w_in):
    aq, ak = w_in[:, 0:512], w_in[:, 512:1024]
    bq, bk = w_in[:, 1536:2048], w_in[:, 2048:2176]
    ext = [w_in, _rotated_columns(aq, DA_DIM), _rotated_columns(ak, DA_DIM),
           _rotated_columns(bq, WA_DIM), _rotated_columns(bk, WA_DIM)]
    return jnp.concatenate(ext, axis=1).astype(BF16)


def _block_diag(w):
    eye = jnp.eye(LRU_BLOCKS, dtype=w.dtype)
    return jnp.einsum('nij,nm->nimj', w, eye).reshape(LRU_WIDTH, LRU_WIDTH)


def _vec3(v, bsz):
    return v.reshape(-1, 1, v.shape[-1]) if v.shape[0] == bsz else jnp.broadcast_to(v[None], (bsz, 1, v.shape[-1]))


def kernel(x, c, ctx, c_ctx, ada_w, ada_b, norm1_g, norm2_g, att_w_in, att_w_out, da_lam_q1, da_lam_k1, da_lam_q2, da_lam_k2, da_subln_g, wa_sink, rec_w_in, rec_w_out, lru_conv_w, lru_conv_b, lru_w_a, lru_b_a, lru_w_x, lru_b_x, lru_lam, s5_a_re, s5_a_im, s5_log_step, s5_b_re, s5_b_im, s5_c_re, s5_c_im, s5_d, s5_w_glu, ffn_w_g, ffn_w_u, ffn_conv_w, ffn_w_down, final_g):
    bsz, n_lat, d = x.shape
    n_ctx = ctx.shape[1]
    assert d == D_MODEL and bsz + 1 <= SUBLANES
    assert n_lat % ROW_TILE == 0 and n_ctx % DA_TK == 0 and n_ctx <= ROW_TILE
    x = x.astype(F32)
    ctx = ctx.astype(F32)

    cond = jnp.zeros((SUBLANES, d), F32).at[:bsz].set(c.astype(F32)).at[bsz].set(c_ctx.astype(F32))
    mods = _ada_all(cond, ada_w.astype(F32), ada_b.astype(F32))

    ca, sa = _rope_tables(n_lat, DA_DIM)
    cb, sb = _rope_tables(n_lat, WA_DIM)
    ones = jnp.ones((n_ctx, LANES), F32)
    zeros = jnp.zeros((n_ctx, LANES), F32)

    for l in range(DEPTH):
        ctx_out = l < DEPTH - 1
        m = mods[l]
        sh1, sc1, g1, sh2, sc2, g2 = [_vec3(v, bsz) for v in jnp.split(m[:bsz], 6, axis=-1)]
        csh1, csc1, cg1, csh2, csc2, cg2 = [_vec3(v, bsz) for v in jnp.split(m[bsz:bsz + 1], 6, axis=-1)]
        n1 = norm1_g[l].reshape(1, d).astype(F32)
        n2 = norm2_g[l].reshape(1, d).astype(F32)
        j = l // 2
        if l % 2 == 0:
            lam_init = 0.8 - 0.6 * math.exp(-0.3 * l)
            w_ext = _att_weights(att_w_in[j].astype(F32))
            w_out = att_w_out[j].astype(BF16)
            lam = (jnp.exp(jnp.sum(da_lam_q1[j].astype(F32) * da_lam_k1[j].astype(F32)))
                   - jnp.exp(jnp.sum(da_lam_q2[j].astype(F32) * da_lam_k2[j].astype(F32))) + lam_init)
            aql, akl, avl, bql, bkl, bvl = _att_inproj(x, n1, sh1, sc1, w_ext, ca, sa, cb, sb)
            aqc, akc, avc, bqc, bkc, bvc = _att_inproj(ctx, n1, csh1, csc1, w_ext, ones, zeros, ones, zeros)
            k_all = jnp.concatenate([akc, akl], axis=1)
            v_all = jnp.concatenate([avc, avl], axis=1)
            oal = _diff_attention(aql, k_all, v_all, lam, da_subln_g[j], 1.0 - lam_init)
            obl = _window_attention(bql, bkc, bvc, wa_sink[j], bkl, bvl)
            x = _att_out(x, oal, obl, w_out, g1)
            if ctx_out:
                oac = _diff_attention(aqc, akc, avc, lam, da_subln_g[j], 1.0 - lam_init)
                obc = _window_attention(bqc, bkc, bvc, wa_sink[j])
                ctx = _att_out(ctx, oac, obc, w_out, cg1)
        else:
            w_in = rec_w_in[j].astype(BF16)
            w_f = jnp.concatenate([_block_diag(lru_w_a[j, 0]), _block_diag(lru_w_x[j, 0])], axis=1).astype(BF16)
            w_b = jnp.concatenate([_block_diag(lru_w_a[j, 1]), _block_diag(lru_w_x[j, 1])], axis=1).astype(BF16)
            bias_f = jnp.concatenate([lru_b_a[j, 0], lru_b_x[j, 0]]).reshape(1, -1).astype(F32)
            bias_b = jnp.concatenate([lru_b_a[j, 1], lru_b_x[j, 1]]).reshape(1, -1).astype(F32)
            nsp = (-LRU_C * jax.nn.softplus(-lru_lam[j].astype(F32))).reshape(2, 1, LRU_WIDTH)
            conv_w = lru_conv_w[j].astype(F32)
            conv_b = lru_conv_b[j].reshape(1, -1).astype(F32)
            s5p = _s5_prepare(s5_a_re[j], s5_a_im[j], s5_log_step[j], s5_b_re[j], s5_b_im[j],
                              s5_c_re[j], s5_c_im[j], s5_d[j])
            gc_, rc_, uc_ = _rec_inproj(ctx, n1, csh1, csc1, w_in)
            gl_, rl_, ul_ = _rec_inproj(x, n1, sh1, sc1, w_in)
            h0 = jnp.zeros((bsz, 2, 1, LRU_WIDTH), F32)
            hfc, hbc, hlast = _lru(rc_, conv_w, conv_b, w_f, w_b, bias_f, bias_b, nsp, h0)
            hfl, hbl, _ = _lru(rl_, conv_w, conv_b, w_f, w_b, bias_f, bias_b, nsp, hlast)
            s0 = jnp.zeros((bsz, 2, 1, S5_LANES), F32)
            yc, lre, lim = _s5(uc_, s5p, s0, s0)
            yl, _, _ = _s5(ul_, s5p, lre, lim)
            w_glu = s5_w_glu[j].astype(BF16)
            w_out = rec_w_out[j].astype(BF16)
            x = _rec_out(x, gl_, hfl, hbl, yl, w_glu, w_out, g1)
            if ctx_out:
                ctx = _rec_out(ctx, gc_, hfc, hbc, yc, w_glu, w_out, cg1)
        wg, wu, wd = ffn_w_g[l].astype(BF16), ffn_w_u[l].astype(BF16), ffn_w_down[l].astype(BF16)
        cw = ffn_conv_w[l].astype(F32)
        x = _ffn(x, n2, sh2, sc2, g2, wg, wu, cw, wd)
        if ctx_out:
            ctx = _ffn(ctx, n2, csh2, csc2, cg2, wg, wu, cw, wd)
    return _final_norm(x, final_g)
```

```python
import functools
import math

import jax
import jax.numpy as jnp
from jax import lax
from jax.experimental import pallas as pl
from jax.experimental.pallas import tpu as pltpu

F32 = jnp.float32
BF16 = jnp.bfloat16

D_MODEL = 1024
DEPTH = 4
GRID_W = 64
ROPE_THETA = 10000.0
NORM_EPS = 1e-6
NEG_INF = -1e30
LOG2_E = math.log2(math.e)

DA_HEADS = 8
DA_DIM = 32
WA_HEADS = 8
WA_KV_HEADS = 2
WA_DIM = 64
WA_WINDOW = 128
WA_GROUP = WA_HEADS // WA_KV_HEADS
LRU_WIDTH = 768
LRU_BLOCKS = 8
LRU_BLOCK_DIM = LRU_WIDTH // LRU_BLOCKS
LRU_CONV = 4
LRU_C = 8.0
S5_WIDTH = 256
S5_GROUP = 16
S5_GROUPS = S5_WIDTH // S5_GROUP
S5_STATE = 64
S5_CHUNK = 16
S5_LANES = S5_GROUPS * S5_STATE
D_FF = 2816
FFN_CHUNK = 1408

DA_QK = DA_HEADS * 2 * DA_DIM
DA_V = DA_HEADS * 2 * DA_DIM
WA_Q = WA_HEADS * WA_DIM
WA_KV = WA_KV_HEADS * WA_DIM
ATT_IN = 2 * DA_QK + DA_V + WA_Q + 2 * WA_KV

SUBLANES = 8
LANES = 128
VMEM_LIMIT_BYTES = 56 * 1024 * 1024
ROW_TILE = 512
DA_TQ = 512
DA_TK = 256
DA_UNROLL = 13
DA_VROWS = 80


def _params(*sem):
    return pltpu.CompilerParams(dimension_semantics=sem, vmem_limit_bytes=VMEM_LIMIT_BYTES)


def _resident(shape):
    nd = len(shape)
    return pl.BlockSpec(shape, lambda *_: (0,) * nd, pipeline_mode=pl.Buffered(1))


def _sigmoid(x):
    return 1.0 / (1.0 + jnp.exp(-x))


def _gelu(x):
    return 0.5 * x * (1.0 + jnp.tanh(math.sqrt(2.0 / math.pi) * (x + 0.044715 * (x * x * x))))


def _modulated(x, g, shift, scale):
    y = x * lax.rsqrt(jnp.mean(x * x, axis=-1, keepdims=True) + NORM_EPS)
    return (y * g * (1.0 + scale) + shift).astype(BF16)


def _ada_kernel(c_ref, w_ref, b_ref, o_ref):
    c = c_ref[...]
    s = c * _sigmoid(c)
    o_ref[...] = jnp.dot(s, w_ref[...], precision=lax.Precision.HIGHEST,
                         preferred_element_type=F32) + b_ref[...]


def _ada_all(cond, ada_w, ada_b):
    depth, d, n = ada_w.shape
    tn = 1536
    return pl.pallas_call(
        _ada_kernel,
        grid=(depth, n // tn),
        in_specs=[pl.BlockSpec((SUBLANES, d), lambda l, j: (0, 0)),
                  pl.BlockSpec((None, d, tn), lambda l, j: (l, 0, j)),
                  pl.BlockSpec((None, 1, tn), lambda l, j: (l, 0, j))],
        out_specs=pl.BlockSpec((None, SUBLANES, tn), lambda l, j: (l, 0, j)),
        out_shape=jax.ShapeDtypeStruct((depth, SUBLANES, n), F32),
        compiler_params=_params("parallel", "parallel"),
        name="ada_mod",
    )(cond, ada_w, ada_b.reshape(depth, 1, n))


def _att_inproj_kernel(x_ref, g_ref, sh_ref, sc_ref, w_ref, ca_ref, sa_ref, cb_ref, sb_ref,
                       aq_ref, ak_ref, avt_ref, bq_ref, bk_ref, bv_ref):
    h = _modulated(x_ref[...], g_ref[...], sh_ref[...], sc_ref[...])
    z = jnp.dot(h, w_ref[...], preferred_element_type=F32)
    ca = jnp.tile(ca_ref[...], (1, 4))
    sa = jnp.tile(sa_ref[...], (1, 4))
    cb = cb_ref[...]
    sb = sb_ref[...]
    cb4 = jnp.tile(cb, (1, 4))
    sb4 = jnp.tile(sb, (1, 4))
    o = ATT_IN
    aq = z[:, 0:512] * ca + z[:, o:o + 512] * sa
    ak = z[:, 512:1024] * ca + z[:, o + 512:o + 1024] * sa
    bq = z[:, 1536:2048] * cb4 + z[:, o + 1024:o + 1536] * sb4
    bk = z[:, 2048:2176] * cb + z[:, o + 1536:o + 1664] * sb
    aq_ref[...] = (aq * (DA_DIM ** -0.5 * LOG2_E)).astype(BF16)
    ak_ref[...] = ak.astype(BF16)
    for hp in range(4):
        avt_ref[hp] = z[:, 1024 + LANES * hp:1024 + LANES * (hp + 1)].T.astype(BF16)
    bq_ref[...] = (bq * (WA_DIM ** -0.5)).astype(BF16)
    bk_ref[...] = bk.astype(BF16)
    bv_ref[...] = z[:, 2176:2304].astype(BF16)


def _att_inproj(x, g, sh, sc, w_ext, ca, sa, cb, sb):
    bsz, t, d = x.shape
    tm = min(ROW_TILE, t)
    n = w_ext.shape[1]
    row = lambda w: pl.BlockSpec((None, tm, w), lambda b, i: (b, i, 0))
    vec = pl.BlockSpec((None, 1, d), lambda b, i: (b, 0, 0))
    tab = pl.BlockSpec((tm, LANES), lambda b, i: (i, 0))
    rows = lambda w: (row(w), jax.ShapeDtypeStruct((bsz, t, w), BF16))
    outs = [rows(512), rows(512),
            (pl.BlockSpec((None, 4, LANES, tm), lambda b, i: (b, 0, 0, i)),
             jax.ShapeDtypeStruct((bsz, 4, LANES, t), BF16)),
            rows(512), rows(128), rows(128)]
    return pl.pallas_call(
        _att_inproj_kernel,
        grid=(bsz, t // tm),
        in_specs=[row(d), pl.BlockSpec((1, d), lambda b, i: (0, 0)), vec, vec, _resident((d, n)),
                  tab, tab, tab, tab],
        out_specs=[o[0] for o in outs],
        out_shape=[o[1] for o in outs],
        compiler_params=_params("parallel", "parallel"),
        name="att_inproj",
    )(x, g, sh, sc, w_ext, ca, sa, cb, sb)


def _da_kernel(lam_ref, q_ref, k_ref, vt_ref, g_ref, o_ref, acc_ref, s_ref, *, tk, nkt, out_scale):
    tq = q_ref.shape[0]
    qf = q_ref[...].astype(F32).T
    row = lax.broadcasted_iota(jnp.int32, qf.shape, 0)
    ones = jnp.where(lax.broadcasted_iota(jnp.int32, (DA_VROWS - 64, tk), 0) == 0, 1.0, 0.0).astype(BF16)
    w = [jnp.where(jnp.right_shift(row, 5) == s, qf, 0.0).astype(BF16) for s in range(4)]
    acc_ref[...] = jnp.zeros_like(acc_ref)

    def scores(kt, s):
        off = pl.multiple_of(kt * tk, tk)
        return jnp.dot(k_ref[pl.ds(off, tk), :], w[s], preferred_element_type=F32)

    def absorb(st, s, vt, m):
        m_new = jnp.maximum(m, jnp.max(st, axis=0, keepdims=True))
        p = jnp.exp2(st - m_new).astype(BF16)
        pv = jnp.dot(vt[s // 2], p, preferred_element_type=F32)
        acc_ref[s] = jnp.exp2(m - m_new) * acc_ref[s] + pv
        return m_new

    s_ref[0] = scores(0, 0)
    s_ref[1] = scores(0, 1)

    def body(kt, ms):
        v2 = vt_ref[:, pl.ds(pl.multiple_of(kt * tk, tk), tk)]
        vt = [jnp.concatenate([v2[64 * h:64 * h + 64, :], ones], axis=0) for h in range(2)]
        nxt = jnp.minimum(kt + 1, nkt - 1)
        st2 = scores(kt, 2)
        m0 = absorb(s_ref[0], 0, vt, ms[0])
        st3 = scores(kt, 3)
        m1 = absorb(s_ref[1], 1, vt, ms[1])
        s_ref[0] = scores(nxt, 0)
        m2 = absorb(st2, 2, vt, ms[2])
        s_ref[1] = scores(nxt, 1)
        m3 = absorb(st3, 3, vt, ms[3])
        return m0, m1, m2, m3

    lax.fori_loop(0, nkt, body, tuple(jnp.full((1, tq), NEG_INF, F32) for _ in range(4)),
                  unroll=DA_UNROLL if nkt % DA_UNROLL == 0 else 1)
    lam = lam_ref[0]
    ys = []
    for h in range(2):
        a0 = acc_ref[2 * h]
        a1 = acc_ref[2 * h + 1]
        o = a0[0:64, :] / a0[64:65, :] - lam * (a1[0:64, :] / a1[64:65, :])
        y = o * lax.rsqrt(jnp.mean(o * o, axis=0, keepdims=True) + NORM_EPS)
        ys.append(y * g_ref[64 * h:64 * h + 64, :] * out_scale)
    o_ref[...] = jnp.concatenate(ys, axis=0).T.astype(o_ref.dtype)


def _diff_attention(q, k_all, vt_all, lam, subln_g, out_scale):
    bsz, tq_all, _ = q.shape
    tk_all = k_all.shape[1]
    tq = min(DA_TQ, tq_all)
    tk = DA_TK
    nkt = tk_all // tk
    gcol = jnp.tile(subln_g.astype(F32), 2).reshape(LANES, 1)
    return pl.pallas_call(
        functools.partial(_da_kernel, tk=tk, nkt=nkt, out_scale=out_scale),
        grid=(bsz, 4, tq_all // tq),
        in_specs=[pl.BlockSpec(memory_space=pltpu.SMEM),
                  pl.BlockSpec((None, tq, LANES), lambda b, hp, i: (b, i, hp)),
                  pl.BlockSpec((None, tk_all, LANES), lambda b, hp, i: (b, 0, hp)),
                  pl.BlockSpec((None, None, LANES, tk_all), lambda b, hp, i: (b, hp, 0, 0)),
                  pl.BlockSpec((LANES, 1), lambda b, hp, i: (0, 0))],
        out_specs=pl.BlockSpec((None, tq, LANES), lambda b, hp, i: (b, i, hp)),
        out_shape=jax.ShapeDtypeStruct((bsz, tq_all, 512), BF16),
        scratch_shapes=[pltpu.VMEM((4, DA_VROWS, tq), F32), pltpu.VMEM((2, tk, tq), F32)],
        compiler_params=_params("parallel", "parallel", "arbitrary"),
        name="diff_attention",
    )(lam.reshape(1).astype(F32), q, k_all, vt_all, gcol)


def _wa_kernel(*refs, has_window):
    if has_window:
        (q_ref, kc_ref, vct_ref, sink_ref, kp_ref, km_ref, kn_ref, vp_ref, vm_ref, vn_ref, o_ref) = refs
    else:
        (q_ref, kc_ref, vct_ref, sink_ref, o_ref) = refs
    i = pl.program_id(1)
    nblk = pl.num_programs(1)
    zeros = jnp.zeros((WA_DIM, 4 * LANES), BF16)
    qt = q_ref[...].astype(F32).T
    outs = []
    if has_window:
        vwin = [v_ref[...].astype(F32).T.astype(BF16) for v_ref in (vp_ref, vm_ref, vn_ref)]
        r = lax.broadcasted_iota(jnp.int32, (LANES, 4 * LANES), 0)
        c = jnp.bitwise_and(lax.broadcasted_iota(jnp.int32, (LANES, 4 * LANES), 1), LANES - 1)
        off_p = jnp.where(i > 0, 0, 2 * LANES)
        off_n = jnp.where(i < nblk - 1, 0, 2 * LANES)
        mask_p = r >= c + off_p
        mask_n = r + off_n <= c
    scores = []
    for h in range(WA_KV_HEADS):
        heads = [WA_DIM * (WA_GROUP * h + g) for g in range(WA_GROUP)]
        q = jnp.concatenate([qt[r0:r0 + WA_DIM, :] for r0 in heads], axis=1).astype(BF16)
        w = jnp.concatenate([q, zeros] if h == 0 else [zeros, q], axis=0)
        parts = [jnp.dot(kc_ref[...], w, preferred_element_type=F32)]
        if has_window:
            parts += [jnp.dot(k_ref[...], w, preferred_element_type=F32) for k_ref in (kp_ref, km_ref, kn_ref)]
        scores.append(parts)
    for h in range(WA_KV_HEADS):
        hs = slice(WA_DIM * h, WA_DIM * h + WA_DIM)
        sink = sink_ref[h]
        parts = scores[h]
        vts = [vct_ref[hs, :]]
        if has_window:
            parts = [parts[0], jnp.where(mask_p, parts[1], NEG_INF), parts[2],
                     jnp.where(mask_n, parts[3], NEG_INF)]
            vts += [v[hs, :] for v in vwin]
        m = sink
        for s in parts:
            m = jnp.maximum(m, jnp.max(s, axis=0, keepdims=True))
        den = jnp.exp(sink - m)
        o = jnp.zeros((WA_DIM, 4 * LANES), F32)
        for s, vt in zip(parts, vts):
            p = jnp.exp(s - m)
            den = den + jnp.sum(p, axis=0, keepdims=True)
            o = o + jnp.dot(vt, p.astype(BF16), preferred_element_type=F32)
        o = o / den
        outs += [o[:, LANES * g:LANES * (g + 1)] for g in range(WA_GROUP)]
    o_ref[...] = jnp.concatenate(outs, axis=0).T.astype(o_ref.dtype)


def _window_attention(q, kc, vc, sink, kl=None, vl=None):
    bsz, t, _ = q.shape
    nblk = t // LANES
    n_ctx = kc.shape[1]
    has_window = kl is not None
    vct = vc.transpose(0, 2, 1)
    sink_rows = jnp.repeat(sink.astype(F32).reshape(WA_KV_HEADS, 1, WA_GROUP), LANES, axis=-1)
    qspec = pl.BlockSpec((None, LANES, 512), lambda b, i: (b, i, 0))
    in_specs = [qspec,
                pl.BlockSpec((None, n_ctx, LANES), lambda b, i: (b, 0, 0)),
                pl.BlockSpec((None, LANES, n_ctx), lambda b, i: (b, 0, 0)),
                pl.BlockSpec((WA_KV_HEADS, 1, WA_GROUP * LANES), lambda b, i: (0, 0, 0))]
    args = [q, kc, vct, sink_rows]
    if has_window:
        prev = lambda b, i: (b, jnp.maximum(i - 1, 0), 0)
        here = lambda b, i: (b, i, 0)
        nxt = lambda b, i: (b, jnp.minimum(i + 1, nblk - 1), 0)
        in_specs += 2 * [pl.BlockSpec((None, LANES, LANES), f) for f in (prev, here, nxt)]
        args += [kl, kl, kl, vl, vl, vl]
    return pl.pallas_call(
        functools.partial(_wa_kernel, has_window=has_window),
        grid=(bsz, nblk),
        in_specs=in_specs,
        out_specs=qspec,
        out_shape=jax.ShapeDtypeStruct(q.shape, BF16),
        compiler_params=_params("parallel", "parallel"),
        name="window_attention" if has_window else "context_gqa",
    )(*args)


def _att_out_kernel(x_ref, oa_ref, ob_ref, w_ref, gate_ref, o_ref):
    o = jnp.dot(oa_ref[...], w_ref[0:512, :], preferred_element_type=F32)
    o = o + jnp.dot(ob_ref[...], w_ref[512:1024, :], preferred_element_type=F32)
    o_ref[...] = x_ref[...] + gate_ref[...] * o


def _att_out(x, oa, ob, w_out, gate):
    bsz, t, d = x.shape
    tm = min(ROW_TILE, t)
    row = lambda w: pl.BlockSpec((None, tm, w), lambda b, i: (b, i, 0))
    return pl.pallas_call(
        _att_out_kernel,
        grid=(bsz, t // tm),
        in_specs=[row(d), row(512), row(512), _resident(w_out.shape),
                  pl.BlockSpec((None, 1, d), lambda b, i: (b, 0, 0))],
        out_specs=row(d),
        out_shape=jax.ShapeDtypeStruct(x.shape, F32),
        compiler_params=_params("parallel", "parallel"),
        name="att_out",
    )(x, oa, ob, w_out, gate)


def _ffn_kernel(xm_ref, xp_ref, xn_ref, g_ref, sh_ref, sc_ref, gate_ref, wg_ref, wu_ref, cw_ref,
                wd_ref, o_ref, gs_ref):
    i = pl.program_id(1)
    first = i == 0
    last = i == pl.num_programs(1) - 1
    tm = xm_ref.shape[0]
    xm = xm_ref[...]
    g, sh, sc = g_ref[...], sh_ref[...], sc_ref[...]
    hm = _modulated(xm, g, sh, sc)
    hh = _modulated(jnp.concatenate([xp_ref[...], xn_ref[...]], axis=0), g, sh, sc)
    acc = jnp.zeros((tm, xm.shape[1]), F32)
    n_chunks = D_FF // FFN_CHUNK

    def up(c):
        cs = slice(c * FFN_CHUNK, (c + 1) * FFN_CHUNK)
        return (jnp.dot(hm, wg_ref[:, cs], preferred_element_type=F32),
                jnp.dot(hh, wg_ref[:, cs], preferred_element_type=F32),
                jnp.dot(hm, wu_ref[:, cs], preferred_element_type=F32))

    nxt = up(0)
    for c in range(n_chunks):
        cs = slice(c * FFN_CHUNK, (c + 1) * FFN_CHUNK)
        gm, gh, u = nxt
        if c + 1 < n_chunks:
            nxt = up(c + 1)
        gs_ref[0:SUBLANES, :] = jnp.where(first, 0.0, gh[0:SUBLANES, :])
        gs_ref[SUBLANES:SUBLANES + tm, :] = gm
        gs_ref[SUBLANES + tm:2 * SUBLANES + tm, :] = jnp.where(last, 0.0, gh[SUBLANES:, :])
        cw = cw_ref[:, cs]
        gc = (cw[0:1, :] * gs_ref[SUBLANES - 1:SUBLANES - 1 + tm, :] + cw[1:2, :] * gm
              + cw[2:3, :] * gs_ref[SUBLANES + 1:SUBLANES + 1 + tm, :])
        act = (gc * _sigmoid(gc) * u).astype(BF16)
        acc = acc + jnp.dot(act, wd_ref[cs, :], preferred_element_type=F32)
    o_ref[...] = xm + gate_ref[...] * acc


def _ffn(x, g, sh, sc, gate, w_g, w_u, conv_w, w_d):
    bsz, t, d = x.shape
    tm = min(ROW_TILE, t)
    nb = tm // SUBLANES
    last_blk = t // SUBLANES - 1
    vec = pl.BlockSpec((None, 1, d), lambda b, i: (b, 0, 0))
    return pl.pallas_call(
        _ffn_kernel,
        grid=(bsz, t // tm),
        in_specs=[pl.BlockSpec((None, tm, d), lambda b, i: (b, i, 0)),
                  pl.BlockSpec((None, SUBLANES, d), lambda b, i: (b, jnp.maximum(i * nb - 1, 0), 0)),
                  pl.BlockSpec((None, SUBLANES, d),
                               lambda b, i: (b, jnp.minimum((i + 1) * nb, last_blk), 0)),
                  pl.BlockSpec((1, d), lambda b, i: (0, 0)), vec, vec, vec,
                  _resident(w_g.shape), _resident(w_u.shape), _resident(conv_w.shape),
                  _resident(w_d.shape)],
        out_specs=pl.BlockSpec((None, tm, d), lambda b, i: (b, i, 0)),
        out_shape=jax.ShapeDtypeStruct(x.shape, F32),
        scratch_shapes=[pltpu.VMEM((tm + 2 * SUBLANES, FFN_CHUNK), F32)],
        compiler_params=_params("parallel", "parallel"),
        name="conv_ffn",
    )(x, x, x, g, sh, sc, gate, w_g, w_u, conv_w, w_d)


def _rec_inproj_kernel(x_ref, g_ref, sh_ref, sc_ref, w_ref, gl_ref, rl_ref, ul_ref):
    h = _modulated(x_ref[...], g_ref[...], sh_ref[...], sc_ref[...])
    z = jnp.dot(h, w_ref[...], preferred_element_type=F32)
    gl_ref[...] = z[:, 0:LRU_WIDTH]
    rl_ref[...] = z[:, LRU_WIDTH:2 * LRU_WIDTH]
    ul_ref[...] = z[:, 2 * LRU_WIDTH:]


def _rec_inproj(x, g, sh, sc, w_in):
    bsz, t, d = x.shape
    tm = min(ROW_TILE, t)
    row = lambda w: pl.BlockSpec((None, tm, w), lambda b, i: (b, i, 0))
    vec = pl.BlockSpec((None, 1, d), lambda b, i: (b, 0, 0))
    outs = [LRU_WIDTH, LRU_WIDTH, S5_WIDTH]
    return pl.pallas_call(
        _rec_inproj_kernel,
        grid=(bsz, t // tm),
        in_specs=[row(d), pl.BlockSpec((1, d), lambda b, i: (0, 0)), vec, vec, _resident(w_in.shape)],
        out_specs=[row(w) for w in outs],
        out_shape=[jax.ShapeDtypeStruct((bsz, t, w), F32) for w in outs],
        compiler_params=_params("parallel", "parallel"),
        name="rec_inproj",
    )(x, g, sh, sc, w_in)


def _lru_direction(xm_ref, xp_ref, xn_ref, at_start, at_end, cw_ref, cb_ref, w_ref, bias_ref, nsp,
                   out_ref, carry_ref, xs_ref, a_ref, b_ref, reverse):
    tm = xm_ref.shape[0]
    xs_ref[0:SUBLANES, :] = jnp.where(at_start, 0.0, xp_ref[...])
    xs_ref[SUBLANES:SUBLANES + tm, :] = xm_ref[...]
    xs_ref[SUBLANES + tm:2 * SUBLANES + tm, :] = jnp.where(at_end, 0.0, xn_ref[...])
    cw = cw_ref[...]
    rc = cb_ref[...] + cw[2:3, :] * xm_ref[...]
    for k in (0, 1, 3):
        rc = rc + cw[k:k + 1, :] * xs_ref[SUBLANES - 2 + k:SUBLANES - 2 + k + tm, :]
    gz = jnp.dot(rc.astype(BF16), w_ref[...], preferred_element_type=F32) + bias_ref[...]
    r = _sigmoid(gz[:, 0:LRU_WIDTH])
    ig = _sigmoid(gz[:, LRU_WIDTH:])
    a = jnp.exp(nsp * r)
    b = jnp.sqrt(1.0 - a * a) * (ig * rc)
    row8 = jnp.bitwise_and(lax.broadcasted_iota(jnp.int32, (tm, LRU_WIDTH), 0), SUBLANES - 1)
    for s in (1, 2, 4):
        if reverse:
            keep = row8 <= SUBLANES - 1 - s
            shift = tm - s
        else:
            keep = row8 >= s
            shift = s
        a_sh = jnp.where(keep, pltpu.roll(a, shift, 0), 1.0)
        b_sh = jnp.where(keep, pltpu.roll(b, shift, 0), 0.0)
        b = b + a * b_sh
        a = a * a_sh
    a_ref[...] = a
    b_ref[...] = b
    ngroups = tm // SUBLANES
    edge = 0 if reverse else SUBLANES - 1

    def body(j, carry):
        grp = (ngroups - 1 - j) if reverse else j
        off = pl.multiple_of(grp * SUBLANES, SUBLANES)
        h = b_ref[pl.ds(off, SUBLANES), :] + a_ref[pl.ds(off, SUBLANES), :] * carry
        out_ref[pl.ds(off, SUBLANES), :] = h
        return h[edge:edge + 1, :]

    carry_ref[...] = lax.fori_loop(0, ngroups, body, carry_ref[...])


def _lru_kernel(fm_ref, fp_ref, fn_ref, bm_ref, bp_ref, bn_ref, cw_ref, cb_ref, wf_ref, wb_ref,
                biasf_ref, biasb_ref, nsp_ref, h0_ref, hf_ref, hb_ref, hlast_ref,
                carry_f, carry_b, xs_ref, a_ref, b_ref):
    i = pl.program_id(1)
    first = i == 0
    last = i == pl.num_programs(1) - 1

    @pl.when(first)
    def _():
        carry_f[...] = h0_ref[0]
        carry_b[...] = h0_ref[1]

    _lru_direction(fm_ref, fp_ref, fn_ref, first, last, cw_ref, cb_ref, wf_ref, biasf_ref, nsp_ref[0],
                   hf_ref, carry_f, xs_ref, a_ref, b_ref, reverse=False)
    _lru_direction(bm_ref, bp_ref, bn_ref, last, first, cw_ref, cb_ref, wb_ref, biasb_ref, nsp_ref[1],
                   hb_ref, carry_b, xs_ref, a_ref, b_ref, reverse=True)
    hlast_ref[0] = carry_f[...]
    hlast_ref[1] = carry_b[...]


def _lru(rl, conv_w, conv_b, w_f, w_b, bias_f, bias_b, nsp, h0):
    bsz, t, c = rl.shape
    tm = min(ROW_TILE, t)
    nt = t // tm
    nb = tm // SUBLANES
    last_blk = t // SUBLANES - 1
    fwd = lambda b, i: (b, i, 0)
    bwd = lambda b, i: (b, nt - 1 - i, 0)
    fprev = lambda b, i: (b, jnp.maximum(i * nb - 1, 0), 0)
    fnext = lambda b, i: (b, jnp.minimum((i + 1) * nb, last_blk), 0)
    bprev = lambda b, i: (b, jnp.maximum((nt - 1 - i) * nb - 1, 0), 0)
    bnext = lambda b, i: (b, jnp.minimum((nt - i) * nb, last_blk), 0)
    main = lambda f: pl.BlockSpec((None, tm, c), f)
    halo = lambda f: pl.BlockSpec((None, SUBLANES, c), f)
    state = pl.BlockSpec((None, 2, 1, c), lambda b, i: (b, 0, 0, 0))
    return pl.pallas_call(
        _lru_kernel,
        grid=(bsz, nt),
        in_specs=[main(fwd), halo(fprev), halo(fnext), main(bwd), halo(bprev), halo(bnext),
                  _resident(conv_w.shape), _resident(conv_b.shape), _resident(w_f.shape),
                  _resident(w_b.shape), _resident(bias_f.shape), _resident(bias_b.shape),
                  _resident(nsp.shape), state],
        out_specs=[main(fwd), main(bwd), state],
        out_shape=[jax.ShapeDtypeStruct(rl.shape, F32), jax.ShapeDtypeStruct(rl.shape, F32),
                   jax.ShapeDtypeStruct((bsz, 2, 1, c), F32)],
        scratch_shapes=[pltpu.VMEM((1, c), F32), pltpu.VMEM((1, c), F32),
                        pltpu.VMEM((tm + 2 * SUBLANES, c), F32),
                        pltpu.VMEM((tm, c), F32), pltpu.VMEM((tm, c), F32)],
        compiler_params=_params("parallel", "arbitrary"),
        name="rglru",
    )(rl, rl, rl, rl, rl, rl, conv_w, conv_b, w_f, w_b, bias_f, bias_b, nsp, h0)


def _s5_in_kernel(u_ref, t_ref, bs_ref, yi_ref, xre_ref, xim_ref):
    x = jnp.zeros((u_ref.shape[1], 4 * LANES), F32)
    for q in range(2):
        ub = u_ref[q].astype(BF16)
        yi_ref[q] = jnp.dot(ub, t_ref[q], preferred_element_type=F32)
        x = x + jnp.dot(ub, bs_ref[q], preferred_element_type=F32)
    for d in range(2):
        xre_ref[d] = x[:, 2 * d * LANES:(2 * d + 1) * LANES]
        xim_ref[d] = x[:, (2 * d + 1) * LANES:(2 * d + 2) * LANES]


def _s5_scan_kernel(xre_ref, xim_ref, h0re_ref, h0im_ref, shr_ref, shi_ref, cwr_ref, cwi_ref,
                    sre_ref, sim_ref, lre_ref, lim_ref, pre_ref, pim_ref, *, reverse):
    nc, n = xre_ref.shape
    xr = xre_ref[...]
    xi = xim_ref[...]
    row8 = jnp.bitwise_and(lax.broadcasted_iota(jnp.int32, (nc, n), 0), SUBLANES - 1)
    for k, s in enumerate((1, 2, 4)):
        keep = (row8 <= SUBLANES - 1 - s) if reverse else (row8 >= s)
        shift = nc - s if reverse else s
        xr_sh = jnp.where(keep, pltpu.roll(xr, shift, 0), 0.0)
        xi_sh = jnp.where(keep, pltpu.roll(xi, shift, 0), 0.0)
        mr = shr_ref[k]
        mi = shi_ref[k]
        xr, xi = xr + (mr * xr_sh - mi * xi_sh), xi + (mr * xi_sh + mi * xr_sh)
    pre_ref[...] = xr
    pim_ref[...] = xi
    cwr = cwr_ref[...]
    cwi = cwi_ref[...]
    row = lax.broadcasted_iota(jnp.int32, (SUBLANES, n), 0)
    ngroups = nc // SUBLANES
    edge = 0 if reverse else SUBLANES - 1
    entry = SUBLANES - 1 - edge
    step = SUBLANES - 1 if reverse else 1

    def body(j, carry):
        cr, ci = carry
        grp = (ngroups - 1 - j) if reverse else j
        off = pl.multiple_of(grp * SUBLANES, SUBLANES)
        sr = pre_ref[pl.ds(off, SUBLANES), :] + (cwr * cr - cwi * ci)
        si = pim_ref[pl.ds(off, SUBLANES), :] + (cwr * ci + cwi * cr)
        pre_ref[pl.ds(off, SUBLANES), :] = jnp.where(row == entry, cr, pltpu.roll(sr, step, 0))
        pim_ref[pl.ds(off, SUBLANES), :] = jnp.where(row == entry, ci, pltpu.roll(si, step, 0))
        return sr[edge:edge + 1, :], si[edge:edge + 1, :]

    cr, ci = lax.fori_loop(0, ngroups, body, (h0re_ref[...], h0im_ref[...]))
    lre_ref[...] = cr
    lim_ref[...] = ci
    sre_ref[...] = pre_ref[...].astype(BF16)
    sim_ref[...] = pim_ref[...].astype(BF16)


def _s5_out_kernel(yi_ref, u_ref, sre_ref, sim_ref, cs_ref, d_ref, y_ref):
    s = jnp.concatenate([sre_ref[0], sim_ref[0], sre_ref[1], sim_ref[1]], axis=-1)
    for q in range(2):
        y = yi_ref[q] + d_ref[q] * u_ref[q]
        y_ref[q] = y + jnp.dot(s, cs_ref[q], preferred_element_type=F32)


def _s5(u, prm, h0re, h0im):
    bsz, t, _ = u.shape
    nc = t // S5_CHUNK
    ug = u.reshape(bsz, nc, S5_CHUNK, S5_GROUPS, S5_GROUP).transpose(0, 3, 1, 2, 4)
    ug = ug.reshape(bsz, S5_GROUPS, nc, 256)
    npair = S5_GROUPS // 2
    pair = pl.BlockSpec((None, 2, nc, 256), lambda b, p: (b, p, 0, 0))
    wpair = lambda r, w: pl.BlockSpec((2, r, w), lambda b, p: (p, 0, 0))
    lanes = pl.BlockSpec((None, 2, nc, LANES), lambda b, p: (b, 0, 0, p))
    state_shape = (bsz, 2, nc, S5_LANES)
    yi, xre, xim = pl.pallas_call(
        _s5_in_kernel,
        grid=(bsz, npair),
        in_specs=[pair, wpair(256, 256), wpair(256, 4 * LANES)],
        out_specs=[pair, lanes, lanes],
        out_shape=[jax.ShapeDtypeStruct(ug.shape, F32), jax.ShapeDtypeStruct(state_shape, F32),
                   jax.ShapeDtypeStruct(state_shape, F32)],
        compiler_params=_params("parallel", "parallel"),
        name="s5_in",
    )(ug, prm["toep"], prm["bs"])

    states = []
    for d in range(2):
        full = pl.BlockSpec((None, None, nc, S5_LANES), lambda b, d=d: (b, d, 0, 0))
        one = pl.BlockSpec((None, None, 1, S5_LANES), lambda b, d=d: (b, d, 0, 0))
        tab = pl.BlockSpec((None, 3, 1, S5_LANES), lambda b, d=d: (d, 0, 0, 0))
        tab8 = pl.BlockSpec((None, SUBLANES, S5_LANES), lambda b, d=d: (d, 0, 0))
        plain = pl.BlockSpec((None, nc, S5_LANES), lambda b: (b, 0, 0))
        plain1 = pl.BlockSpec((None, 1, S5_LANES), lambda b: (b, 0, 0))
        states.append(pl.pallas_call(
            functools.partial(_s5_scan_kernel, reverse=bool(d)),
            grid=(bsz,),
            in_specs=[full, full, one, one, tab, tab, tab8, tab8],
            out_specs=[plain, plain, plain1, plain1],
            out_shape=[jax.ShapeDtypeStruct((bsz, nc, S5_LANES), BF16),
                       jax.ShapeDtypeStruct((bsz, nc, S5_LANES), BF16),
                       jax.ShapeDtypeStruct((bsz, 1, S5_LANES), F32),
                       jax.ShapeDtypeStruct((bsz, 1, S5_LANES), F32)],
            scratch_shapes=[pltpu.VMEM((nc, S5_LANES), F32), pltpu.VMEM((nc, S5_LANES), F32)],
            compiler_params=_params("parallel"),
            name="s5_scan_bwd" if d else "s5_scan_fwd",
        )(xre, xim, h0re, h0im, prm["sh_r"], prm["sh_i"], prm["cw_r"], prm["cw_i"]))
    sre, sim, lre, lim = [jnp.stack([states[0][k], states[1][k]], axis=1) for k in range(4)]

    y = pl.pallas_call(
        _s5_out_kernel,
        grid=(bsz, npair),
        in_specs=[pair, pair, lanes, lanes, wpair(4 * LANES, 256), wpair(1, 256)],
        out_specs=pair,
        out_shape=jax.ShapeDtypeStruct(ug.shape, F32),
        compiler_params=_params("parallel", "parallel"),
        name="s5_out",
    )(yi, ug, sre, sim, prm["cs"], prm["dskip"])
    y = y.reshape(bsz, S5_GROUPS, nc, S5_CHUNK, S5_GROUP).transpose(0, 2, 3, 1, 4)
    return y.reshape(bsz, t, S5_WIDTH), lre, lim


def _s5_prepare(a_re, a_im, log_step, b_re, b_im, c_re, c_im, d_skip):
    a_re, a_im = a_re.astype(F32), a_im.astype(F32)
    step = jnp.exp(log_step.astype(F32))[..., None]
    mag = jnp.exp(a_re * step)
    abr, abi = mag * jnp.cos(a_im * step), mag * jnp.sin(a_im * step)
    den = a_re * a_re + a_im * a_im
    qr = ((abr - 1.0) * a_re + abi * a_im) / den
    qi = (abi * a_re - (abr - 1.0) * a_im) / den
    br_, bi_ = b_re.astype(F32), b_im.astype(F32)
    bbr = qr[..., None] * br_ - qi[..., None] * bi_
    bbi = qr[..., None] * bi_ + qi[..., None] * br_
    cr, ci = c_re.astype(F32), c_im.astype(F32)

    def power(n):
        n = jnp.asarray(n, F32)[..., None, None, None]
        m = jnp.exp(n * a_re * step)
        return m * jnp.cos(n * a_im * step), m * jnp.sin(n * a_im * step)

    lc = S5_CHUNK
    pr, pi = power(jnp.arange(lc + 1))
    mr = pr[..., None] * bbr - pi[..., None] * bbi
    mi = pr[..., None] * bbi + pi[..., None] * bbr
    hp = lax.Precision.HIGHEST
    kern = (jnp.einsum('dgop,jdgpi->jdgoi', cr, mr[:lc], precision=hp)
            - jnp.einsum('dgop,jdgpi->jdgoi', ci, mi[:lc], precision=hp))
    s_idx = jnp.arange(lc)[:, None]
    r_idx = jnp.arange(lc)[None, :]
    lag = jnp.clip(r_idx - s_idx, 0, lc - 1)
    toep = jnp.where((r_idx >= s_idx)[:, :, None, None, None, None], kern[lag], 0.0)
    toep = toep.transpose(2, 3, 0, 5, 1, 4).reshape(2, S5_GROUPS, lc * S5_GROUP, lc * S5_GROUP)
    rev = jnp.arange(lc - 1, -1, -1)
    bs_re = mr[rev].transpose(1, 2, 0, 4, 3).reshape(2, S5_GROUPS, lc * S5_GROUP, S5_STATE)
    bs_im = mi[rev].transpose(1, 2, 0, 4, 3).reshape(2, S5_GROUPS, lc * S5_GROUP, S5_STATE)
    pr1, pi1 = pr[1:], pi[1:]
    cs_re = cr[None] * pr1[:, :, :, None, :] - ci[None] * pi1[:, :, :, None, :]
    cs_im = -(cr[None] * pi1[:, :, :, None, :] + ci[None] * pr1[:, :, :, None, :])
    cs_re = cs_re.transpose(1, 2, 4, 0, 3).reshape(2, S5_GROUPS, S5_STATE, lc * S5_GROUP)
    cs_im = cs_im.transpose(1, 2, 4, 0, 3).reshape(2, S5_GROUPS, S5_STATE, lc * S5_GROUP)
    half = (jnp.arange(S5_GROUPS) % 2)[None, :, None, None]

    def pad_cols(m):
        z = jnp.zeros_like(m)
        return jnp.where(half == 0, jnp.concatenate([m, z], -1), jnp.concatenate([z, m], -1))

    def pad_rows(m):
        z = jnp.zeros_like(m)
        return jnp.where(half == 0, jnp.concatenate([m, z], -2), jnp.concatenate([z, m], -2))

    def mirror_rows(m):
        shp = m.shape
        return m.reshape(shp[0], lc, shp[1] // lc, shp[2])[:, ::-1].reshape(shp)

    def mirror_cols(m):
        shp = m.shape
        return m.reshape(shp[0], shp[1], lc, shp[2] // lc)[:, :, ::-1].reshape(shp)

    flat = lambda x: x.reshape(x.shape[:-2] + (S5_LANES,))
    shr, shi = power(jnp.asarray([lc, 2 * lc, 4 * lc]))
    cwr, cwi = power(lc * jnp.arange(1, SUBLANES + 1))
    cwr, cwi = flat(cwr).transpose(1, 0, 2), flat(cwi).transpose(1, 0, 2)
    bs_re, bs_im = pad_cols(bs_re), pad_cols(bs_im)
    cs_re, cs_im = pad_rows(cs_re), pad_rows(cs_im)
    return {
        "toep": (toep[0] + mirror_cols(mirror_rows(toep[1]))).astype(BF16),
        "bs": jnp.concatenate([bs_re[0], bs_im[0], mirror_rows(bs_re[1]), mirror_rows(bs_im[1])],
                              axis=-1).astype(BF16),
        "cs": jnp.concatenate([cs_re[0], cs_im[0], mirror_cols(cs_re[1]), mirror_cols(cs_im[1])],
                              axis=-2).astype(BF16),
        "sh_r": flat(shr).transpose(1, 0, 2)[:, :, None, :], "sh_i": flat(shi).transpose(1, 0, 2)[:, :, None, :],
        "cw_r": jnp.stack([cwr[0], cwr[1, ::-1]]), "cw_i": jnp.stack([cwi[0], cwi[1, ::-1]]),
        "dskip": jnp.tile(d_skip.astype(F32).reshape(S5_GROUPS, 1, S5_GROUP), (1, 1, lc)),
    }


def _rec_out_kernel(x_ref, gl_ref, hf_ref, hb_ref, y_ref, wglu_ref, w_ref, gate_ref, o_ref):
    t1 = ((hf_ref[...] + hb_ref[...]) * _gelu(gl_ref[...])).astype(BF16)
    yg = _gelu(y_ref[...])
    t2 = (yg * _sigmoid(jnp.dot(yg.astype(BF16), wglu_ref[...], preferred_element_type=F32))).astype(BF16)
    o = jnp.dot(t1, w_ref[0:LRU_WIDTH, :], preferred_element_type=F32)
    o = o + jnp.dot(t2, w_ref[LRU_WIDTH:, :], preferred_element_type=F32)
    o_ref[...] = x_ref[...] + gate_ref[...] * o


def _rec_out(x, gl, hf, hb, y, w_glu, w_out, gate):
    bsz, t, d = x.shape
    tm = min(ROW_TILE, t)
    row = lambda w: pl.BlockSpec((None, tm, w), lambda b, i: (b, i, 0))
    return pl.pallas_call(
        _rec_out_kernel,
        grid=(bsz, t // tm),
        in_specs=[row(d), row(LRU_WIDTH), row(LRU_WIDTH), row(LRU_WIDTH), row(S5_WIDTH),
                  _resident(w_glu.shape), _resident(w_out.shape),
                  pl.BlockSpec((None, 1, d), lambda b, i: (b, 0, 0))],
        out_specs=row(d),
        out_shape=jax.ShapeDtypeStruct(x.shape, F32),
        compiler_params=_params("parallel", "parallel"),
        name="rec_out",
    )(x, gl, hf, hb, y, w_glu, w_out, gate)


def _final_norm_kernel(x_ref, g_ref, o_ref):
    x = x_ref[...]
    o_ref[...] = x * lax.rsqrt(jnp.mean(x * x, axis=-1, keepdims=True) + NORM_EPS) * g_ref[...]


def _final_norm(x, g):
    bsz, t, d = x.shape
    tm = min(ROW_TILE, t)
    return pl.pallas_call(
        _final_norm_kernel,
        grid=(bsz, t // tm),
        in_specs=[pl.BlockSpec((None, tm, d), lambda b, i: (b, i, 0)),
                  pl.BlockSpec((1, d), lambda b, i: (0, 0))],
        out_specs=pl.BlockSpec((None, tm, d), lambda b, i: (b, i, 0)),
        out_shape=jax.ShapeDtypeStruct(x.shape, F32),
        compiler_params=_params("parallel", "parallel"),
        name="final_norm",
    )(x, g.reshape(1, d).astype(F32))


def _rope_tables(n, dim):
    rows = n // GRID_W
    row = jnp.repeat(jnp.arange(rows), GRID_W).astype(F32)
    col = jnp.tile(jnp.arange(GRID_W), rows).astype(F32)
    half = dim // 2
    freqs = ROPE_THETA ** (-jnp.arange(0, half, 2, dtype=F32) / half)

    def angles(pos):
        a = pos[:, None] * freqs[None, :]
        return jnp.concatenate([a, a], axis=-1)

    ang = jnp.concatenate([angles(row), angles(col)], axis=-1)
    reps = LANES // dim
    return jnp.tile(jnp.cos(ang), (1, reps)), jnp.tile(jnp.sin(ang), (1, reps))


def _rotated_columns(w, dim):
    quarter = dim // 4
    idx = jnp.arange(w.shape[1])
    low = (idx % (2 * quarter)) < quarter
    src = jnp.where(low, idx + quarter, idx - quarter)
    return jnp.where(low[None, :], -1.0, 1.0) * w[:, src]


def _att_weights(w_in):
    aq, ak = w_in[:, 0:512], w_in[:, 512:1024]
    bq, bk = w_in[:, 1536:2048], w_in[:, 2048:2176]
    ext = [w_in, _rotated_columns(aq, DA_DIM), _rotated_columns(ak, DA_DIM),
           _rotated_columns(bq, WA_DIM), _rotated_columns(bk, WA_DIM)]
    return jnp.concatenate(ext, axis=1).astype(BF16)


def _block_diag(w):
    eye = jnp.eye(LRU_BLOCKS, dtype=w.dtype)
    return jnp.einsum('nij,nm->nimj', w, eye).reshape(LRU_WIDTH, LRU_WIDTH)


def _vec3(v, bsz):
    return v.reshape(-1, 1, v.shape[-1]) if v.shape[0] == bsz else jnp.broadcast_to(v[None], (bsz, 1, v.shape[-1]))


def kernel(x, c, ctx, c_ctx, ada_w, ada_b, norm1_g, norm2_g, att_w_in, att_w_out, da_lam_q1, da_lam_k1, da_lam_q2, da_lam_k2, da_subln_g, wa_sink, rec_w_in, rec_w_out, lru_conv_w, lru_conv_b, lru_w_a, lru_b_a, lru_w_x, lru_b_x, lru_lam, s5_a_re, s5_a_im, s5_log_step, s5_b_re, s5_b_im, s5_c_re, s5_c_im, s5_d, s5_w_glu, ffn_w_g, ffn_w_u, ffn_conv_w, ffn_w_down, final_g):
    bsz, n_lat, d = x.shape
    n_ctx = ctx.shape[1]
    assert d == D_MODEL and bsz + 1 <= SUBLANES
    assert n_lat % ROW_TILE == 0 and n_ctx % DA_TK == 0 and n_ctx <= ROW_TILE
    x = x.astype(F32)
    ctx = ctx.astype(F32)

    cond = jnp.zeros((SUBLANES, d), F32).at[:bsz].set(c.astype(F32)).at[bsz].set(c_ctx.astype(F32))
    mods = _ada_all(cond, ada_w.astype(F32), ada_b.astype(F32))

    ca, sa = _rope_tables(n_lat, DA_DIM)
    cb, sb = _rope_tables(n_lat, WA_DIM)
    ones = jnp.ones((n_ctx, LANES), F32)
    zeros = jnp.zeros((n_ctx, LANES), F32)

    for l in range(DEPTH):
        ctx_out = l < DEPTH - 1
        m = mods[l]
        sh1, sc1, g1, sh2, sc2, g2 = [_vec3(v, bsz) for v in jnp.split(m[:bsz], 6, axis=-1)]
        csh1, csc1, cg1, csh2, csc2, cg2 = [_vec3(v, bsz) for v in jnp.split(m[bsz:bsz + 1], 6, axis=-1)]
        n1 = norm1_g[l].reshape(1, d).astype(F32)
        n2 = norm2_g[l].reshape(1, d).astype(F32)
        j = l // 2
        if l % 2 == 0:
            lam_init = 0.8 - 0.6 * math.exp(-0.3 * l)
            w_ext = _att_weights(att_w_in[j].astype(F32))
            w_out = att_w_out[j].astype(BF16)
            lam = (jnp.exp(jnp.sum(da_lam_q1[j].astype(F32) * da_lam_k1[j].astype(F32)))
                   - jnp.exp(jnp.sum(da_lam_q2[j].astype(F32) * da_lam_k2[j].astype(F32))) + lam_init)
            aql, akl, avl, bql, bkl, bvl = _att_inproj(x, n1, sh1, sc1, w_ext, ca, sa, cb, sb)
            aqc, akc, avc, bqc, bkc, bvc = _att_inproj(ctx, n1, csh1, csc1, w_ext, ones, zeros, ones, zeros)
            k_all = jnp.concatenate([akc, akl], axis=1)
            v_all = jnp.concatenate([avc, avl], axis=-1)
            oal = _diff_attention(aql, k_all, v_all, lam, da_subln_g[j], 1.0 - lam_init)
            obl = _window_attention(bql, bkc, bvc, wa_sink[j], bkl, bvl)
            x = _att_out(x, oal, obl, w_out, g1)
            if ctx_out:
                oac = _diff_attention(aqc, akc, avc, lam, da_subln_g[j], 1.0 - lam_init)
                obc = _window_attention(bqc, bkc, bvc, wa_sink[j])
                ctx = _att_out(ctx, oac, obc, w_out, cg1)
        else:
            w_in = rec_w_in[j].astype(BF16)
            w_f = jnp.concatenate([_block_diag(lru_w_a[j, 0]), _block_diag(lru_w_x[j, 0])], axis=1).astype(BF16)
            w_b = jnp.concatenate([_block_diag(lru_w_a[j, 1]), _block_diag(lru_w_x[j, 1])], axis=1).astype(BF16)
            bias_f = jnp.concatenate([lru_b_a[j, 0], lru_b_x[j, 0]]).reshape(1, -1).astype(F32)
            bias_b = jnp.concatenate([lru_b_a[j, 1], lru_b_x[j, 1]]).reshape(1, -1).astype(F32)
            nsp = (-LRU_C * jax.nn.softplus(-lru_lam[j].astype(F32))).reshape(2, 1, LRU_WIDTH)
            conv_w = lru_conv_w[j].astype(F32)
            conv_b = lru_conv_b[j].reshape(1, -1).astype(F32)
            s5p = _s5_prepare(s5_a_re[j], s5_a_im[j], s5_log_step[j], s5_b_re[j], s5_b_im[j],
                              s5_c_re[j], s5_c_im[j], s5_d[j])
            gc_, rc_, uc_ = _rec_inproj(ctx, n1, csh1, csc1, w_in)
            gl_, rl_, ul_ = _rec_inproj(x, n1, sh1, sc1, w_in)
            h0 = jnp.zeros((bsz, 2, 1, LRU_WIDTH), F32)
            hfc, hbc, hlast = _lru(rc_, conv_w, conv_b, w_f, w_b, bias_f, bias_b, nsp, h0)
            hfl, hbl, _ = _lru(rl_, conv_w, conv_b, w_f, w_b, bias_f, bias_b, nsp, hlast)
            s0 = jnp.zeros((bsz, 2, 1, S5_LANES), F32)
            yc, lre, lim = _s5(uc_, s5p, s0, s0)
            yl, _, _ = _s5(ul_, s5p, lre, lim)
            w_glu = s5_w_glu[j].astype(BF16)
            w_out = rec_w_out[j].astype(BF16)
            x = _rec_out(x, gl_, hfl, hbl, yl, w_glu, w_out, g1)
            if ctx_out:
                ctx = _rec_out(ctx, gc_, hfc, hbc, yc, w_glu, w_out, cg1)
        wg, wu, wd = ffn_w_g[l].astype(BF16), ffn_w_u[l].astype(BF16), ffn_w_down[l].astype(BF16)
        cw = ffn_conv_w[l].astype(F32)
        x = _ffn(x, n2, sh2, sc2, g2, wg, wu, cw, wd)
        if ctx_out:
            ctx = _ffn(ctx, n2, csh2, csc2, cg2, wg, wu, cw, wd)
    return _final_norm(x, final_g)
```

```python
import functools
import math

import jax
import jax.numpy as jnp
from jax import lax
from jax.experimental import pallas as pl
from jax.experimental.pallas import tpu as pltpu

F32 = jnp.float32
BF16 = jnp.bfloat16

D_MODEL = 1024
DEPTH = 4
GRID_W = 64
ROPE_THETA = 10000.0
NORM_EPS = 1e-6
NEG_INF = -1e30
LOG2_E = math.log2(math.e)

DA_HEADS = 8
DA_DIM = 32
WA_HEADS = 8
WA_KV_HEADS = 2
WA_DIM = 64
WA_WINDOW = 128
WA_GROUP = WA_HEADS // WA_KV_HEADS
LRU_WIDTH = 768
LRU_BLOCKS = 8
LRU_BLOCK_DIM = LRU_WIDTH // LRU_BLOCKS
LRU_CONV = 4
LRU_C = 8.0
S5_WIDTH = 256
S5_GROUP = 16
S5_GROUPS = S5_WIDTH // S5_GROUP
S5_STATE = 64
S5_CHUNK = 16
S5_LANES = S5_GROUPS * S5_STATE
D_FF = 2816
FFN_CHUNK = 1408

DA_QK = DA_HEADS * 2 * DA_DIM
DA_V = DA_HEADS * 2 * DA_DIM
WA_Q = WA_HEADS * WA_DIM
WA_KV = WA_KV_HEADS * WA_DIM
ATT_IN = 2 * DA_QK + DA_V + WA_Q + 2 * WA_KV

SUBLANES = 8
LANES = 128
VMEM_LIMIT_BYTES = 56 * 1024 * 1024
ROW_TILE = 512
DA_TQ = 512
DA_TK = 256
DA_UNROLL = 13
DA_VROWS = 80


def _params(*sem):
    return pltpu.CompilerParams(dimension_semantics=sem, vmem_limit_bytes=VMEM_LIMIT_BYTES)


def _resident(shape):
    nd = len(shape)
    return pl.BlockSpec(shape, lambda *_: (0,) * nd, pipeline_mode=pl.Buffered(1))


def _sigmoid(x):
    return 1.0 / (1.0 + jnp.exp(-x))


def _gelu(x):
    return 0.5 * x * (1.0 + jnp.tanh(math.sqrt(2.0 / math.pi) * (x + 0.044715 * (x * x * x))))


def _modulated(x, g, shift, scale):
    y = x * lax.rsqrt(jnp.mean(x * x, axis=-1, keepdims=True) + NORM_EPS)
    return (y * g * (1.0 + scale) + shift).astype(BF16)


def _ada_kernel(c_ref, w_ref, b_ref, o_ref):
    c = c_ref[...]
    s = c * _sigmoid(c)
    o_ref[...] = jnp.dot(s, w_ref[...], precision=lax.Precision.HIGHEST,
                         preferred_element_type=F32) + b_ref[...]


def _ada_all(cond, ada_w, ada_b):
    depth, d, n = ada_w.shape
    tn = 1536
    return pl.pallas_call(
        _ada_kernel,
        grid=(depth, n // tn),
        in_specs=[pl.BlockSpec((SUBLANES, d), lambda l, j: (0, 0)),
                  pl.BlockSpec((None, d, tn), lambda l, j: (l, 0, j)),
                  pl.BlockSpec((None, 1, tn), lambda l, j: (l, 0, j))],
        out_specs=pl.BlockSpec((None, SUBLANES, tn), lambda l, j: (l, 0, j)),
        out_shape=jax.ShapeDtypeStruct((depth, SUBLANES, n), F32),
        compiler_params=_params("parallel", "parallel"),
        name="ada_mod",
    )(cond, ada_w, ada_b.reshape(depth, 1, n))


def _att_inproj_kernel(x_ref, g_ref, sh_ref, sc_ref, w_ref, ca_ref, sa_ref, cb_ref, sb_ref,
                       aq_ref, ak_ref, avt_ref, bq_ref, bk_ref, bv_ref):
    h = _modulated(x_ref[...], g_ref[...], sh_ref[...], sc_ref[...])
    z = jnp.dot(h, w_ref[...], preferred_element_type=F32)
    ca = jnp.tile(ca_ref[...], (1, 4))
    sa = jnp.tile(sa_ref[...], (1, 4))
    cb = cb_ref[...]
    sb = sb_ref[...]
    cb4 = jnp.tile(cb, (1, 4))
    sb4 = jnp.tile(sb, (1, 4))
    o = ATT_IN
    aq = z[:, 0:512] * ca + z[:, o:o + 512] * sa
    ak = z[:, 512:1024] * ca + z[:, o + 512:o + 1024] * sa
    bq = z[:, 1536:2048] * cb4 + z[:, o + 1024:o + 1536] * sb4
    bk = z[:, 2048:2176] * cb + z[:, o + 1536:o + 1664] * sb
    aq_ref[...] = (aq * (DA_DIM ** -0.5 * LOG2_E)).astype(BF16)
    ak_ref[...] = ak.astype(BF16)
    for hp in range(4):
        avt_ref[hp] = z[:, 1024 + LANES * hp:1024 + LANES * (hp + 1)].T.astype(BF16)
    bq_ref[...] = (bq * (WA_DIM ** -0.5)).astype(BF16)
    bk_ref[...] = bk.astype(BF16)
    bv_ref[...] = z[:, 2176:2304].astype(BF16)


def _att_inproj(x, g, sh, sc, w_ext, ca, sa, cb, sb):
    bsz, t, d = x.shape
    tm = min(ROW_TILE, t)
    n = w_ext.shape[1]
    row = lambda w: pl.BlockSpec((None, tm, w), lambda b, i: (b, i, 0))
    vec = pl.BlockSpec((None, 1, d), lambda b, i: (b, 0, 0))
    tab = pl.BlockSpec((tm, LANES), lambda b, i: (i, 0))
    rows = lambda w: (row(w), jax.ShapeDtypeStruct((bsz, t, w), BF16))
    outs = [rows(512), rows(512),
            (pl.BlockSpec((None, 4, LANES, tm), lambda b, i: (b, 0, 0, i)),
             jax.ShapeDtypeStruct((bsz, 4, LANES, t), BF16)),
            rows(512), rows(128), rows(128)]
    return pl.pallas_call(
        _att_inproj_kernel,
        grid=(bsz, t // tm),
        in_specs=[row(d), pl.BlockSpec((1, d), lambda b, i: (0, 0)), vec, vec, _resident((d, n)),
                  tab, tab, tab, tab],
        out_specs=[o[0] for o in outs],
        out_shape=[o[1] for o in outs],
        compiler_params=_params("parallel", "parallel"),
        name="att_inproj",
    )(x, g, sh, sc, w_ext, ca, sa, cb, sb)


def _da_kernel(lam_ref, q_ref, k_ref, vt_ref, g_ref, o_ref, acc_ref, s_ref, *, tk, nkt, out_scale):
    tq = q_ref.shape[0]
    qf = q_ref[...].astype(F32).T
    row = lax.broadcasted_iota(jnp.int32, qf.shape, 0)
    ones = jnp.where(lax.broadcasted_iota(jnp.int32, (DA_VROWS - 64, tk), 0) == 0, 1.0, 0.0).astype(BF16)
    w = [jnp.where(jnp.right_shift(row, 5) == s, qf, 0.0).astype(BF16) for s in range(4)]
    acc_ref[...] = jnp.zeros_like(acc_ref)

    def scores(kt, s):
        off = pl.multiple_of(kt * tk, tk)
        st = jnp.dot(k_ref[pl.ds(off, tk), :], w[s], preferred_element_type=F32)
        s_ref[s] = st
        return jnp.max(st, axis=0, keepdims=True)

    def absorb(s, vt, m, mt):
        m_new = jnp.maximum(m, mt)
        p = jnp.exp2(s_ref[s] - m_new).astype(BF16)
        pv = jnp.dot(vt[s // 2], p, preferred_element_type=F32)
        acc_ref[s] = jnp.exp2(m - m_new) * acc_ref[s] + pv
        return m_new

    def body(kt, carry):
        ms, mt0, mt1 = carry
        v2 = vt_ref[:, pl.ds(pl.multiple_of(kt * tk, tk), tk)]
        vt = [jnp.concatenate([v2[64 * h:64 * h + 64, :], ones], axis=0) for h in range(2)]
        nxt = jnp.minimum(kt + 1, nkt - 1)
        mt2 = scores(kt, 2)
        mt3 = scores(kt, 3)
        m0 = absorb(0, vt, ms[0], mt0)
        mt0 = scores(nxt, 0)
        m1 = absorb(1, vt, ms[1], mt1)
        mt1 = scores(nxt, 1)
        m2 = absorb(2, vt, ms[2], mt2)
        m3 = absorb(3, vt, ms[3], mt3)
        return (m0, m1, m2, m3), mt0, mt1

    m_init = tuple(jnp.full((1, tq), NEG_INF, F32) for _ in range(4))
    lax.fori_loop(0, nkt, body, (m_init, scores(0, 0), scores(0, 1)),
                  unroll=DA_UNROLL if nkt % DA_UNROLL == 0 else 1)
    lam = lam_ref[0]
    ys = []
    for h in range(2):
        a0 = acc_ref[2 * h]
        a1 = acc_ref[2 * h + 1]
        o = a0[0:64, :] / a0[64:65, :] - lam * (a1[0:64, :] / a1[64:65, :])
        y = o * lax.rsqrt(jnp.mean(o * o, axis=0, keepdims=True) + NORM_EPS)
        ys.append(y * g_ref[64 * h:64 * h + 64, :] * out_scale)
    o_ref[...] = jnp.concatenate(ys, axis=0).T.astype(o_ref.dtype)


def _diff_attention(q, k_all, vt_all, lam, subln_g, out_scale):
    bsz, tq_all, _ = q.shape
    tk_all = k_all.shape[1]
    tq = min(DA_TQ, tq_all)
    tk = DA_TK
    nkt = tk_all // tk
    gcol = jnp.tile(subln_g.astype(F32), 2).reshape(LANES, 1)
    return pl.pallas_call(
        functools.partial(_da_kernel, tk=tk, nkt=nkt, out_scale=out_scale),
        grid=(bsz, 4, tq_all // tq),
        in_specs=[pl.BlockSpec(memory_space=pltpu.SMEM),
                  pl.BlockSpec((None, tq, LANES), lambda b, hp, i: (b, i, hp)),
                  pl.BlockSpec((None, tk_all, LANES), lambda b, hp, i: (b, 0, hp)),
                  pl.BlockSpec((None, None, LANES, tk_all), lambda b, hp, i: (b, hp, 0, 0)),
                  pl.BlockSpec((LANES, 1), lambda b, hp, i: (0, 0))],
        out_specs=pl.BlockSpec((None, tq, LANES), lambda b, hp, i: (b, i, hp)),
        out_shape=jax.ShapeDtypeStruct((bsz, tq_all, 512), BF16),
        scratch_shapes=[pltpu.VMEM((4, DA_VROWS, tq), F32), pltpu.VMEM((4, tk, tq), F32)],
        compiler_params=_params("parallel", "parallel", "arbitrary"),
        name="diff_attention",
    )(lam.reshape(1).astype(F32), q, k_all, vt_all, gcol)


def _wa_kernel(*refs, has_window):
    if has_window:
        (q_ref, kc_ref, vct_ref, sink_ref, kp_ref, km_ref, kn_ref, vp_ref, vm_ref, vn_ref, o_ref) = refs
    else:
        (q_ref, kc_ref, vct_ref, sink_ref, o_ref) = refs
    i = pl.program_id(1)
    nblk = pl.num_programs(1)
    zeros = jnp.zeros((WA_DIM, 4 * LANES), BF16)
    qt = q_ref[...].astype(F32).T
    outs = []
    if has_window:
        vwin = [v_ref[...].astype(F32).T.astype(BF16) for v_ref in (vp_ref, vm_ref, vn_ref)]
        r = lax.broadcasted_iota(jnp.int32, (LANES, 4 * LANES), 0)
        c = jnp.bitwise_and(lax.broadcasted_iota(jnp.int32, (LANES, 4 * LANES), 1), LANES - 1)
        off_p = jnp.where(i > 0, 0, 2 * LANES)
        off_n = jnp.where(i < nblk - 1, 0, 2 * LANES)
        mask_p = r >= c + off_p
        mask_n = r + off_n <= c
    scores = []
    for h in range(WA_KV_HEADS):
        heads = [WA_DIM * (WA_GROUP * h + g) for g in range(WA_GROUP)]
        q = jnp.concatenate([qt[r0:r0 + WA_DIM, :] for r0 in heads], axis=1).astype(BF16)
        w = jnp.concatenate([q, zeros] if h == 0 else [zeros, q], axis=0)
        parts = [jnp.dot(kc_ref[...], w, preferred_element_type=F32)]
        if has_window:
            parts += [jnp.dot(k_ref[...], w, preferred_element_type=F32) for k_ref in (kp_ref, km_ref, kn_ref)]
        scores.append(parts)
    for h in range(WA_KV_HEADS):
        hs = slice(WA_DIM * h, WA_DIM * h + WA_DIM)
        sink = sink_ref[h]
        parts = scores[h]
        vts = [vct_ref[hs, :]]
        if has_window:
            parts = [parts[0], jnp.where(mask_p, parts[1], NEG_INF), parts[2],
                     jnp.where(mask_n, parts[3], NEG_INF)]
            vts += [v[hs, :] for v in vwin]
        m = sink
        for s in parts:
            m = jnp.maximum(m, jnp.max(s, axis=0, keepdims=True))
        den = jnp.exp(sink - m)
        o = jnp.zeros((WA_DIM, 4 * LANES), F32)
        for s, vt in zip(parts, vts):
            p = jnp.exp(s - m)
            den = den + jnp.sum(p, axis=0, keepdims=True)
            o = o + jnp.dot(vt, p.astype(BF16), preferred_element_type=F32)
        o = o / den
        outs += [o[:, LANES * g:LANES * (g + 1)] for g in range(WA_GROUP)]
    o_ref[...] = jnp.concatenate(outs, axis=0).T.astype(o_ref.dtype)


def _window_attention(q, kc, vc, sink, kl=None, vl=None):
    bsz, t, _ = q.shape
    nblk = t // LANES
    n_ctx = kc.shape[1]
    has_window = kl is not None
    vct = vc.transpose(0, 2, 1)
    sink_rows = jnp.repeat(sink.astype(F32).reshape(WA_KV_HEADS, 1, WA_GROUP), LANES, axis=-1)
    qspec = pl.BlockSpec((None, LANES, 512), lambda b, i: (b, i, 0))
    in_specs = [qspec,
                pl.BlockSpec((None, n_ctx, LANES), lambda b, i: (b, 0, 0)),
                pl.BlockSpec((None, LANES, n_ctx), lambda b, i: (b, 0, 0)),
                pl.BlockSpec((WA_KV_HEADS, 1, WA_GROUP * LANES), lambda b, i: (0, 0, 0))]
    args = [q, kc, vct, sink_rows]
    if has_window:
        prev = lambda b, i: (b, jnp.maximum(i - 1, 0), 0)
        here = lambda b, i: (b, i, 0)
        nxt = lambda b, i: (b, jnp.minimum(i + 1, nblk - 1), 0)
        in_specs += 2 * [pl.BlockSpec((None, LANES, LANES), f) for f in (prev, here, nxt)]
        args += [kl, kl, kl, vl, vl, vl]
    return pl.pallas_call(
        functools.partial(_wa_kernel, has_window=has_window),
        grid=(bsz, nblk),
        in_specs=in_specs,
        out_specs=qspec,
        out_shape=jax.ShapeDtypeStruct(q.shape, BF16),
        compiler_params=_params("parallel", "parallel"),
        name="window_attention" if has_window else "context_gqa",
    )(*args)


def _att_out_kernel(x_ref, oa_ref, ob_ref, w_ref, gate_ref, o_ref):
    o = jnp.dot(oa_ref[...], w_ref[0:512, :], preferred_element_type=F32)
    o = o + jnp.dot(ob_ref[...], w_ref[512:1024, :], preferred_element_type=F32)
    o_ref[...] = x_ref[...] + gate_ref[...] * o


def _att_out(x, oa, ob, w_out, gate):
    bsz, t, d = x.shape
    tm = min(ROW_TILE, t)
    row = lambda w: pl.BlockSpec((None, tm, w), lambda b, i: (b, i, 0))
    return pl.pallas_call(
        _att_out_kernel,
        grid=(bsz, t // tm),
        in_specs=[row(d), row(512), row(512), _resident(w_out.shape),
                  pl.BlockSpec((None, 1, d), lambda b, i: (b, 0, 0))],
        out_specs=row(d),
        out_shape=jax.ShapeDtypeStruct(x.shape, F32),
        compiler_params=_params("parallel", "parallel"),
        name="att_out",
    )(x, oa, ob, w_out, gate)


def _ffn_kernel(xm_ref, xp_ref, xn_ref, g_ref, sh_ref, sc_ref, gate_ref, wg_ref, wu_ref, cw_ref,
                wd_ref, og_ref, o_ref, gs_ref, *, final_norm):
    i = pl.program_id(1)
    first = i == 0
    last = i == pl.num_programs(1) - 1
    tm = xm_ref.shape[0]
    xm = xm_ref[...]
    g, sh, sc = g_ref[...], sh_ref[...], sc_ref[...]
    hm = _modulated(xm, g, sh, sc)
    hh = _modulated(jnp.concatenate([xp_ref[...], xn_ref[...]], axis=0), g, sh, sc)
    acc = jnp.zeros((tm, xm.shape[1]), F32)
    n_chunks = D_FF // FFN_CHUNK

    def up(c):
        cs = slice(c * FFN_CHUNK, (c + 1) * FFN_CHUNK)
        return (jnp.dot(hm, wg_ref[:, cs], preferred_element_type=F32),
                jnp.dot(hh, wg_ref[:, cs], preferred_element_type=F32),
                jnp.dot(hm, wu_ref[:, cs], preferred_element_type=F32))

    nxt = up(0)
    for c in range(n_chunks):
        cs = slice(c * FFN_CHUNK, (c + 1) * FFN_CHUNK)
        gm, gh, u = nxt
        if c + 1 < n_chunks:
            nxt = up(c + 1)
        gs_ref[0:SUBLANES, :] = jnp.where(first, 0.0, gh[0:SUBLANES, :])
        gs_ref[SUBLANES:SUBLANES + tm, :] = gm
        gs_ref[SUBLANES + tm:2 * SUBLANES + tm, :] = jnp.where(last, 0.0, gh[SUBLANES:, :])
        cw = cw_ref[:, cs]
        gc = (cw[0:1, :] * gs_ref[SUBLANES - 1:SUBLANES - 1 + tm, :] + cw[1:2, :] * gm
              + cw[2:3, :] * gs_ref[SUBLANES + 1:SUBLANES + 1 + tm, :])
        act = (gc * _sigmoid(gc) * u).astype(BF16)
        acc = acc + jnp.dot(act, wd_ref[cs, :], preferred_element_type=F32)
    y = xm + gate_ref[...] * acc
    if final_norm:
        y = y * lax.rsqrt(jnp.mean(y * y, axis=-1, keepdims=True) + NORM_EPS) * og_ref[...]
    o_ref[...] = y


def _ffn(x, g, sh, sc, gate, w_g, w_u, conv_w, w_d, out_g, final_norm):
    bsz, t, d = x.shape
    tm = min(ROW_TILE, t)
    nb = tm // SUBLANES
    last_blk = t // SUBLANES - 1
    vec = pl.BlockSpec((None, 1, d), lambda b, i: (b, 0, 0))
    return pl.pallas_call(
        functools.partial(_ffn_kernel, final_norm=final_norm),
        grid=(bsz, t // tm),
        in_specs=[pl.BlockSpec((None, tm, d), lambda b, i: (b, i, 0)),
                  pl.BlockSpec((None, SUBLANES, d), lambda b, i: (b, jnp.maximum(i * nb - 1, 0), 0)),
                  pl.BlockSpec((None, SUBLANES, d),
                               lambda b, i: (b, jnp.minimum((i + 1) * nb, last_blk), 0)),
                  pl.BlockSpec((1, d), lambda b, i: (0, 0)), vec, vec, vec,
                  _resident(w_g.shape), _resident(w_u.shape), _resident(conv_w.shape),
                  _resident(w_d.shape), pl.BlockSpec((1, d), lambda b, i: (0, 0))],
        out_specs=pl.BlockSpec((None, tm, d), lambda b, i: (b, i, 0)),
        out_shape=jax.ShapeDtypeStruct(x.shape, F32),
        scratch_shapes=[pltpu.VMEM((tm + 2 * SUBLANES, FFN_CHUNK), F32)],
        compiler_params=_params("parallel", "parallel"),
        name="conv_ffn",
    )(x, x, x, g, sh, sc, gate, w_g, w_u, conv_w, w_d, out_g)


def _rec_inproj_kernel(x_ref, g_ref, sh_ref, sc_ref, w_ref, gl_ref, rl_ref, ul_ref):
    h = _modulated(x_ref[...], g_ref[...], sh_ref[...], sc_ref[...])
    z = jnp.dot(h, w_ref[...], preferred_element_type=F32)
    gl_ref[...] = z[:, 0:LRU_WIDTH]
    rl_ref[...] = z[:, LRU_WIDTH:2 * LRU_WIDTH]
    ul_ref[...] = z[:, 2 * LRU_WIDTH:]


def _rec_inproj(x, g, sh, sc, w_in):
    bsz, t, d = x.shape
    tm = min(ROW_TILE, t)
    row = lambda w: pl.BlockSpec((None, tm, w), lambda b, i: (b, i, 0))
    vec = pl.BlockSpec((None, 1, d), lambda b, i: (b, 0, 0))
    outs = [LRU_WIDTH, LRU_WIDTH, S5_WIDTH]
    return pl.pallas_call(
        _rec_inproj_kernel,
        grid=(bsz, t // tm),
        in_specs=[row(d), pl.BlockSpec((1, d), lambda b, i: (0, 0)), vec, vec, _resident(w_in.shape)],
        out_specs=[row(w) for w in outs],
        out_shape=[jax.ShapeDtypeStruct((bsz, t, w), F32) for w in outs],
        compiler_params=_params("parallel", "parallel"),
        name="rec_inproj",
    )(x, g, sh, sc, w_in)


def _lru_direction(xm_ref, xp_ref, xn_ref, at_start, at_end, cw_ref, cb_ref, w_ref, bias_ref, nsp,
                   out_ref, carry_ref, xs_ref, a_ref, b_ref, reverse):
    tm = xm_ref.shape[0]
    xs_ref[0:SUBLANES, :] = jnp.where(at_start, 0.0, xp_ref[...])
    xs_ref[SUBLANES:SUBLANES + tm, :] = xm_ref[...]
    xs_ref[SUBLANES + tm:2 * SUBLANES + tm, :] = jnp.where(at_end, 0.0, xn_ref[...])
    cw = cw_ref[...]
    rc = cb_ref[...] + cw[2:3, :] * xm_ref[...]
    for k in (0, 1, 3):
        rc = rc + cw[k:k + 1, :] * xs_ref[SUBLANES - 2 + k:SUBLANES - 2 + k + tm, :]
    gz = jnp.dot(rc.astype(BF16), w_ref[...], preferred_element_type=F32) + bias_ref[...]
    r = _sigmoid(gz[:, 0:LRU_WIDTH])
    ig = _sigmoid(gz[:, LRU_WIDTH:])
    a = jnp.exp(nsp * r)
    b = jnp.sqrt(1.0 - a * a) * (ig * rc)
    ngroups = tm // SUBLANES
    a = a.reshape(ngroups, SUBLANES, LRU_WIDTH)
    b = b.reshape(ngroups, SUBLANES, LRU_WIDTH)
    row8 = lax.broadcasted_iota(jnp.int32, a.shape, 1)
    for s in (1, 2, 4):
        if reverse:
            keep = row8 <= SUBLANES - 1 - s
            shift = SUBLANES - s
        else:
            keep = row8 >= s
            shift = s
        a_sh = jnp.where(keep, pltpu.roll(a, shift, 1), 1.0)
        b_sh = jnp.where(keep, pltpu.roll(b, shift, 1), 0.0)
        b = b + a * b_sh
        a = a * a_sh
    a_ref[...] = a.reshape(tm, LRU_WIDTH)
    b_ref[...] = b.reshape(tm, LRU_WIDTH)
    edge = 0 if reverse else SUBLANES - 1

    def body(j, carry):
        grp = (ngroups - 1 - j) if reverse else j
        off = pl.multiple_of(grp * SUBLANES, SUBLANES)
        h = b_ref[pl.ds(off, SUBLANES), :] + a_ref[pl.ds(off, SUBLANES), :] * carry
        out_ref[pl.ds(off, SUBLANES), :] = h
        return h[edge:edge + 1, :]

    carry_ref[...] = lax.fori_loop(0, ngroups, body, carry_ref[...])


def _lru_kernel(fm_ref, fp_ref, fn_ref, bm_ref, bp_ref, bn_ref, cw_ref, cb_ref, wf_ref, wb_ref,
                biasf_ref, biasb_ref, nsp_ref, h0_ref, hf_ref, hb_ref, hlast_ref,
                carry_f, carry_b, xs_ref, a_ref, b_ref):
    i = pl.program_id(1)
    first = i == 0
    last = i == pl.num_programs(1) - 1

    @pl.when(first)
    def _():
        carry_f[...] = h0_ref[0]
        carry_b[...] = h0_ref[1]

    _lru_direction(fm_ref, fp_ref, fn_ref, first, last, cw_ref, cb_ref, wf_ref, biasf_ref, nsp_ref[0],
                   hf_ref, carry_f, xs_ref, a_ref, b_ref, reverse=False)
    _lru_direction(bm_ref, bp_ref, bn_ref, last, first, cw_ref, cb_ref, wb_ref, biasb_ref, nsp_ref[1],
                   hb_ref, carry_b, xs_ref, a_ref, b_ref, reverse=True)
    hlast_ref[0] = carry_f[...]
    hlast_ref[1] = carry_b[...]


def _lru(rl, conv_w, conv_b, w_f, w_b, bias_f, bias_b, nsp, h0):
    bsz, t, c = rl.shape
    tm = min(ROW_TILE, t)
    nt = t // tm
    nb = tm // SUBLANES
    last_blk = t // SUBLANES - 1
    fwd = lambda b, i: (b, i, 0)
    bwd = lambda b, i: (b, nt - 1 - i, 0)
    fprev = lambda b, i: (b, jnp.maximum(i * nb - 1, 0), 0)
    fnext = lambda b, i: (b, jnp.minimum((i + 1) * nb, last_blk), 0)
    bprev = lambda b, i: (b, jnp.maximum((nt - 1 - i) * nb - 1, 0), 0)
    bnext = lambda b, i: (b, jnp.minimum((nt - i) * nb, last_blk), 0)
    main = lambda f: pl.BlockSpec((None, tm, c), f)
    halo = lambda f: pl.BlockSpec((None, SUBLANES, c), f)
    state = pl.BlockSpec((None, 2, 1, c), lambda b, i: (b, 0, 0, 0))
    return pl.pallas_call(
        _lru_kernel,
        grid=(bsz, nt),
        in_specs=[main(fwd), halo(fprev), halo(fnext), main(bwd), halo(bprev), halo(bnext),
                  _resident(conv_w.shape), _resident(conv_b.shape), _resident(w_f.shape),
                  _resident(w_b.shape), _resident(bias_f.shape), _resident(bias_b.shape),
                  _resident(nsp.shape), state],
        out_specs=[main(fwd), main(bwd), state],
        out_shape=[jax.ShapeDtypeStruct(rl.shape, F32), jax.ShapeDtypeStruct(rl.shape, F32),
                   jax.ShapeDtypeStruct((bsz, 2, 1, c), F32)],
        scratch_shapes=[pltpu.VMEM((1, c), F32), pltpu.VMEM((1, c), F32),
                        pltpu.VMEM((tm + 2 * SUBLANES, c), F32),
                        pltpu.VMEM((tm, c), F32), pltpu.VMEM((tm, c), F32)],
        compiler_params=_params("parallel", "arbitrary"),
        name="rglru",
    )(rl, rl, rl, rl, rl, rl, conv_w, conv_b, w_f, w_b, bias_f, bias_b, nsp, h0)


def _s5_in_kernel(u_ref, t_ref, bs_ref, yi_ref, xre_ref, xim_ref):
    x = jnp.zeros((u_ref.shape[1], 4 * LANES), F32)
    for q in range(2):
        ub = u_ref[q].astype(BF16)
        yi_ref[q] = jnp.dot(ub, t_ref[q], preferred_element_type=F32)
        x = x + jnp.dot(ub, bs_ref[q], preferred_element_type=F32)
    for d in range(2):
        xre_ref[d] = x[:, 2 * d * LANES:(2 * d + 1) * LANES]
        xim_ref[d] = x[:, (2 * d + 1) * LANES:(2 * d + 2) * LANES]


def _s5_scan_kernel(xre_ref, xim_ref, h0re_ref, h0im_ref, shr_ref, shi_ref, cwr_ref, cwi_ref,
                    sre_ref, sim_ref, lre_ref, lim_ref, pre_ref, pim_ref, *, reverse):
    nc, n = xre_ref.shape
    xr = xre_ref[...]
    xi = xim_ref[...]
    row8 = jnp.bitwise_and(lax.broadcasted_iota(jnp.int32, (nc, n), 0), SUBLANES - 1)
    for k, s in enumerate((1, 2, 4)):
        keep = (row8 <= SUBLANES - 1 - s) if reverse else (row8 >= s)
        shift = nc - s if reverse else s
        xr_sh = jnp.where(keep, pltpu.roll(xr, shift, 0), 0.0)
        xi_sh = jnp.where(keep, pltpu.roll(xi, shift, 0), 0.0)
        mr = shr_ref[k]
        mi = shi_ref[k]
        xr, xi = xr + (mr * xr_sh - mi * xi_sh), xi + (mr * xi_sh + mi * xr_sh)
    pre_ref[...] = xr
    pim_ref[...] = xi
    cwr = cwr_ref[...]
    cwi = cwi_ref[...]
    row = lax.broadcasted_iota(jnp.int32, (SUBLANES, n), 0)
    ngroups = nc // SUBLANES
    edge = 0 if reverse else SUBLANES - 1
    entry = SUBLANES - 1 - edge
    step = SUBLANES - 1 if reverse else 1

    def body(j, carry):
        cr, ci = carry
        grp = (ngroups - 1 - j) if reverse else j
        off = pl.multiple_of(grp * SUBLANES, SUBLANES)
        sr = pre_ref[pl.ds(off, SUBLANES), :] + (cwr * cr - cwi * ci)
        si = pim_ref[pl.ds(off, SUBLANES), :] + (cwr * ci + cwi * cr)
        pre_ref[pl.ds(off, SUBLANES), :] = jnp.where(row == entry, cr, pltpu.roll(sr, step, 0))
        pim_ref[pl.ds(off, SUBLANES), :] = jnp.where(row == entry, ci, pltpu.roll(si, step, 0))
        return sr[edge:edge + 1, :], si[edge:edge + 1, :]

    cr, ci = lax.fori_loop(0, ngroups, body, (h0re_ref[...], h0im_ref[...]))
    lre_ref[...] = cr
    lim_ref[...] = ci
    sre_ref[...] = pre_ref[...].astype(BF16)
    sim_ref[...] = pim_ref[...].astype(BF16)


def _s5_out_kernel(yi_ref, u_ref, sre_ref, sim_ref, cs_ref, d_ref, y_ref):
    s = jnp.concatenate([sre_ref[0], sim_ref[0], sre_ref[1], sim_ref[1]], axis=-1)
    for q in range(2):
        y = yi_ref[q] + d_ref[q] * u_ref[q]
        y_ref[q] = y + jnp.dot(s, cs_ref[q], preferred_element_type=F32)


def _s5(u, prm, h0re, h0im):
    bsz, t, _ = u.shape
    nc = t // S5_CHUNK
    ug = u.reshape(bsz, nc, S5_CHUNK, S5_GROUPS, S5_GROUP).transpose(0, 3, 1, 2, 4)
    ug = ug.reshape(bsz, S5_GROUPS, nc, 256)
    npair = S5_GROUPS // 2
    pair = pl.BlockSpec((None, 2, nc, 256), lambda b, p: (b, p, 0, 0))
    wpair = lambda r, w: pl.BlockSpec((2, r, w), lambda b, p: (p, 0, 0))
    lanes = pl.BlockSpec((None, 2, nc, LANES), lambda b, p: (b, 0, 0, p))
    state_shape = (bsz, 2, nc, S5_LANES)
    yi, xre, xim = pl.pallas_call(
        _s5_in_kernel,
        grid=(bsz, npair),
        in_specs=[pair, wpair(256, 256), wpair(256, 4 * LANES)],
        out_specs=[pair, lanes, lanes],
        out_shape=[jax.ShapeDtypeStruct(ug.shape, F32), jax.ShapeDtypeStruct(state_shape, F32),
                   jax.ShapeDtypeStruct(state_shape, F32)],
        compiler_params=_params("parallel", "parallel"),
        name="s5_in",
    )(ug, prm["toep"], prm["bs"])

    states = []
    for d in range(2):
        full = pl.BlockSpec((None, None, nc, S5_LANES), lambda b, d=d: (b, d, 0, 0))
        one = pl.BlockSpec((None, None, 1, S5_LANES), lambda b, d=d: (b, d, 0, 0))
        tab = pl.BlockSpec((None, 3, 1, S5_LANES), lambda b, d=d: (d, 0, 0, 0))
        tab8 = pl.BlockSpec((None, SUBLANES, S5_LANES), lambda b, d=d: (d, 0, 0))
        plain = pl.BlockSpec((None, nc, S5_LANES), lambda b: (b, 0, 0))
        plain1 = pl.BlockSpec((None, 1, S5_LANES), lambda b: (b, 0, 0))
        states.append(pl.pallas_call(
            functools.partial(_s5_scan_kernel, reverse=bool(d)),
            grid=(bsz,),
            in_specs=[full, full, one, one, tab, tab, tab8, tab8],
            out_specs=[plain, plain, plain1, plain1],
            out_shape=[jax.ShapeDtypeStruct((bsz, nc, S5_LANES), BF16),
                       jax.ShapeDtypeStruct((bsz, nc, S5_LANES), BF16),
                       jax.ShapeDtypeStruct((bsz, 1, S5_LANES), F32),
                       jax.ShapeDtypeStruct((bsz, 1, S5_LANES), F32)],
            scratch_shapes=[pltpu.VMEM((nc, S5_LANES), F32), pltpu.VMEM((nc, S5_LANES), F32)],
            compiler_params=_params("parallel"),
            name="s5_scan_bwd" if d else "s5_scan_fwd",
        )(xre, xim, h0re, h0im, prm["sh_r"], prm["sh_i"], prm["cw_r"], prm["cw_i"]))
    sre, sim, lre, lim = [jnp.stack([states[0][k], states[1][k]], axis=1) for k in range(4)]

    y = pl.pallas_call(
        _s5_out_kernel,
        grid=(bsz, npair),
        in_specs=[pair, pair, lanes, lanes, wpair(4 * LANES, 256), wpair(1, 256)],
        out_specs=pair,
        out_shape=jax.ShapeDtypeStruct(ug.shape, F32),
        compiler_params=_params("parallel", "parallel"),
        name="s5_out",
    )(yi, ug, sre, sim, prm["cs"], prm["dskip"])
    y = y.reshape(bsz, S5_GROUPS, nc, S5_CHUNK, S5_GROUP).transpose(0, 2, 3, 1, 4)
    return y.reshape(bsz, t, S5_WIDTH), lre, lim


def _s5_prepare(a_re, a_im, log_step, b_re, b_im, c_re, c_im, d_skip):
    a_re, a_im = a_re.astype(F32), a_im.astype(F32)
    step = jnp.exp(log_step.astype(F32))[..., None]
    mag = jnp.exp(a_re * step)
    abr, abi = mag * jnp.cos(a_im * step), mag * jnp.sin(a_im * step)
    den = a_re * a_re + a_im * a_im
    qr = ((abr - 1.0) * a_re + abi * a_im) / den
    qi = (abi * a_re - (abr - 1.0) * a_im) / den
    br_, bi_ = b_re.astype(F32), b_im.astype(F32)
    bbr = qr[..., None] * br_ - qi[..., None] * bi_
    bbi = qr[..., None] * bi_ + qi[..., None] * br_
    cr, ci = c_re.astype(F32), c_im.astype(F32)

    def power(n):
        n = jnp.asarray(n, F32)[..., None, None, None]
        m = jnp.exp(n * a_re * step)
        return m * jnp.cos(n * a_im * step), m * jnp.sin(n * a_im * step)

    lc = S5_CHUNK
    pr, pi = power(jnp.arange(lc + 1))
    mr = pr[..., None] * bbr - pi[..., None] * bbi
    mi = pr[..., None] * bbi + pi[..., None] * bbr
    hp = lax.Precision.HIGHEST
    kern = (jnp.einsum('dgop,jdgpi->jdgoi', cr, mr[:lc], precision=hp)
            - jnp.einsum('dgop,jdgpi->jdgoi', ci, mi[:lc], precision=hp))
    s_idx = jnp.arange(lc)[:, None]
    r_idx = jnp.arange(lc)[None, :]
    lag = jnp.clip(r_idx - s_idx, 0, lc - 1)
    toep = jnp.where((r_idx >= s_idx)[:, :, None, None, None, None], kern[lag], 0.0)
    toep = toep.transpose(2, 3, 0, 5, 1, 4).reshape(2, S5_GROUPS, lc * S5_GROUP, lc * S5_GROUP)
    rev = jnp.arange(lc - 1, -1, -1)
    bs_re = mr[rev].transpose(1, 2, 0, 4, 3).reshape(2, S5_GROUPS, lc * S5_GROUP, S5_STATE)
    bs_im = mi[rev].transpose(1, 2, 0, 4, 3).reshape(2, S5_GROUPS, lc * S5_GROUP, S5_STATE)
    pr1, pi1 = pr[1:], pi[1:]
    cs_re = cr[None] * pr1[:, :, :, None, :] - ci[None] * pi1[:, :, :, None, :]
    cs_im = -(cr[None] * pi1[:, :, :, None, :] + ci[None] * pr1[:, :, :, None, :])
    cs_re = cs_re.transpose(1, 2, 4, 0, 3).reshape(2, S5_GROUPS, S5_STATE, lc * S5_GROUP)
    cs_im = cs_im.transpose(1, 2, 4, 0, 3).reshape(2, S5_GROUPS, S5_STATE, lc * S5_GROUP)
    half = (jnp.arange(S5_GROUPS) % 2)[None, :, None, None]

    def pad_cols(m):
        z = jnp.zeros_like(m)
        return jnp.where(half == 0, jnp.concatenate([m, z], -1), jnp.concatenate([z, m], -1))

    def pad_rows(m):
        z = jnp.zeros_like(m)
        return jnp.where(half == 0, jnp.concatenate([m, z], -2), jnp.concatenate([z, m], -2))

    def mirror_rows(m):
        shp = m.shape
        return m.reshape(shp[0], lc, shp[1] // lc, shp[2])[:, ::-1].reshape(shp)

    def mirror_cols(m):
        shp = m.shape
        return m.reshape(shp[0], shp[1], lc, shp[2] // lc)[:, :, ::-1].reshape(shp)

    flat = lambda x: x.reshape(x.shape[:-2] + (S5_LANES,))
    shr, shi = power(jnp.asarray([lc, 2 * lc, 4 * lc]))
    cwr, cwi = power(lc * jnp.arange(1, SUBLANES + 1))
    cwr, cwi = flat(cwr).transpose(1, 0, 2), flat(cwi).transpose(1, 0, 2)
    bs_re, bs_im = pad_cols(bs_re), pad_cols(bs_im)
    cs_re, cs_im = pad_rows(cs_re), pad_rows(cs_im)
    return {
        "toep": (toep[0] + mirror_cols(mirror_rows(toep[1]))).astype(BF16),
        "bs": jnp.concatenate([bs_re[0], bs_im[0], mirror_rows(bs_re[1]), mirror_rows(bs_im[1])],
                              axis=-1).astype(BF16),
        "cs": jnp.concatenate([cs_re[0], cs_im[0], mirror_cols(cs_re[1]), mirror_cols(cs_im[1])],
                              axis=-2).astype(BF16),
        "sh_r": flat(shr).transpose(1, 0, 2)[:, :, None, :], "sh_i": flat(shi).transpose(1, 0, 2)[:, :, None, :],
        "cw_r": jnp.stack([cwr[0], cwr[1, ::-1]]), "cw_i": jnp.stack([cwi[0], cwi[1, ::-1]]),
        "dskip": jnp.tile(d_skip.astype(F32).reshape(S5_GROUPS, 1, S5_GROUP), (1, 1, lc)),
    }


def _rec_out_kernel(x_ref, gl_ref, hf_ref, hb_ref, y_ref, wglu_ref, w_ref, gate_ref, o_ref):
    t1 = ((hf_ref[...] + hb_ref[...]) * _gelu(gl_ref[...])).astype(BF16)
    yg = _gelu(y_ref[...])
    t2 = (yg * _sigmoid(jnp.dot(yg.astype(BF16), wglu_ref[...], preferred_element_type=F32))).astype(BF16)
    o = jnp.dot(t1, w_ref[0:LRU_WIDTH, :], preferred_element_type=F32)
    o = o + jnp.dot(t2, w_ref[LRU_WIDTH:, :], preferred_element_type=F32)
    o_ref[...] = x_ref[...] + gate_ref[...] * o


def _rec_out(x, gl, hf, hb, y, w_glu, w_out, gate):
    bsz, t, d = x.shape
    tm = min(ROW_TILE, t)
    row = lambda w: pl.BlockSpec((None, tm, w), lambda b, i: (b, i, 0))
    return pl.pallas_call(
        _rec_out_kernel,
        grid=(bsz, t // tm),
        in_specs=[row(d), row(LRU_WIDTH), row(LRU_WIDTH), row(LRU_WIDTH), row(S5_WIDTH),
                  _resident(w_glu.shape), _resident(w_out.shape),
                  pl.BlockSpec((None, 1, d), lambda b, i: (b, 0, 0))],
        out_specs=row(d),
        out_shape=jax.ShapeDtypeStruct(x.shape, F32),
        compiler_params=_params("parallel", "parallel"),
        name="rec_out",
    )(x, gl, hf, hb, y, w_glu, w_out, gate)


def _rope_tables(n, dim):
    rows = n // GRID_W
    row = jnp.repeat(jnp.arange(rows), GRID_W).astype(F32)
    col = jnp.tile(jnp.arange(GRID_W), rows).astype(F32)
    half = dim // 2
    freqs = ROPE_THETA ** (-jnp.arange(0, half, 2, dtype=F32) / half)

    def angles(pos):
        a = pos[:, None] * freqs[None, :]
        return jnp.concatenate([a, a], axis=-1)

    ang = jnp.concatenate([angles(row), angles(col)], axis=-1)
    reps = LANES // dim
    return jnp.tile(jnp.cos(ang), (1, reps)), jnp.tile(jnp.sin(ang), (1, reps))


def _rotated_columns(w, dim):
    quarter = dim // 4
    idx = jnp.arange(w.shape[1])
    low = (idx % (2 * quarter)) < quarter
    src = jnp.where(low, idx + quarter, idx - quarter)
    return jnp.where(low[None, :], -1.0, 1.0) * w[:, src]


def _att_weights(w_in):
    aq, ak = w_in[:, 0:512], w_in[:, 512:1024]
    bq, bk = w_in[:, 1536:2048], w_in[:, 2048:2176]
    ext = [w_in, _rotated_columns(aq, DA_DIM), _rotated_columns(ak, DA_DIM),
           _rotated_columns(bq, WA_DIM), _rotated_columns(bk, WA_DIM)]
    return jnp.concatenate(ext, axis=1).astype(BF16)


def _block_diag(w):
    eye = jnp.eye(LRU_BLOCKS, dtype=w.dtype)
    return jnp.einsum('nij,nm->nimj', w, eye).reshape(LRU_WIDTH, LRU_WIDTH)


def _vec3(v, bsz):
    return v.reshape(-1, 1, v.shape[-1]) if v.shape[0] == bsz else jnp.broadcast_to(v[None], (bsz, 1, v.shape[-1]))


def kernel(x, c, ctx, c_ctx, ada_w, ada_b, norm1_g, norm2_g, att_w_in, att_w_out, da_lam_q1, da_lam_k1, da_lam_q2, da_lam_k2, da_subln_g, wa_sink, rec_w_in, rec_w_out, lru_conv_w, lru_conv_b, lru_w_a, lru_b_a, lru_w_x, lru_b_x, lru_lam, s5_a_re, s5_a_im, s5_log_step, s5_b_re, s5_b_im, s5_c_re, s5_c_im, s5_d, s5_w_glu, ffn_w_g, ffn_w_u, ffn_conv_w, ffn_w_down, final_g):
    bsz, n_lat, d = x.shape
    n_ctx = ctx.shape[1]
    assert d == D_MODEL and bsz + 1 <= SUBLANES
    assert n_lat % ROW_TILE == 0 and n_ctx % DA_TK == 0 and n_ctx <= ROW_TILE
    x = x.astype(F32)
    ctx = ctx.astype(F32)

    cond = jnp.zeros((SUBLANES, d), F32).at[:bsz].set(c.astype(F32)).at[bsz].set(c_ctx.astype(F32))
    mods = _ada_all(cond, ada_w.astype(F32), ada_b.astype(F32))

    ca, sa = _rope_tables(n_lat, DA_DIM)
    cb, sb = _rope_tables(n_lat, WA_DIM)
    fg = final_g.reshape(1, d).astype(F32)
    ones = jnp.ones((n_ctx, LANES), F32)
    zeros = jnp.zeros((n_ctx, LANES), F32)

    for l in range(DEPTH):
        ctx_out = l < DEPTH - 1
        m = mods[l]
        sh1, sc1, g1, sh2, sc2, g2 = [_vec3(v, bsz) for v in jnp.split(m[:bsz], 6, axis=-1)]
        csh1, csc1, cg1, csh2, csc2, cg2 = [_vec3(v, bsz) for v in jnp.split(m[bsz:bsz + 1], 6, axis=-1)]
        n1 = norm1_g[l].reshape(1, d).astype(F32)
        n2 = norm2_g[l].reshape(1, d).astype(F32)
        j = l // 2
        if l % 2 == 0:
            lam_init = 0.8 - 0.6 * math.exp(-0.3 * l)
            w_ext = _att_weights(att_w_in[j].astype(F32))
            w_out = att_w_out[j].astype(BF16)
            lam = (jnp.exp(jnp.sum(da_lam_q1[j].astype(F32) * da_lam_k1[j].astype(F32)))
                   - jnp.exp(jnp.sum(da_lam_q2[j].astype(F32) * da_lam_k2[j].astype(F32))) + lam_init)
            aql, akl, avl, bql, bkl, bvl = _att_inproj(x, n1, sh1, sc1, w_ext, ca, sa, cb, sb)
            aqc, akc, avc, bqc, bkc, bvc = _att_inproj(ctx, n1, csh1, csc1, w_ext, ones, zeros, ones, zeros)
            k_all = jnp.concatenate([akc, akl], axis=1)
            v_all = jnp.concatenate([avc, avl], axis=-1)
            oal = _diff_attention(aql, k_all, v_all, lam, da_subln_g[j], 1.0 - lam_init)
            obl = _window_attention(bql, bkc, bvc, wa_sink[j], bkl, bvl)
            x = _att_out(x, oal, obl, w_out, g1)
            if ctx_out:
                oac = _diff_attention(aqc, akc, avc, lam, da_subln_g[j], 1.0 - lam_init)
                obc = _window_attention(bqc, bkc, bvc, wa_sink[j])
                ctx = _att_out(ctx, oac, obc, w_out, cg1)
        else:
            w_in = rec_w_in[j].astype(BF16)
            w_f = jnp.concatenate([_block_diag(lru_w_a[j, 0]), _block_diag(lru_w_x[j, 0])], axis=1).astype(BF16)
            w_b = jnp.concatenate([_block_diag(lru_w_a[j, 1]), _block_diag(lru_w_x[j, 1])], axis=1).astype(BF16)
            bias_f = jnp.concatenate([lru_b_a[j, 0], lru_b_x[j, 0]]).reshape(1, -1).astype(F32)
            bias_b = jnp.concatenate([lru_b_a[j, 1], lru_b_x[j, 1]]).reshape(1, -1).astype(F32)
            nsp = (-LRU_C * jax.nn.softplus(-lru_lam[j].astype(F32))).reshape(2, 1, LRU_WIDTH)
            conv_w = lru_conv_w[j].astype(F32)
            conv_b = lru_conv_b[j].reshape(1, -1).astype(F32)
            s5p = _s5_prepare(s5_a_re[j], s5_a_im[j], s5_log_step[j], s5_b_re[j], s5_b_im[j],
                              s5_c_re[j], s5_c_im[j], s5_d[j])
            gc_, rc_, uc_ = _rec_inproj(ctx, n1, csh1, csc1, w_in)
            gl_, rl_, ul_ = _rec_inproj(x, n1, sh1, sc1, w_in)
            h0 = jnp.zeros((bsz, 2, 1, LRU_WIDTH), F32)
            hfc, hbc, hlast = _lru(rc_, conv_w, conv_b, w_f, w_b, bias_f, bias_b, nsp, h0)
            hfl, hbl, _ = _lru(rl_, conv_w, conv_b, w_f, w_b, bias_f, bias_b, nsp, hlast)
            s0 = jnp.zeros((bsz, 2, 1, S5_LANES), F32)
            yc, lre, lim = _s5(uc_, s5p, s0, s0)
            yl, _, _ = _s5(ul_, s5p, lre, lim)
            w_glu = s5_w_glu[j].astype(BF16)
            w_out = rec_w_out[j].astype(BF16)
            x = _rec_out(x, gl_, hfl, hbl, yl, w_glu, w_out, g1)
            if ctx_out:
                ctx = _rec_out(ctx, gc_, hfc, hbc, yc, w_glu, w_out, cg1)
        wg, wu, wd = ffn_w_g[l].astype(BF16), ffn_w_u[l].astype(BF16), ffn_w_down[l].astype(BF16)
        cw = ffn_conv_w[l].astype(F32)
        x = _ffn(x, n2, sh2, sc2, g2, wg, wu, cw, wd, fg, final_norm=l == DEPTH - 1)
        if ctx_out:
            ctx = _ffn(ctx, n2, csh2, csc2, cg2, wg, wu, cw, wd, fg, final_norm=False)
    return x
```

```python
import functools
import math

import jax
import jax.numpy as jnp
from jax import lax
from jax.experimental import pallas as pl
from jax.experimental.pallas import tpu as pltpu

F32 = jnp.float32
BF16 = jnp.bfloat16

D_MODEL = 1024
DEPTH = 4
GRID_W = 64
ROPE_THETA = 10000.0
NORM_EPS = 1e-6
NEG_INF = -1e30
LOG2_E = math.log2(math.e)

DA_HEADS = 8
DA_DIM = 32
WA_HEADS = 8
WA_KV_HEADS = 2
WA_DIM = 64
WA_WINDOW = 128
WA_GROUP = WA_HEADS // WA_KV_HEADS
LRU_WIDTH = 768
LRU_BLOCKS = 8
LRU_BLOCK_DIM = LRU_WIDTH // LRU_BLOCKS
LRU_CONV = 4
LRU_C = 8.0
S5_WIDTH = 256
S5_GROUP = 16
S5_GROUPS = S5_WIDTH // S5_GROUP
S5_STATE = 64
S5_CHUNK = 16
S5_LANES = S5_GROUPS * S5_STATE
D_FF = 2816
FFN_CHUNK = 1408

DA_QK = DA_HEADS * 2 * DA_DIM
DA_V = DA_HEADS * 2 * DA_DIM
WA_Q = WA_HEADS * WA_DIM
WA_KV = WA_KV_HEADS * WA_DIM
ATT_IN = 2 * DA_QK + DA_V + WA_Q + 2 * WA_KV

SUBLANES = 8
LANES = 128
VMEM_LIMIT_BYTES = 56 * 1024 * 1024
ROW_TILE = 512
DA_TQ = 512
DA_TK = 256
DA_AHEAD = 2
DA_BOUND_LIMIT = 48.0
DA_UNROLL = 13
DA_VROWS = 80


def _params(*sem):
    return pltpu.CompilerParams(dimension_semantics=sem, vmem_limit_bytes=VMEM_LIMIT_BYTES)


def _resident(shape):
    nd = len(shape)
    return pl.BlockSpec(shape, lambda *_: (0,) * nd, pipeline_mode=pl.Buffered(1))


def _sigmoid(x):
    return 1.0 / (1.0 + jnp.exp(-x))


def _gelu(x):
    return 0.5 * x * (1.0 + jnp.tanh(math.sqrt(2.0 / math.pi) * (x + 0.044715 * (x * x * x))))


def _modulated(x, g, shift, scale):
    y = x * lax.rsqrt(jnp.mean(x * x, axis=-1, keepdims=True) + NORM_EPS)
    return (y * g * (1.0 + scale) + shift).astype(BF16)


def _ada_kernel(c_ref, w_ref, b_ref, o_ref):
    c = c_ref[...]
    s = c * _sigmoid(c)
    o_ref[...] = jnp.dot(s, w_ref[...], precision=lax.Precision.HIGHEST,
                         preferred_element_type=F32) + b_ref[...]


def _ada_all(cond, ada_w, ada_b):
    depth, d, n = ada_w.shape
    tn = 1536
    return pl.pallas_call(
        _ada_kernel,
        grid=(depth, n // tn),
        in_specs=[pl.BlockSpec((SUBLANES, d), lambda l, j: (0, 0)),
                  pl.BlockSpec((None, d, tn), lambda l, j: (l, 0, j)),
                  pl.BlockSpec((None, 1, tn), lambda l, j: (l, 0, j))],
        out_specs=pl.BlockSpec((None, SUBLANES, tn), lambda l, j: (l, 0, j)),
        out_shape=jax.ShapeDtypeStruct((depth, SUBLANES, n), F32),
        compiler_params=_params("parallel", "parallel"),
        name="ada_mod",
    )(cond, ada_w, ada_b.reshape(depth, 1, n))


def _att_inproj_kernel(x_ref, g_ref, sh_ref, sc_ref, w_ref, ca_ref, sa_ref, cb_ref, sb_ref,
                       aq_ref, ak_ref, avt_ref, bq_ref, bk_ref, bv_ref):
    h = _modulated(x_ref[...], g_ref[...], sh_ref[...], sc_ref[...])
    z = jnp.dot(h, w_ref[...], preferred_element_type=F32)
    ca = jnp.tile(ca_ref[...], (1, 4))
    sa = jnp.tile(sa_ref[...], (1, 4))
    cb = cb_ref[...]
    sb = sb_ref[...]
    cb4 = jnp.tile(cb, (1, 4))
    sb4 = jnp.tile(sb, (1, 4))
    o = ATT_IN
    aq = z[:, 0:512] * ca + z[:, o:o + 512] * sa
    ak = z[:, 512:1024] * ca + z[:, o + 512:o + 1024] * sa
    bq = z[:, 1536:2048] * cb4 + z[:, o + 1024:o + 1536] * sb4
    bk = z[:, 2048:2176] * cb + z[:, o + 1536:o + 1664] * sb
    aq_ref[...] = (aq * (DA_DIM ** -0.5 * LOG2_E)).astype(BF16)
    ak_ref[...] = ak.astype(BF16)
    for hp in range(4):
        avt_ref[hp] = z[:, 1024 + LANES * hp:1024 + LANES * (hp + 1)].T.astype(BF16)
    bq_ref[...] = (bq * (WA_DIM ** -0.5)).astype(BF16)
    bk_ref[...] = bk.astype(BF16)
    bv_ref[...] = z[:, 2176:2304].astype(BF16)


def _att_inproj(x, g, sh, sc, w_ext, ca, sa, cb, sb):
    bsz, t, d = x.shape
    tm = min(ROW_TILE, t)
    n = w_ext.shape[1]
    row = lambda w: pl.BlockSpec((None, tm, w), lambda b, i: (b, i, 0))
    vec = pl.BlockSpec((None, 1, d), lambda b, i: (b, 0, 0))
    tab = pl.BlockSpec((tm, LANES), lambda b, i: (i, 0))
    rows = lambda w: (row(w), jax.ShapeDtypeStruct((bsz, t, w), BF16))
    outs = [rows(512), rows(512),
            (pl.BlockSpec((None, 4, LANES, tm), lambda b, i: (b, 0, 0, i)),
             jax.ShapeDtypeStruct((bsz, 4, LANES, t), BF16)),
            rows(512), rows(128), rows(128)]
    return pl.pallas_call(
        _att_inproj_kernel,
        grid=(bsz, t // tm),
        in_specs=[row(d), pl.BlockSpec((1, d), lambda b, i: (0, 0)), vec, vec, _resident((d, n)),
                  tab, tab, tab, tab],
        out_specs=[o[0] for o in outs],
        out_shape=[o[1] for o in outs],
        compiler_params=_params("parallel", "parallel"),
        name="att_inproj",
    )(x, g, sh, sc, w_ext, ca, sa, cb, sb)


def _da_kernel(lam_ref, q_ref, k_ref, vt_ref, g_ref, o_ref, acc_ref, s_ref, *, tk, nkt, out_scale):
    tq = q_ref.shape[0]
    qf = q_ref[...].astype(F32).T
    row = lax.broadcasted_iota(jnp.int32, qf.shape, 0)
    ones = jnp.where(lax.broadcasted_iota(jnp.int32, (DA_VROWS - 64, tk), 0) == 0, 1.0, 0.0).astype(BF16)
    w = [jnp.where(jnp.right_shift(row, 5) == s, qf, 0.0).astype(BF16) for s in range(4)]
    acc_ref[...] = jnp.zeros_like(acc_ref)

    def scores(kt, s):
        off = pl.multiple_of(kt * tk, tk)
        st = jnp.dot(k_ref[pl.ds(off, tk), :], w[s], preferred_element_type=F32)
        s_ref[s] = st
        return jnp.max(st, axis=0, keepdims=True)

    def absorb(s, vt, m, mt):
        m_new = jnp.maximum(m, mt)
        p = jnp.exp2(s_ref[s] - m_new).astype(BF16)
        pv = jnp.dot(vt[s // 2], p, preferred_element_type=F32)
        acc_ref[s] = jnp.exp2(m - m_new) * acc_ref[s] + pv
        return m_new

    def body(kt, carry):
        ms, mt0, mt1 = carry
        v2 = vt_ref[:, pl.ds(pl.multiple_of(kt * tk, tk), tk)]
        vt = [jnp.concatenate([v2[64 * h:64 * h + 64, :], ones], axis=0) for h in range(2)]
        nxt = jnp.minimum(kt + 1, nkt - 1)
        mt2 = scores(kt, 2)
        mt3 = scores(kt, 3)
        m0 = absorb(0, vt, ms[0], mt0)
        mt0 = scores(nxt, 0)
        m1 = absorb(1, vt, ms[1], mt1)
        mt1 = scores(nxt, 1)
        m2 = absorb(2, vt, ms[2], mt2)
        m3 = absorb(3, vt, ms[3], mt3)
        return (m0, m1, m2, m3), mt0, mt1

    m_init = tuple(jnp.full((1, tq), NEG_INF, F32) for _ in range(4))
    lax.fori_loop(0, nkt, body, (m_init, scores(0, 0), scores(0, 1)),
                  unroll=DA_UNROLL if nkt % DA_UNROLL == 0 else 1)
    lam = lam_ref[0]
    ys = []
    for h in range(2):
        a0 = acc_ref[2 * h]
        a1 = acc_ref[2 * h + 1]
        o = a0[0:64, :] / a0[64:65, :] - lam * (a1[0:64, :] / a1[64:65, :])
        y = o * lax.rsqrt(jnp.mean(o * o, axis=0, keepdims=True) + NORM_EPS)
        ys.append(y * g_ref[64 * h:64 * h + 64, :] * out_scale)
    o_ref[...] = jnp.concatenate(ys, axis=0).T.astype(o_ref.dtype)


def _da_bounded_kernel(lam_ref, kn_ref, q_ref, k_ref, vt_ref, g_ref, o_ref, acc_ref, *, tk, nkt, out_scale):
    b, hp = pl.program_id(0), pl.program_id(1)
    tq = q_ref.shape[0]
    qf = q_ref[...].astype(F32).T
    row = lax.broadcasted_iota(jnp.int32, qf.shape, 0)
    ones = jnp.where(lax.broadcasted_iota(jnp.int32, (DA_VROWS - 64, tk), 0) == 0, 1.0, 0.0).astype(BF16)
    one_col = jnp.where(lax.broadcasted_iota(jnp.int32, (tk, LANES), 1) == 0, 1.0, 0.0).astype(BF16)
    w = []
    for s in range(4):
        qs = jnp.where(jnp.right_shift(row, 5) == s, qf, 0.0)
        bound = jnp.sqrt(jnp.sum(qs * qs, axis=0, keepdims=True)) * kn_ref[b, 4 * hp + s]
        w.append(jnp.concatenate([qs, jnp.where(row == 0, -bound, 0.0)], axis=0).astype(BF16))
    acc_ref[...] = jnp.zeros_like(acc_ref)
    unroll = DA_UNROLL if nkt % DA_UNROLL == 0 else 1
    steps = [(u, s) for u in range(unroll) for s in range(4)]

    def trip(j, carry):
        def keys(u):
            off = pl.multiple_of((j * unroll + u) * tk, tk)
            return jnp.concatenate([k_ref[pl.ds(off, tk), :], one_col], axis=1)

        def values(u):
            off = pl.multiple_of((j * unroll + u) * tk, tk)
            v2 = vt_ref[:, pl.ds(off, tk)]
            return [jnp.concatenate([v2[64 * h:64 * h + 64, :], ones], axis=0) for h in range(2)]

        def score(n):
            u, s = steps[n]
            return jnp.dot(keys(u), w[s], preferred_element_type=F32)

        queue = [score(n) for n in range(min(DA_AHEAD, len(steps)))]
        for n, (u, s) in enumerate(steps):
            if n + DA_AHEAD < len(steps):
                queue.append(score(n + DA_AHEAD))
            p = jnp.exp2(queue.pop(0)).astype(BF16)
            acc_ref[s] = acc_ref[s] + jnp.dot(values(u)[s // 2], p, preferred_element_type=F32)
        return carry

    lax.fori_loop(0, nkt // unroll, trip, 0)
    lam = lam_ref[0]
    ys = []
    for h in range(2):
        a0 = acc_ref[2 * h]
        a1 = acc_ref[2 * h + 1]
        o = a0[0:64, :] / a0[64:65, :] - lam * (a1[0:64, :] / a1[64:65, :])
        y = o * lax.rsqrt(jnp.mean(o * o, axis=0, keepdims=True) + NORM_EPS)
        ys.append(y * g_ref[64 * h:64 * h + 64, :] * out_scale)
    o_ref[...] = jnp.concatenate(ys, axis=0).T.astype(o_ref.dtype)


def _diff_attention(q, k_all, vt_all, lam, subln_g, out_scale):
    bsz, tq_all, _ = q.shape
    tk_all = k_all.shape[1]
    tq = min(DA_TQ, tq_all)
    tk = DA_TK
    nkt = tk_all // tk
    gcol = jnp.tile(subln_g.astype(F32), 2).reshape(LANES, 1)
    lam1 = lam.reshape(1).astype(F32)
    smem = pl.BlockSpec(memory_space=pltpu.SMEM)
    specs = [pl.BlockSpec((None, tq, LANES), lambda b, hp, i: (b, i, hp)),
             pl.BlockSpec((None, tk_all, LANES), lambda b, hp, i: (b, 0, hp)),
             pl.BlockSpec((None, None, LANES, tk_all), lambda b, hp, i: (b, hp, 0, 0)),
             pl.BlockSpec((LANES, 1), lambda b, hp, i: (0, 0))]
    common = dict(grid=(bsz, 4, tq_all // tq),
                  out_specs=pl.BlockSpec((None, tq, LANES), lambda b, hp, i: (b, i, hp)),
                  out_shape=jax.ShapeDtypeStruct((bsz, tq_all, 512), BF16),
                  compiler_params=_params("parallel", "parallel", "arbitrary"))

    def running_max(_):
        return pl.pallas_call(
            functools.partial(_da_kernel, tk=tk, nkt=nkt, out_scale=out_scale),
            in_specs=[smem] + specs,
            scratch_shapes=[pltpu.VMEM((4, DA_VROWS, tq), F32), pltpu.VMEM((4, tk, tq), F32)],
            name="diff_attention", **common)(lam1, q, k_all, vt_all, gcol)

    def bounded(kn):
        return pl.pallas_call(
            functools.partial(_da_bounded_kernel, tk=tk, nkt=nkt, out_scale=out_scale),
            in_specs=[smem, smem] + specs,
            scratch_shapes=[pltpu.VMEM((4, DA_VROWS, tq), F32)],
            name="diff_attention_bounded", **common)(lam1, kn, q, k_all, vt_all, gcol)

    norms = lambda z: jnp.sqrt(jnp.max(jnp.sum(jnp.square(z.astype(F32).reshape(bsz, -1, 16, DA_DIM)), -1), 1))
    kn = norms(k_all)
    safe = jnp.max(norms(q) * kn) < DA_BOUND_LIMIT
    return lax.cond(safe, bounded, running_max, kn)


def _wa_kernel(*refs, has_window):
    if has_window:
        (q_ref, kc_ref, vct_ref, sink_ref, kp_ref, km_ref, kn_ref, vp_ref, vm_ref, vn_ref, o_ref) = refs
    else:
        (q_ref, kc_ref, vct_ref, sink_ref, o_ref) = refs
    i = pl.program_id(1)
    nblk = pl.num_programs(1)
    zeros = jnp.zeros((WA_DIM, 4 * LANES), BF16)
    qt = q_ref[...].astype(F32).T
    outs = []
    if has_window:
        vwin = [v_ref[...].astype(F32).T.astype(BF16) for v_ref in (vp_ref, vm_ref, vn_ref)]
        r = lax.broadcasted_iota(jnp.int32, (LANES, 4 * LANES), 0)
        c = jnp.bitwise_and(lax.broadcasted_iota(jnp.int32, (LANES, 4 * LANES), 1), LANES - 1)
        off_p = jnp.where(i > 0, 0, 2 * LANES)
        off_n = jnp.where(i < nblk - 1, 0, 2 * LANES)
        mask_p = r >= c + off_p
        mask_n = r + off_n <= c
    scores = []
    for h in range(WA_KV_HEADS):
        heads = [WA_DIM * (WA_GROUP * h + g) for g in range(WA_GROUP)]
        q = jnp.concatenate([qt[r0:r0 + WA_DIM, :] for r0 in heads], axis=1).astype(BF16)
        w = jnp.concatenate([q, zeros] if h == 0 else [zeros, q], axis=0)
        parts = [jnp.dot(kc_ref[...], w, preferred_element_type=F32)]
        if has_window:
            parts += [jnp.dot(k_ref[...], w, preferred_element_type=F32) for k_ref in (kp_ref, km_ref, kn_ref)]
        scores.append(parts)
    for h in range(WA_KV_HEADS):
        hs = slice(WA_DIM * h, WA_DIM * h + WA_DIM)
        sink = sink_ref[h]
        parts = scores[h]
        vts = [vct_ref[hs, :]]
        if has_window:
            parts = [parts[0], jnp.where(mask_p, parts[1], NEG_INF), parts[2],
                     jnp.where(mask_n, parts[3], NEG_INF)]
            vts += [v[hs, :] for v in vwin]
        m = sink
        for s in parts:
            m = jnp.maximum(m, jnp.max(s, axis=0, keepdims=True))
        den = jnp.exp(sink - m)
        o = jnp.zeros((WA_DIM, 4 * LANES), F32)
        for s, vt in zip(parts, vts):
            p = jnp.exp(s - m)
            den = den + jnp.sum(p, axis=0, keepdims=True)
            o = o + jnp.dot(vt, p.astype(BF16), preferred_element_type=F32)
        o = o / den
        outs += [o[:, LANES * g:LANES * (g + 1)] for g in range(WA_GROUP)]
    o_ref[...] = jnp.concatenate(outs, axis=0).T.astype(o_ref.dtype)


def _window_attention(q, kc, vc, sink, kl=None, vl=None):
    bsz, t, _ = q.shape
    nblk = t // LANES
    n_ctx = kc.shape[1]
    has_window = kl is not None
    vct = vc.transpose(0, 2, 1)
    sink_rows = jnp.repeat(sink.astype(F32).reshape(WA_KV_HEADS, 1, WA_GROUP), LANES, axis=-1)
    qspec = pl.BlockSpec((None, LANES, 512), lambda b, i: (b, i, 0))
    in_specs = [qspec,
                pl.BlockSpec((None, n_ctx, LANES), lambda b, i: (b, 0, 0)),
                pl.BlockSpec((None, LANES, n_ctx), lambda b, i: (b, 0, 0)),
                pl.BlockSpec((WA_KV_HEADS, 1, WA_GROUP * LANES), lambda b, i: (0, 0, 0))]
    args = [q, kc, vct, sink_rows]
    if has_window:
        prev = lambda b, i: (b, jnp.maximum(i - 1, 0), 0)
        here = lambda b, i: (b, i, 0)
        nxt = lambda b, i: (b, jnp.minimum(i + 1, nblk - 1), 0)
        in_specs += 2 * [pl.BlockSpec((None, LANES, LANES), f) for f in (prev, here, nxt)]
        args += [kl, kl, kl, vl, vl, vl]
    return pl.pallas_call(
        functools.partial(_wa_kernel, has_window=has_window),
        grid=(bsz, nblk),
        in_specs=in_specs,
        out_specs=qspec,
        out_shape=jax.ShapeDtypeStruct(q.shape, BF16),
        compiler_params=_params("parallel", "parallel"),
        name="window_attention" if has_window else "context_gqa",
    )(*args)


def _att_out_kernel(x_ref, oa_ref, ob_ref, w_ref, gate_ref, o_ref):
    o = jnp.dot(oa_ref[...], w_ref[0:512, :], preferred_element_type=F32)
    o = o + jnp.dot(ob_ref[...], w_ref[512:1024, :], preferred_element_type=F32)
    o_ref[...] = x_ref[...] + gate_ref[...] * o


def _att_out(x, oa, ob, w_out, gate):
    bsz, t, d = x.shape
    tm = min(ROW_TILE, t)
    row = lambda w: pl.BlockSpec((None, tm, w), lambda b, i: (b, i, 0))
    return pl.pallas_call(
        _att_out_kernel,
        grid=(bsz, t // tm),
        in_specs=[row(d), row(512), row(512), _resident(w_out.shape),
                  pl.BlockSpec((None, 1, d), lambda b, i: (b, 0, 0))],
        out_specs=row(d),
        out_shape=jax.ShapeDtypeStruct(x.shape, F32),
        compiler_params=_params("parallel", "parallel"),
        name="att_out",
    )(x, oa, ob, w_out, gate)


def _ffn_kernel(xm_ref, xp_ref, xn_ref, g_ref, sh_ref, sc_ref, gate_ref, wg_ref, wu_ref, cw_ref,
                wd_ref, og_ref, o_ref, gs_ref, *, final_norm):
    i = pl.program_id(1)
    first = i == 0
    last = i == pl.num_programs(1) - 1
    tm = xm_ref.shape[0]
    xm = xm_ref[...]
    g, sh, sc = g_ref[...], sh_ref[...], sc_ref[...]
    hm = _modulated(xm, g, sh, sc)
    hh = _modulated(jnp.concatenate([xp_ref[...], xn_ref[...]], axis=0), g, sh, sc)
    acc = jnp.zeros((tm, xm.shape[1]), F32)
    n_chunks = D_FF // FFN_CHUNK

    def up(c):
        cs = slice(c * FFN_CHUNK, (c + 1) * FFN_CHUNK)
        return (jnp.dot(hm, wg_ref[:, cs], preferred_element_type=F32),
                jnp.dot(hh, wg_ref[:, cs], preferred_element_type=F32),
                jnp.dot(hm, wu_ref[:, cs], preferred_element_type=F32))

    nxt = up(0)
    for c in range(n_chunks):
        cs = slice(c * FFN_CHUNK, (c + 1) * FFN_CHUNK)
        gm, gh, u = nxt
        if c + 1 < n_chunks:
            nxt = up(c + 1)
        gs_ref[0:SUBLANES, :] = jnp.where(first, 0.0, gh[0:SUBLANES, :])
        gs_ref[SUBLANES:SUBLANES + tm, :] = gm
        gs_ref[SUBLANES + tm:2 * SUBLANES + tm, :] = jnp.where(last, 0.0, gh[SUBLANES:, :])
        cw = cw_ref[:, cs]
        gc = (cw[0:1, :] * gs_ref[SUBLANES - 1:SUBLANES - 1 + tm, :] + cw[1:2, :] * gm
              + cw[2:3, :] * gs_ref[SUBLANES + 1:SUBLANES + 1 + tm, :])
        act = (gc * _sigmoid(gc) * u).astype(BF16)
        acc = acc + jnp.dot(act, wd_ref[cs, :], preferred_element_type=F32)
    y = xm + gate_ref[...] * acc
    if final_norm:
        y = y * lax.rsqrt(jnp.mean(y * y, axis=-1, keepdims=True) + NORM_EPS) * og_ref[...]
    o_ref[...] = y


def _ffn(x, g, sh, sc, gate, w_g, w_u, conv_w, w_d, out_g, final_norm):
    bsz, t, d = x.shape
    tm = min(ROW_TILE, t)
    nb = tm // SUBLANES
    last_blk = t // SUBLANES - 1
    vec = pl.BlockSpec((None, 1, d), lambda b, i: (b, 0, 0))
    return pl.pallas_call(
        functools.partial(_ffn_kernel, final_norm=final_norm),
        grid=(bsz, t // tm),
        in_specs=[pl.BlockSpec((None, tm, d), lambda b, i: (b, i, 0)),
                  pl.BlockSpec((None, SUBLANES, d), lambda b, i: (b, jnp.maximum(i * nb - 1, 0), 0)),
                  pl.BlockSpec((None, SUBLANES, d),
                               lambda b, i: (b, jnp.minimum((i + 1) * nb, last_blk), 0)),
                  pl.BlockSpec((1, d), lambda b, i: (0, 0)), vec, vec, vec,
                  _resident(w_g.shape), _resident(w_u.shape), _resident(conv_w.shape),
                  _resident(w_d.shape), pl.BlockSpec((1, d), lambda b, i: (0, 0))],
        out_specs=pl.BlockSpec((None, tm, d), lambda b, i: (b, i, 0)),
        out_shape=jax.ShapeDtypeStruct(x.shape, F32),
        scratch_shapes=[pltpu.VMEM((tm + 2 * SUBLANES, FFN_CHUNK), F32)],
        compiler_params=_params("parallel", "parallel"),
        name="conv_ffn",
    )(x, x, x, g, sh, sc, gate, w_g, w_u, conv_w, w_d, out_g)


def _rec_inproj_kernel(x_ref, g_ref, sh_ref, sc_ref, w_ref, gl_ref, rl_ref, ul_ref):
    h = _modulated(x_ref[...], g_ref[...], sh_ref[...], sc_ref[...])
    z = jnp.dot(h, w_ref[...], preferred_element_type=F32)
    gl_ref[...] = z[:, 0:LRU_WIDTH]
    rl_ref[...] = z[:, LRU_WIDTH:2 * LRU_WIDTH]
    ul_ref[...] = z[:, 2 * LRU_WIDTH:]


def _rec_inproj(x, g, sh, sc, w_in):
    bsz, t, d = x.shape
    tm = min(ROW_TILE, t)
    row = lambda w: pl.BlockSpec((None, tm, w), lambda b, i: (b, i, 0))
    vec = pl.BlockSpec((None, 1, d), lambda b, i: (b, 0, 0))
    outs = [LRU_WIDTH, LRU_WIDTH, S5_WIDTH]
    return pl.pallas_call(
        _rec_inproj_kernel,
        grid=(bsz, t // tm),
        in_specs=[row(d), pl.BlockSpec((1, d), lambda b, i: (0, 0)), vec, vec, _resident(w_in.shape)],
        out_specs=[row(w) for w in outs],
        out_shape=[jax.ShapeDtypeStruct((bsz, t, w), F32) for w in outs],
        compiler_params=_params("parallel", "parallel"),
        name="rec_inproj",
    )(x, g, sh, sc, w_in)


def _lru_direction(xm_ref, xp_ref, xn_ref, at_start, at_end, cw_ref, cb_ref, w_ref, bias_ref, nsp,
                   out_ref, carry_ref, xs_ref, a_ref, b_ref, reverse):
    tm = xm_ref.shape[0]
    xs_ref[0:SUBLANES, :] = jnp.where(at_start, 0.0, xp_ref[...])
    xs_ref[SUBLANES:SUBLANES + tm, :] = xm_ref[...]
    xs_ref[SUBLANES + tm:2 * SUBLANES + tm, :] = jnp.where(at_end, 0.0, xn_ref[...])
    cw = cw_ref[...]
    rc = cb_ref[...] + cw[2:3, :] * xm_ref[...]
    for k in (0, 1, 3):
        rc = rc + cw[k:k + 1, :] * xs_ref[SUBLANES - 2 + k:SUBLANES - 2 + k + tm, :]
    gz = jnp.dot(rc.astype(BF16), w_ref[...], preferred_element_type=F32) + bias_ref[...]
    r = _sigmoid(gz[:, 0:LRU_WIDTH])
    ig = _sigmoid(gz[:, LRU_WIDTH:])
    a = jnp.exp(nsp * r)
    b = jnp.sqrt(1.0 - a * a) * (ig * rc)
    ngroups = tm // SUBLANES
    a = a.reshape(ngroups, SUBLANES, LRU_WIDTH)
    b = b.reshape(ngroups, SUBLANES, LRU_WIDTH)
    row8 = lax.broadcasted_iota(jnp.int32, a.shape, 1)
    for s in (1, 2, 4):
        if reverse:
            keep = row8 <= SUBLANES - 1 - s
            shift = SUBLANES - s
        else:
            keep = row8 >= s
            shift = s
        a_sh = jnp.where(keep, pltpu.roll(a, shift, 1), 1.0)
        b_sh = jnp.where(keep, pltpu.roll(b, shift, 1), 0.0)
        b = b + a * b_sh
        a = a * a_sh
    a_ref[...] = a.reshape(tm, LRU_WIDTH)
    b_ref[...] = b.reshape(tm, LRU_WIDTH)
    edge = 0 if reverse else SUBLANES - 1

    def body(j, carry):
        grp = (ngroups - 1 - j) if reverse else j
        off = pl.multiple_of(grp * SUBLANES, SUBLANES)
        h = b_ref[pl.ds(off, SUBLANES), :] + a_ref[pl.ds(off, SUBLANES), :] * carry
        out_ref[pl.ds(off, SUBLANES), :] = h
        return h[edge:edge + 1, :]

    carry_ref[...] = lax.fori_loop(0, ngroups, body, carry_ref[...])


def _lru_kernel(fm_ref, fp_ref, fn_ref, bm_ref, bp_ref, bn_ref, cw_ref, cb_ref, wf_ref, wb_ref,
                biasf_ref, biasb_ref, nsp_ref, h0_ref, hf_ref, hb_ref, hlast_ref,
                carry_f, carry_b, xs_ref, a_ref, b_ref):
    i = pl.program_id(1)
    first = i == 0
    last = i == pl.num_programs(1) - 1

    @pl.when(first)
    def _():
        carry_f[...] = h0_ref[0]
        carry_b[...] = h0_ref[1]

    _lru_direction(fm_ref, fp_ref, fn_ref, first, last, cw_ref, cb_ref, wf_ref, biasf_ref, nsp_ref[0],
                   hf_ref, carry_f, xs_ref, a_ref, b_ref, reverse=False)
    _lru_direction(bm_ref, bp_ref, bn_ref, last, first, cw_ref, cb_ref, wb_ref, biasb_ref, nsp_ref[1],
                   hb_ref, carry_b, xs_ref, a_ref, b_ref, reverse=True)
    hlast_ref[0] = carry_f[...]
    hlast_ref[1] = carry_b[...]


def _lru(rl, conv_w, conv_b, w_f, w_b, bias_f, bias_b, nsp, h0):
    bsz, t, c = rl.shape
    tm = min(ROW_TILE, t)
    nt = t // tm
    nb = tm // SUBLANES
    last_blk = t // SUBLANES - 1
    fwd = lambda b, i: (b, i, 0)
    bwd = lambda b, i: (b, nt - 1 - i, 0)
    fprev = lambda b, i: (b, jnp.maximum(i * nb - 1, 0), 0)
    fnext = lambda b, i: (b, jnp.minimum((i + 1) * nb, last_blk), 0)
    bprev = lambda b, i: (b, jnp.maximum((nt - 1 - i) * nb - 1, 0), 0)
    bnext = lambda b, i: (b, jnp.minimum((nt - i) * nb, last_blk), 0)
    main = lambda f: pl.BlockSpec((None, tm, c), f)
    halo = lambda f: pl.BlockSpec((None, SUBLANES, c), f)
    state = pl.BlockSpec((None, 2, 1, c), lambda b, i: (b, 0, 0, 0))
    return pl.pallas_call(
        _lru_kernel,
        grid=(bsz, nt),
        in_specs=[main(fwd), halo(fprev), halo(fnext), main(bwd), halo(bprev), halo(bnext),
                  _resident(conv_w.shape), _resident(conv_b.shape), _resident(w_f.shape),
                  _resident(w_b.shape), _resident(bias_f.shape), _resident(bias_b.shape),
                  _resident(nsp.shape), state],
        out_specs=[main(fwd), main(bwd), state],
        out_shape=[jax.ShapeDtypeStruct(rl.shape, F32), jax.ShapeDtypeStruct(rl.shape, F32),
                   jax.ShapeDtypeStruct((bsz, 2, 1, c), F32)],
        scratch_shapes=[pltpu.VMEM((1, c), F32), pltpu.VMEM((1, c), F32),
                        pltpu.VMEM((tm + 2 * SUBLANES, c), F32),
                        pltpu.VMEM((tm, c), F32), pltpu.VMEM((tm, c), F32)],
        compiler_params=_params("parallel", "arbitrary"),
        name="rglru",
    )(rl, rl, rl, rl, rl, rl, conv_w, conv_b, w_f, w_b, bias_f, bias_b, nsp, h0)


def _s5_in_kernel(u_ref, t_ref, bs_ref, yi_ref, xre_ref, xim_ref):
    x = jnp.zeros((u_ref.shape[1], 4 * LANES), F32)
    for q in range(2):
        ub = u_ref[q].astype(BF16)
        yi_ref[q] = jnp.dot(ub, t_ref[q], preferred_element_type=F32)
        x = x + jnp.dot(ub, bs_ref[q], preferred_element_type=F32)
    for d in range(2):
        xre_ref[d] = x[:, 2 * d * LANES:(2 * d + 1) * LANES]
        xim_ref[d] = x[:, (2 * d + 1) * LANES:(2 * d + 2) * LANES]


def _s5_scan_kernel(xre_ref, xim_ref, h0re_ref, h0im_ref, shr_ref, shi_ref, cwr_ref, cwi_ref,
                    sre_ref, sim_ref, lre_ref, lim_ref, pre_ref, pim_ref, *, reverse):
    nc, n = xre_ref.shape
    xr = xre_ref[...]
    xi = xim_ref[...]
    row8 = jnp.bitwise_and(lax.broadcasted_iota(jnp.int32, (nc, n), 0), SUBLANES - 1)
    for k, s in enumerate((1, 2, 4)):
        keep = (row8 <= SUBLANES - 1 - s) if reverse else (row8 >= s)
        shift = nc - s if reverse else s
        xr_sh = jnp.where(keep, pltpu.roll(xr, shift, 0), 0.0)
        xi_sh = jnp.where(keep, pltpu.roll(xi, shift, 0), 0.0)
        mr = shr_ref[k]
        mi = shi_ref[k]
        xr, xi = xr + (mr * xr_sh - mi * xi_sh), xi + (mr * xi_sh + mi * xr_sh)
    pre_ref[...] = xr
    pim_ref[...] = xi
    cwr = cwr_ref[...]
    cwi = cwi_ref[...]
    row = lax.broadcasted_iota(jnp.int32, (SUBLANES, n), 0)
    ngroups = nc // SUBLANES
    edge = 0 if reverse else SUBLANES - 1
    entry = SUBLANES - 1 - edge
    step = SUBLANES - 1 if reverse else 1

    def body(j, carry):
        cr, ci = carry
        grp = (ngroups - 1 - j) if reverse else j
        off = pl.multiple_of(grp * SUBLANES, SUBLANES)
        sr = pre_ref[pl.ds(off, SUBLANES), :] + (cwr * cr - cwi * ci)
        si = pim_ref[pl.ds(off, SUBLANES), :] + (cwr * ci + cwi * cr)
        pre_ref[pl.ds(off, SUBLANES), :] = jnp.where(row == entry, cr, pltpu.roll(sr, step, 0))
        pim_ref[pl.ds(off, SUBLANES), :] = jnp.where(row == entry, ci, pltpu.roll(si, step, 0))
        return sr[edge:edge + 1, :], si[edge:edge + 1, :]

    cr, ci = lax.fori_loop(0, ngroups, body, (h0re_ref[...], h0im_ref[...]))
    lre_ref[...] = cr
    lim_ref[...] = ci
    sre_ref[...] = pre_ref[...].astype(BF16)
    sim_ref[...] = pim_ref[...].astype(BF16)


def _s5_out_kernel(yi_ref, u_ref, sre_ref, sim_ref, cs_ref, d_ref, y_ref):
    s = jnp.concatenate([sre_ref[0], sim_ref[0], sre_ref[1], sim_ref[1]], axis=-1)
    for q in range(2):
        y = yi_ref[q] + d_ref[q] * u_ref[q]
        y_ref[q] = y + jnp.dot(s, cs_ref[q], preferred_element_type=F32)


def _s5(u, prm, h0re, h0im):
    bsz, t, _ = u.shape
    nc = t // S5_CHUNK
    ug = u.reshape(bsz, nc, S5_CHUNK, S5_GROUPS, S5_GROUP).transpose(0, 3, 1, 2, 4)
    ug = ug.reshape(bsz, S5_GROUPS, nc, 256)
    npair = S5_GROUPS // 2
    pair = pl.BlockSpec((None, 2, nc, 256), lambda b, p: (b, p, 0, 0))
    wpair = lambda r, w: pl.BlockSpec((2, r, w), lambda b, p: (p, 0, 0))
    lanes = pl.BlockSpec((None, 2, nc, LANES), lambda b, p: (b, 0, 0, p))
    state_shape = (bsz, 2, nc, S5_LANES)
    yi, xre, xim = pl.pallas_call(
        _s5_in_kernel,
        grid=(bsz, npair),
        in_specs=[pair, wpair(256, 256), wpair(256, 4 * LANES)],
        out_specs=[pair, lanes, lanes],
        out_shape=[jax.ShapeDtypeStruct(ug.shape, F32), jax.ShapeDtypeStruct(state_shape, F32),
                   jax.ShapeDtypeStruct(state_shape, F32)],
        compiler_params=_params("parallel", "parallel"),
        name="s5_in",
    )(ug, prm["toep"], prm["bs"])

    states = []
    for d in range(2):
        full = pl.BlockSpec((None, None, nc, S5_LANES), lambda b, d=d: (b, d, 0, 0))
        one = pl.BlockSpec((None, None, 1, S5_LANES), lambda b, d=d: (b, d, 0, 0))
        tab = pl.BlockSpec((None, 3, 1, S5_LANES), lambda b, d=d: (d, 0, 0, 0))
        tab8 = pl.BlockSpec((None, SUBLANES, S5_LANES), lambda b, d=d: (d, 0, 0))
        plain = pl.BlockSpec((None, nc, S5_LANES), lambda b: (b, 0, 0))
        plain1 = pl.BlockSpec((None, 1, S5_LANES), lambda b: (b, 0, 0))
        states.append(pl.pallas_call(
            functools.partial(_s5_scan_kernel, reverse=bool(d)),
            grid=(bsz,),
            in_specs=[full, full, one, one, tab, tab, tab8, tab8],
            out_specs=[plain, plain, plain1, plain1],
            out_shape=[jax.ShapeDtypeStruct((bsz, nc, S5_LANES), BF16),
                       jax.ShapeDtypeStruct((bsz, nc, S5_LANES), BF16),
                       jax.ShapeDtypeStruct((bsz, 1, S5_LANES), F32),
                       jax.ShapeDtypeStruct((bsz, 1, S5_LANES), F32)],
            scratch_shapes=[pltpu.VMEM((nc, S5_LANES), F32), pltpu.VMEM((nc, S5_LANES), F32)],
            compiler_params=_params("parallel"),
            name="s5_scan_bwd" if d else "s5_scan_fwd",
        )(xre, xim, h0re, h0im, prm["sh_r"], prm["sh_i"], prm["cw_r"], prm["cw_i"]))
    sre, sim, lre, lim = [jnp.stack([states[0][k], states[1][k]], axis=1) for k in range(4)]

    y = pl.pallas_call(
        _s5_out_kernel,
        grid=(bsz, npair),
        in_specs=[pair, pair, lanes, lanes, wpair(4 * LANES, 256), wpair(1, 256)],
        out_specs=pair,
        out_shape=jax.ShapeDtypeStruct(ug.shape, F32),
        compiler_params=_params("parallel", "parallel"),
        name="s5_out",
    )(yi, ug, sre, sim, prm["cs"], prm["dskip"])
    y = y.reshape(bsz, S5_GROUPS, nc, S5_CHUNK, S5_GROUP).transpose(0, 2, 3, 1, 4)
    return y.reshape(bsz, t, S5_WIDTH), lre, lim


def _s5_prepare(a_re, a_im, log_step, b_re, b_im, c_re, c_im, d_skip):
    a_re, a_im = a_re.astype(F32), a_im.astype(F32)
    step = jnp.exp(log_step.astype(F32))[..., None]
    mag = jnp.exp(a_re * step)
    abr, abi = mag * jnp.cos(a_im * step), mag * jnp.sin(a_im * step)
    den = a_re * a_re + a_im * a_im
    qr = ((abr - 1.0) * a_re + abi * a_im) / den
    qi = (abi * a_re - (abr - 1.0) * a_im) / den
    br_, bi_ = b_re.astype(F32), b_im.astype(F32)
    bbr = qr[..., None] * br_ - qi[..., None] * bi_
    bbi = qr[..., None] * bi_ + qi[..., None] * br_
    cr, ci = c_re.astype(F32), c_im.astype(F32)

    def power(n):
        n = jnp.asarray(n, F32)[..., None, None, None]
        m = jnp.exp(n * a_re * step)
        return m * jnp.cos(n * a_im * step), m * jnp.sin(n * a_im * step)

    lc = S5_CHUNK
    pr, pi = power(jnp.arange(lc + 1))
    mr = pr[..., None] * bbr - pi[..., None] * bbi
    mi = pr[..., None] * bbi + pi[..., None] * bbr
    hp = lax.Precision.HIGHEST
    kern = (jnp.einsum('dgop,jdgpi->jdgoi', cr, mr[:lc], precision=hp)
            - jnp.einsum('dgop,jdgpi->jdgoi', ci, mi[:lc], precision=hp))
    s_idx = jnp.arange(lc)[:, None]
    r_idx = jnp.arange(lc)[None, :]
    lag = jnp.clip(r_idx - s_idx, 0, lc - 1)
    toep = jnp.where((r_idx >= s_idx)[:, :, None, None, None, None], kern[lag], 0.0)
    toep = toep.transpose(2, 3, 0, 5, 1, 4).reshape(2, S5_GROUPS, lc * S5_GROUP, lc * S5_GROUP)
    rev = jnp.arange(lc - 1, -1, -1)
    bs_re = mr[rev].transpose(1, 2, 0, 4, 3).reshape(2, S5_GROUPS, lc * S5_GROUP, S5_STATE)
    bs_im = mi[rev].transpose(1, 2, 0, 4, 3).reshape(2, S5_GROUPS, lc * S5_GROUP, S5_STATE)
    pr1, pi1 = pr[1:], pi[1:]
    cs_re = cr[None] * pr1[:, :, :, None, :] - ci[None] * pi1[:, :, :, None, :]
    cs_im = -(cr[None] * pi1[:, :, :, None, :] + ci[None] * pr1[:, :, :, None, :])
    cs_re = cs_re.transpose(1, 2, 4, 0, 3).reshape(2, S5_GROUPS, S5_STATE, lc * S5_GROUP)
    cs_im = cs_im.transpose(1, 2, 4, 0, 3).reshape(2, S5_GROUPS, S5_STATE, lc * S5_GROUP)
    half = (jnp.arange(S5_GROUPS) % 2)[None, :, None, None]

    def pad_cols(m):
        z = jnp.zeros_like(m)
        return jnp.where(half == 0, jnp.concatenate([m, z], -1), jnp.concatenate([z, m], -1))

    def pad_rows(m):
        z = jnp.zeros_like(m)
        return jnp.where(half == 0, jnp.concatenate([m, z], -2), jnp.concatenate([z, m], -2))

    def mirror_rows(m):
        shp = m.shape
        return m.reshape(shp[0], lc, shp[1] // lc, shp[2])[:, ::-1].reshape(shp)

    def mirror_cols(m):
        shp = m.shape
        return m.reshape(shp[0], shp[1], lc, shp[2] // lc)[:, :, ::-1].reshape(shp)

    flat = lambda x: x.reshape(x.shape[:-2] + (S5_LANES,))
    shr, shi = power(jnp.asarray([lc, 2 * lc, 4 * lc]))
    cwr, cwi = power(lc * jnp.arange(1, SUBLANES + 1))
    cwr, cwi = flat(cwr).transpose(1, 0, 2), flat(cwi).transpose(1, 0, 2)
    bs_re, bs_im = pad_cols(bs_re), pad_cols(bs_im)
    cs_re, cs_im = pad_rows(cs_re), pad_rows(cs_im)
    return {
        "toep": (toep[0] + mirror_cols(mirror_rows(toep[1]))).astype(BF16),
        "bs": jnp.concatenate([bs_re[0], bs_im[0], mirror_rows(bs_re[1]), mirror_rows(bs_im[1])],
                              axis=-1).astype(BF16),
        "cs": jnp.concatenate([cs_re[0], cs_im[0], mirror_cols(cs_re[1]), mirror_cols(cs_im[1])],
                              axis=-2).astype(BF16),
        "sh_r": flat(shr).transpose(1, 0, 2)[:, :, None, :], "sh_i": flat(shi).transpose(1, 0, 2)[:, :, None, :],
        "cw_r": jnp.stack([cwr[0], cwr[1, ::-1]]), "cw_i": jnp.stack([cwi[0], cwi[1, ::-1]]),
        "dskip": jnp.tile(d_skip.astype(F32).reshape(S5_GROUPS, 1, S5_GROUP), (1, 1, lc)),
    }


def _rec_out_kernel(x_ref, gl_ref, hf_ref, hb_ref, y_ref, wglu_ref, w_ref, gate_ref, o_ref):
    t1 = ((hf_ref[...] + hb_ref[...]) * _gelu(gl_ref[...])).astype(BF16)
    yg = _gelu(y_ref[...])
    t2 = (yg * _sigmoid(jnp.dot(yg.astype(BF16), wglu_ref[...], preferred_element_type=F32))).astype(BF16)
    o = jnp.dot(t1, w_ref[0:LRU_WIDTH, :], preferred_element_type=F32)
    o = o + jnp.dot(t2, w_ref[LRU_WIDTH:, :], preferred_element_type=F32)
    o_ref[...] = x_ref[...] + gate_ref[...] * o


def _rec_out(x, gl, hf, hb, y, w_glu, w_out, gate):
    bsz, t, d = x.shape
    tm = min(ROW_TILE, t)
    row = lambda w: pl.BlockSpec((None, tm, w), lambda b, i: (b, i, 0))
    return pl.pallas_call(
        _rec_out_kernel,
        grid=(bsz, t // tm),
        in_specs=[row(d), row(LRU_WIDTH), row(LRU_WIDTH), row(LRU_WIDTH), row(S5_WIDTH),
                  _resident(w_glu.shape), _resident(w_out.shape),
                  pl.BlockSpec((None, 1, d), lambda b, i: (b, 0, 0))],
        out_specs=row(d),
        out_shape=jax.ShapeDtypeStruct(x.shape, F32),
        compiler_params=_params("parallel", "parallel"),
        name="rec_out",
    )(x, gl, hf, hb, y, w_glu, w_out, gate)


def _rope_tables(n, dim):
    rows = n // GRID_W
    row = jnp.repeat(jnp.arange(rows), GRID_W).astype(F32)
    col = jnp.tile(jnp.arange(GRID_W), rows).astype(F32)
    half = dim // 2
    freqs = ROPE_THETA ** (-jnp.arange(0, half, 2, dtype=F32) / half)

    def angles(pos):
        a = pos[:, None] * freqs[None, :]
        return jnp.concatenate([a, a], axis=-1)

    ang = jnp.concatenate([angles(row), angles(col)], axis=-1)
    reps = LANES // dim
    return jnp.tile(jnp.cos(ang), (1, reps)), jnp.tile(jnp.sin(ang), (1, reps))


def _rotated_columns(w, dim):
    quarter = dim // 4
    idx = jnp.arange(w.shape[1])
    low = (idx % (2 * quarter)) < quarter
    src = jnp.where(low, idx + quarter, idx - quarter)
    return jnp.where(low[None, :], -1.0, 1.0) * w[:, src]


def _att_weights(w_in):
    aq, ak = w_in[:, 0:512], w_in[:, 512:1024]
    bq, bk = w_in[:, 1536:2048], w_in[:, 2048:2176]
    ext = [w_in, _rotated_columns(aq, DA_DIM), _rotated_columns(ak, DA_DIM),
           _rotated_columns(bq, WA_DIM), _rotated_columns(bk, WA_DIM)]
    return jnp.concatenate(ext, axis=1).astype(BF16)


def _block_diag(w):
    eye = jnp.eye(LRU_BLOCKS, dtype=w.dtype)
    return jnp.einsum('nij,nm->nimj', w, eye).reshape(LRU_WIDTH, LRU_WIDTH)


def _vec3(v, bsz):
    return v.reshape(-1, 1, v.shape[-1]) if v.shape[0] == bsz else jnp.broadcast_to(v[None], (bsz, 1, v.shape[-1]))


def kernel(x, c, ctx, c_ctx, ada_w, ada_b, norm1_g, norm2_g, att_w_in, att_w_out, da_lam_q1, da_lam_k1, da_lam_q2, da_lam_k2, da_subln_g, wa_sink, rec_w_in, rec_w_out, lru_conv_w, lru_conv_b, lru_w_a, lru_b_a, lru_w_x, lru_b_x, lru_lam, s5_a_re, s5_a_im, s5_log_step, s5_b_re, s5_b_im, s5_c_re, s5_c_im, s5_d, s5_w_glu, ffn_w_g, ffn_w_u, ffn_conv_w, ffn_w_down, final_g):
    bsz, n_lat, d = x.shape
    n_ctx = ctx.shape[1]
    assert d == D_MODEL and bsz + 1 <= SUBLANES
    assert n_lat % ROW_TILE == 0 and n_ctx % DA_TK == 0 and n_ctx <= ROW_TILE
    x = x.astype(F32)
    ctx = ctx.astype(F32)

    cond = jnp.zeros((SUBLANES, d), F32).at[:bsz].set(c.astype(F32)).at[bsz].set(c_ctx.astype(F32))
    mods = _ada_all(cond, ada_w.astype(F32), ada_b.astype(F32))

    ca, sa = _rope_tables(n_lat, DA_DIM)
    cb, sb = _rope_tables(n_lat, WA_DIM)
    fg = final_g.reshape(1, d).astype(F32)
    ones = jnp.ones((n_ctx, LANES), F32)
    zeros = jnp.zeros((n_ctx, LANES), F32)

    for l in range(DEPTH):
        ctx_out = l < DEPTH - 1
        m = mods[l]
        sh1, sc1, g1, sh2, sc2, g2 = [_vec3(v, bsz) for v in jnp.split(m[:bsz], 6, axis=-1)]
        csh1, csc1, cg1, csh2, csc2, cg2 = [_vec3(v, bsz) for v in jnp.split(m[bsz:bsz + 1], 6, axis=-1)]
        n1 = norm1_g[l].reshape(1, d).astype(F32)
        n2 = norm2_g[l].reshape(1, d).astype(F32)
        j = l // 2
        if l % 2 == 0:
            lam_init = 0.8 - 0.6 * math.exp(-0.3 * l)
            w_ext = _att_weights(att_w_in[j].astype(F32))
            w_out = att_w_out[j].astype(BF16)
            lam = (jnp.exp(jnp.sum(da_lam_q1[j].astype(F32) * da_lam_k1[j].astype(F32)))
                   - jnp.exp(jnp.sum(da_lam_q2[j].astype(F32) * da_lam_k2[j].astype(F32))) + lam_init)
            aql, akl, avl, bql, bkl, bvl = _att_inproj(x, n1, sh1, sc1, w_ext, ca, sa, cb, sb)
            aqc, akc, avc, bqc, bkc, bvc = _att_inproj(ctx, n1, csh1, csc1, w_ext, ones, zeros, ones, zeros)
            k_all = jnp.concatenate([akc, akl], axis=1)
            v_all = jnp.concatenate([avc, avl], axis=-1)
            oal = _diff_attention(aql, k_all, v_all, lam, da_subln_g[j], 1.0 - lam_init)
            obl = _window_attention(bql, bkc, bvc, wa_sink[j], bkl, bvl)
            x = _att_out(x, oal, obl, w_out, g1)
            if ctx_out:
                oac = _diff_attention(aqc, akc, avc, lam, da_subln_g[j], 1.0 - lam_init)
                obc = _window_attention(bqc, bkc, bvc, wa_sink[j])
                ctx = _att_out(ctx, oac, obc, w_out, cg1)
        else:
            w_in = rec_w_in[j].astype(BF16)
            w_f = jnp.concatenate([_block_diag(lru_w_a[j, 0]), _block_diag(lru_w_x[j, 0])], axis=1).astype(BF16)
            w_b = jnp.concatenate([_block_diag(lru_w_a[j, 1]), _block_diag(lru_w_x[j, 1])], axis=1).astype(BF16)
            bias_f = jnp.concatenate([lru_b_a[j, 0], lru_b_x[j, 0]]).reshape(1, -1).astype(F32)
            bias_b = jnp.concatenate([lru_b_a[j, 1], lru_b_x[j, 1]]).reshape(1, -1).astype(F32)
            nsp = (-LRU_C * jax.nn.softplus(-lru_lam[j].astype(F32))).reshape(2, 1, LRU_WIDTH)
            conv_w = lru_conv_w[j].astype(F32)
            conv_b = lru_conv_b[j].reshape(1, -1).astype(F32)
            s5p = _s5_prepare(s5_a_re[j], s5_a_im[j], s5_log_step[j], s5_b_re[j], s5_b_im[j],
                              s5_c_re[j], s5_c_im[j], s5_d[j])
            gc_, rc_, uc_ = _rec_inproj(ctx, n1, csh1, csc1, w_in)
            gl_, rl_, ul_ = _rec_inproj(x, n1, sh1, sc1, w_in)
            h0 = jnp.zeros((bsz, 2, 1, LRU_WIDTH), F32)
            hfc, hbc, hlast = _lru(rc_, conv_w, conv_b, w_f, w_b, bias_f, bias_b, nsp, h0)
            hfl, hbl, _ = _lru(rl_, conv_w, conv_b, w_f, w_b, bias_f, bias_b, nsp, hlast)
            s0 = jnp.zeros((bsz, 2, 1, S5_LANES), F32)
            yc, lre, lim = _s5(uc_, s5p, s0, s0)
            yl, _, _ = _s5(ul_, s5p, lre, lim)
            w_glu = s5_w_glu[j].astype(BF16)
            w_out = rec_w_out[j].astype(BF16)
            x = _rec_out(x, gl_, hfl, hbl, yl, w_glu, w_out, g1)
            if ctx_out:
                ctx = _rec_out(ctx, gc_, hfc, hbc, yc, w_glu, w_out, cg1)
        wg, wu, wd = ffn_w_g[l].astype(BF16), ffn_w_u[l].astype(BF16), ffn_w_down[l].astype(BF16)
        cw = ffn_conv_w[l].astype(F32)
        x = _ffn(x, n2, sh2, sc2, g2, wg, wu, cw, wd, fg, final_norm=l == DEPTH - 1)
        if ctx_out:
            ctx = _ffn(ctx, n2, csh2, csc2, cg2, wg, wu, cw, wd, fg, final_norm=False)
    return x
```

```python
import functools
import math

import jax
import jax.numpy as jnp
from jax import lax
from jax.experimental import pallas as pl
from jax.experimental.pallas import tpu as pltpu

F32 = jnp.float32
BF16 = jnp.bfloat16

D_MODEL = 1024
DEPTH = 4
GRID_W = 64
ROPE_THETA = 10000.0
NORM_EPS = 1e-6
NEG_INF = -1e30
LOG2_E = math.log2(math.e)

DA_HEADS = 8
DA_DIM = 32
WA_HEADS = 8
WA_KV_HEADS = 2
WA_DIM = 64
WA_WINDOW = 128
WA_GROUP = WA_HEADS // WA_KV_HEADS
LRU_WIDTH = 768
LRU_BLOCKS = 8
LRU_BLOCK_DIM = LRU_WIDTH // LRU_BLOCKS
LRU_CONV = 4
LRU_C = 8.0
S5_WIDTH = 256
S5_GROUP = 16
S5_GROUPS = S5_WIDTH // S5_GROUP
S5_STATE = 64
S5_CHUNK = 16
S5_LANES = S5_GROUPS * S5_STATE
D_FF = 2816
FFN_CHUNK = 1408

DA_QK = DA_HEADS * 2 * DA_DIM
DA_V = DA_HEADS * 2 * DA_DIM
WA_Q = WA_HEADS * WA_DIM
WA_KV = WA_KV_HEADS * WA_DIM
ATT_IN = 2 * DA_QK + DA_V + WA_Q + 2 * WA_KV

SUBLANES = 8
LANES = 128
VMEM_LIMIT_BYTES = 56 * 1024 * 1024
ROW_TILE = 512
DA_TQ = 512
DA_TK = 256
DA_AHEAD = 2
DA_BOUND_LIMIT = 48.0
DA_UNROLL = 13
DA_VROWS = 80


def _params(*sem):
    return pltpu.CompilerParams(dimension_semantics=sem, vmem_limit_bytes=VMEM_LIMIT_BYTES)


def _resident(shape):
    nd = len(shape)
    return pl.BlockSpec(shape, lambda *_: (0,) * nd, pipeline_mode=pl.Buffered(1))


def _sigmoid(x):
    return 1.0 / (1.0 + jnp.exp(-x))


def _gelu(x):
    return 0.5 * x * (1.0 + jnp.tanh(math.sqrt(2.0 / math.pi) * (x + 0.044715 * (x * x * x))))


def _modulated(x, g, shift, scale):
    y = x * lax.rsqrt(jnp.mean(x * x, axis=-1, keepdims=True) + NORM_EPS)
    return (y * g * (1.0 + scale) + shift).astype(BF16)


def _ada_kernel(c_ref, w_ref, b_ref, o_ref):
    c = c_ref[...]
    s = c * _sigmoid(c)
    o_ref[...] = jnp.dot(s, w_ref[...], precision=lax.Precision.HIGHEST,
                         preferred_element_type=F32) + b_ref[...]


def _ada_all(cond, ada_w, ada_b):
    depth, d, n = ada_w.shape
    tn = 1536
    return pl.pallas_call(
        _ada_kernel,
        grid=(depth, n // tn),
        in_specs=[pl.BlockSpec((SUBLANES, d), lambda l, j: (0, 0)),
                  pl.BlockSpec((None, d, tn), lambda l, j: (l, 0, j)),
                  pl.BlockSpec((None, 1, tn), lambda l, j: (l, 0, j))],
        out_specs=pl.BlockSpec((None, SUBLANES, tn), lambda l, j: (l, 0, j)),
        out_shape=jax.ShapeDtypeStruct((depth, SUBLANES, n), F32),
        compiler_params=_params("parallel", "parallel"),
        name="ada_mod",
    )(cond, ada_w, ada_b.reshape(depth, 1, n))


def _att_inproj_kernel(x_ref, g_ref, sh_ref, sc_ref, w_ref, ca_ref, sa_ref, cb_ref, sb_ref,
                       aq_ref, ak_ref, avt_ref, bq_ref, bk_ref, bv_ref):
    h = _modulated(x_ref[...], g_ref[...], sh_ref[...], sc_ref[...])
    z = jnp.dot(h, w_ref[...], preferred_element_type=F32)
    ca = jnp.tile(ca_ref[...], (1, 4))
    sa = jnp.tile(sa_ref[...], (1, 4))
    cb = cb_ref[...]
    sb = sb_ref[...]
    cb4 = jnp.tile(cb, (1, 4))
    sb4 = jnp.tile(sb, (1, 4))
    o = ATT_IN
    aq = z[:, 0:512] * ca + z[:, o:o + 512] * sa
    ak = z[:, 512:1024] * ca + z[:, o + 512:o + 1024] * sa
    bq = z[:, 1536:2048] * cb4 + z[:, o + 1024:o + 1536] * sb4
    bk = z[:, 2048:2176] * cb + z[:, o + 1536:o + 1664] * sb
    aq_ref[...] = (aq * (DA_DIM ** -0.5 * LOG2_E)).astype(BF16)
    ak_ref[...] = ak.astype(BF16)
    for hp in range(4):
        avt_ref[hp] = z[:, 1024 + LANES * hp:1024 + LANES * (hp + 1)].T.astype(BF16)
    bq_ref[...] = (bq * (WA_DIM ** -0.5)).astype(BF16)
    bk_ref[...] = bk.astype(BF16)
    bv_ref[...] = z[:, 2176:2304].astype(BF16)


def _att_inproj(x, g, sh, sc, w_ext, ca, sa, cb, sb):
    bsz, t, d = x.shape
    tm = min(ROW_TILE, t)
    n = w_ext.shape[1]
    row = lambda w: pl.BlockSpec((None, tm, w), lambda b, i: (b, i, 0))
    vec = pl.BlockSpec((None, 1, d), lambda b, i: (b, 0, 0))
    tab = pl.BlockSpec((tm, LANES), lambda b, i: (i, 0))
    rows = lambda w: (row(w), jax.ShapeDtypeStruct((bsz, t, w), BF16))
    outs = [rows(512), rows(512),
            (pl.BlockSpec((None, 4, LANES, tm), lambda b, i: (b, 0, 0, i)),
             jax.ShapeDtypeStruct((bsz, 4, LANES, t), BF16)),
            rows(512), rows(128), rows(128)]
    return pl.pallas_call(
        _att_inproj_kernel,
        grid=(bsz, t // tm),
        in_specs=[row(d), pl.BlockSpec((1, d), lambda b, i: (0, 0)), vec, vec, _resident((d, n)),
                  tab, tab, tab, tab],
        out_specs=[o[0] for o in outs],
        out_shape=[o[1] for o in outs],
        compiler_params=_params("parallel", "parallel"),
        name="att_inproj",
    )(x, g, sh, sc, w_ext, ca, sa, cb, sb)


def _da_kernel(lam_ref, q_ref, k_ref, vt_ref, g_ref, o_ref, acc_ref, s_ref, *, tk, nkt, out_scale):
    tq = q_ref.shape[0]
    qf = q_ref[...].astype(F32).T
    row = lax.broadcasted_iota(jnp.int32, qf.shape, 0)
    ones = jnp.where(lax.broadcasted_iota(jnp.int32, (DA_VROWS - 64, tk), 0) == 0, 1.0, 0.0).astype(BF16)
    w = [jnp.where(jnp.right_shift(row, 5) == s, qf, 0.0).astype(BF16) for s in range(4)]
    acc_ref[...] = jnp.zeros_like(acc_ref)

    def scores(kt, s):
        off = pl.multiple_of(kt * tk, tk)
        st = jnp.dot(k_ref[pl.ds(off, tk), :], w[s], preferred_element_type=F32)
        s_ref[s] = st
        return jnp.max(st, axis=0, keepdims=True)

    def absorb(s, vt, m, mt):
        m_new = jnp.maximum(m, mt)
        p = jnp.exp2(s_ref[s] - m_new).astype(BF16)
        pv = jnp.dot(vt[s // 2], p, preferred_element_type=F32)
        acc_ref[s] = jnp.exp2(m - m_new) * acc_ref[s] + pv
        return m_new

    def body(kt, carry):
        ms, mt0, mt1 = carry
        v2 = vt_ref[:, pl.ds(pl.multiple_of(kt * tk, tk), tk)]
        vt = [jnp.concatenate([v2[64 * h:64 * h + 64, :], ones], axis=0) for h in range(2)]
        nxt = jnp.minimum(kt + 1, nkt - 1)
        mt2 = scores(kt, 2)
        mt3 = scores(kt, 3)
        m0 = absorb(0, vt, ms[0], mt0)
        mt0 = scores(nxt, 0)
        m1 = absorb(1, vt, ms[1], mt1)
        mt1 = scores(nxt, 1)
        m2 = absorb(2, vt, ms[2], mt2)
        m3 = absorb(3, vt, ms[3], mt3)
        return (m0, m1, m2, m3), mt0, mt1

    m_init = tuple(jnp.full((1, tq), NEG_INF, F32) for _ in range(4))
    lax.fori_loop(0, nkt, body, (m_init, scores(0, 0), scores(0, 1)),
                  unroll=DA_UNROLL if nkt % DA_UNROLL == 0 else 1)
    lam = lam_ref[0]
    ys = []
    for h in range(2):
        a0 = acc_ref[2 * h]
        a1 = acc_ref[2 * h + 1]
        o = a0[0:64, :] / a0[64:65, :] - lam * (a1[0:64, :] / a1[64:65, :])
        y = o * lax.rsqrt(jnp.mean(o * o, axis=0, keepdims=True) + NORM_EPS)
        ys.append(y * g_ref[64 * h:64 * h + 64, :] * out_scale)
    o_ref[...] = jnp.concatenate(ys, axis=0).T.astype(o_ref.dtype)


def _da_bounded_kernel(lam_ref, kn_ref, q_ref, k_ref, vt_ref, g_ref, o_ref, acc_ref, l_ref, *, tk, nkt,
                       out_scale):
    b, hp = pl.program_id(0), pl.program_id(1)
    tq = q_ref.shape[0]
    qf = q_ref[...].astype(F32).T
    row = lax.broadcasted_iota(jnp.int32, qf.shape, 0)
    one_col = jnp.where(lax.broadcasted_iota(jnp.int32, (tk, LANES), 1) == 0, 1.0, 0.0).astype(BF16)
    w = []
    for s in range(4):
        qs = jnp.where(jnp.right_shift(row, 5) == s, qf, 0.0)
        bound = jnp.sqrt(jnp.sum(qs * qs, axis=0, keepdims=True)) * kn_ref[b, 4 * hp + s]
        w.append(jnp.concatenate([qs, jnp.where(row == 0, -bound, 0.0)], axis=0).astype(BF16))
    acc_ref[...] = jnp.zeros_like(acc_ref)
    l_ref[...] = jnp.zeros_like(l_ref)
    unroll = DA_UNROLL if nkt % DA_UNROLL == 0 else 1
    steps = [(u, s) for u in range(unroll) for s in range(4)]

    def trip(j, carry):
        def keys(u):
            off = pl.multiple_of((j * unroll + u) * tk, tk)
            return jnp.concatenate([k_ref[pl.ds(off, tk), :], one_col], axis=1)

        def values(u, h):
            off = pl.multiple_of((j * unroll + u) * tk, tk)
            return vt_ref[64 * h:64 * h + 64, pl.ds(off, tk)]

        def score(n):
            u, s = steps[n]
            return jnp.dot(keys(u), w[s], preferred_element_type=F32)

        queue = [score(n) for n in range(min(DA_AHEAD, len(steps)))]
        for n, (u, s) in enumerate(steps):
            if n + DA_AHEAD < len(steps):
                queue.append(score(n + DA_AHEAD))
            p = jnp.exp2(queue.pop(0))
            l_ref[s] = l_ref[s] + jnp.sum(p.reshape(tk // SUBLANES, SUBLANES, tq), axis=0)
            acc_ref[s] = acc_ref[s] + jnp.dot(values(u, s // 2), p.astype(BF16), preferred_element_type=F32)
        return carry

    lax.fori_loop(0, nkt // unroll, trip, 0)
    lam = lam_ref[0]
    ys = []
    for h in range(2):
        l0 = jnp.sum(l_ref[2 * h], axis=0, keepdims=True)
        l1 = jnp.sum(l_ref[2 * h + 1], axis=0, keepdims=True)
        o = acc_ref[2 * h] / l0 - lam * (acc_ref[2 * h + 1] / l1)
        y = o * lax.rsqrt(jnp.mean(o * o, axis=0, keepdims=True) + NORM_EPS)
        ys.append(y * g_ref[64 * h:64 * h + 64, :] * out_scale)
    o_ref[...] = jnp.concatenate(ys, axis=0).T.astype(o_ref.dtype)


def _diff_attention(q, k_all, vt_all, lam, subln_g, out_scale):
    bsz, tq_all, _ = q.shape
    tk_all = k_all.shape[1]
    tq = min(DA_TQ, tq_all)
    tk = DA_TK
    nkt = tk_all // tk
    gcol = jnp.tile(subln_g.astype(F32), 2).reshape(LANES, 1)
    lam1 = lam.reshape(1).astype(F32)
    smem = pl.BlockSpec(memory_space=pltpu.SMEM)
    specs = [pl.BlockSpec((None, tq, LANES), lambda b, hp, i: (b, i, hp)),
             pl.BlockSpec((None, tk_all, LANES), lambda b, hp, i: (b, 0, hp)),
             pl.BlockSpec((None, None, LANES, tk_all), lambda b, hp, i: (b, hp, 0, 0)),
             pl.BlockSpec((LANES, 1), lambda b, hp, i: (0, 0))]
    common = dict(grid=(bsz, 4, tq_all // tq),
                  out_specs=pl.BlockSpec((None, tq, LANES), lambda b, hp, i: (b, i, hp)),
                  out_shape=jax.ShapeDtypeStruct((bsz, tq_all, 512), BF16),
                  compiler_params=_params("parallel", "parallel", "arbitrary"))

    def running_max(_):
        return pl.pallas_call(
            functools.partial(_da_kernel, tk=tk, nkt=nkt, out_scale=out_scale),
            in_specs=[smem] + specs,
            scratch_shapes=[pltpu.VMEM((4, DA_VROWS, tq), F32), pltpu.VMEM((4, tk, tq), F32)],
            name="diff_attention", **common)(lam1, q, k_all, vt_all, gcol)

    def bounded(kn):
        return pl.pallas_call(
            functools.partial(_da_bounded_kernel, tk=tk, nkt=nkt, out_scale=out_scale),
            in_specs=[smem, smem] + specs,
            scratch_shapes=[pltpu.VMEM((4, 64, tq), F32), pltpu.VMEM((4, SUBLANES, tq), F32)],
            name="diff_attention_bounded", **common)(lam1, kn, q, k_all, vt_all, gcol)

    norms = lambda z: jnp.sqrt(jnp.max(jnp.sum(jnp.square(z.astype(F32).reshape(bsz, -1, 16, DA_DIM)), -1), 1))
    kn = norms(k_all)
    safe = jnp.max(norms(q) * kn) < DA_BOUND_LIMIT
    return lax.cond(safe, bounded, running_max, kn)


def _wa_kernel(*refs, has_window):
    if has_window:
        (q_ref, kc_ref, vct_ref, sink_ref, kp_ref, km_ref, kn_ref, vp_ref, vm_ref, vn_ref, o_ref) = refs
    else:
        (q_ref, kc_ref, vct_ref, sink_ref, o_ref) = refs
    i = pl.program_id(1)
    nblk = pl.num_programs(1)
    zeros = jnp.zeros((WA_DIM, 4 * LANES), BF16)
    qt = q_ref[...].astype(F32).T
    outs = []
    if has_window:
        vwin = [v_ref[...].astype(F32).T.astype(BF16) for v_ref in (vp_ref, vm_ref, vn_ref)]
        r = lax.broadcasted_iota(jnp.int32, (LANES, 4 * LANES), 0)
        c = jnp.bitwise_and(lax.broadcasted_iota(jnp.int32, (LANES, 4 * LANES), 1), LANES - 1)
        off_p = jnp.where(i > 0, 0, 2 * LANES)
        off_n = jnp.where(i < nblk - 1, 0, 2 * LANES)
        mask_p = r >= c + off_p
        mask_n = r + off_n <= c
    scores = []
    for h in range(WA_KV_HEADS):
        heads = [WA_DIM * (WA_GROUP * h + g) for g in range(WA_GROUP)]
        q = jnp.concatenate([qt[r0:r0 + WA_DIM, :] for r0 in heads], axis=1).astype(BF16)
        w = jnp.concatenate([q, zeros] if h == 0 else [zeros, q], axis=0)
        parts = [jnp.dot(kc_ref[...], w, preferred_element_type=F32)]
        if has_window:
            parts += [jnp.dot(k_ref[...], w, preferred_element_type=F32) for k_ref in (kp_ref, km_ref, kn_ref)]
        scores.append(parts)
    for h in range(WA_KV_HEADS):
        hs = slice(WA_DIM * h, WA_DIM * h + WA_DIM)
        sink = sink_ref[h]
        parts = scores[h]
        vts = [vct_ref[hs, :]]
        if has_window:
            parts = [parts[0], jnp.where(mask_p, parts[1], NEG_INF), parts[2],
                     jnp.where(mask_n, parts[3], NEG_INF)]
            vts += [v[hs, :] for v in vwin]
        m = sink
        for s in parts:
            m = jnp.maximum(m, jnp.max(s, axis=0, keepdims=True))
        den = jnp.exp(sink - m)
        o = jnp.zeros((WA_DIM, 4 * LANES), F32)
        for s, vt in zip(parts, vts):
            p = jnp.exp(s - m)
            den = den + jnp.sum(p, axis=0, keepdims=True)
            o = o + jnp.dot(vt, p.astype(BF16), preferred_element_type=F32)
        o = o / den
        outs += [o[:, LANES * g:LANES * (g + 1)] for g in range(WA_GROUP)]
    o_ref[...] = jnp.concatenate(outs, axis=0).T.astype(o_ref.dtype)


def _window_attention(q, kc, vc, sink, kl=None, vl=None):
    bsz, t, _ = q.shape
    nblk = t // LANES
    n_ctx = kc.shape[1]
    has_window = kl is not None
    vct = vc.transpose(0, 2, 1)
    sink_rows = jnp.repeat(sink.astype(F32).reshape(WA_KV_HEADS, 1, WA_GROUP), LANES, axis=-1)
    qspec = pl.BlockSpec((None, LANES, 512), lambda b, i: (b, i, 0))
    in_specs = [qspec,
                pl.BlockSpec((None, n_ctx, LANES), lambda b, i: (b, 0, 0)),
                pl.BlockSpec((None, LANES, n_ctx), lambda b, i: (b, 0, 0)),
                pl.BlockSpec((WA_KV_HEADS, 1, WA_GROUP * LANES), lambda b, i: (0, 0, 0))]
    args = [q, kc, vct, sink_rows]
    if has_window:
        prev = lambda b, i: (b, jnp.maximum(i - 1, 0), 0)
        here = lambda b, i: (b, i, 0)
        nxt = lambda b, i: (b, jnp.minimum(i + 1, nblk - 1), 0)
        in_specs += 2 * [pl.BlockSpec((None, LANES, LANES), f) for f in (prev, here, nxt)]
        args += [kl, kl, kl, vl, vl, vl]
    return pl.pallas_call(
        functools.partial(_wa_kernel, has_window=has_window),
        grid=(bsz, nblk),
        in_specs=in_specs,
        out_specs=qspec,
        out_shape=jax.ShapeDtypeStruct(q.shape, BF16),
        compiler_params=_params("parallel", "parallel"),
        name="window_attention" if has_window else "context_gqa",
    )(*args)


def _att_out_kernel(x_ref, oa_ref, ob_ref, w_ref, gate_ref, o_ref):
    o = jnp.dot(oa_ref[...], w_ref[0:512, :], preferred_element_type=F32)
    o = o + jnp.dot(ob_ref[...], w_ref[512:1024, :], preferred_element_type=F32)
    o_ref[...] = x_ref[...] + gate_ref[...] * o


def _att_out(x, oa, ob, w_out, gate):
    bsz, t, d = x.shape
    tm = min(ROW_TILE, t)
    row = lambda w: pl.BlockSpec((None, tm, w), lambda b, i: (b, i, 0))
    return pl.pallas_call(
        _att_out_kernel,
        grid=(bsz, t // tm),
        in_specs=[row(d), row(512), row(512), _resident(w_out.shape),
                  pl.BlockSpec((None, 1, d), lambda b, i: (b, 0, 0))],
        out_specs=row(d),
        out_shape=jax.ShapeDtypeStruct(x.shape, F32),
        compiler_params=_params("parallel", "parallel"),
        name="att_out",
    )(x, oa, ob, w_out, gate)


def _ffn_kernel(xm_ref, xp_ref, xn_ref, g_ref, sh_ref, sc_ref, gate_ref, wg_ref, wu_ref, cw_ref,
                wd_ref, og_ref, o_ref, gs_ref, *, final_norm):
    i = pl.program_id(1)
    first = i == 0
    last = i == pl.num_programs(1) - 1
    tm = xm_ref.shape[0]
    xm = xm_ref[...]
    g, sh, sc = g_ref[...], sh_ref[...], sc_ref[...]
    hm = _modulated(xm, g, sh, sc)
    hh = _modulated(jnp.concatenate([xp_ref[...], xn_ref[...]], axis=0), g, sh, sc)
    acc = jnp.zeros((tm, xm.shape[1]), F32)
    n_chunks = D_FF // FFN_CHUNK

    def up(c):
        cs = slice(c * FFN_CHUNK, (c + 1) * FFN_CHUNK)
        return (jnp.dot(hm, wg_ref[:, cs], preferred_element_type=F32),
                jnp.dot(hh, wg_ref[:, cs], preferred_element_type=F32),
                jnp.dot(hm, wu_ref[:, cs], preferred_element_type=F32))

    nxt = up(0)
    for c in range(n_chunks):
        cs = slice(c * FFN_CHUNK, (c + 1) * FFN_CHUNK)
        gm, gh, u = nxt
        if c + 1 < n_chunks:
            nxt = up(c + 1)
        gs_ref[0:SUBLANES, :] = jnp.where(first, 0.0, gh[0:SUBLANES, :])
        gs_ref[SUBLANES:SUBLANES + tm, :] = gm
        gs_ref[SUBLANES + tm:2 * SUBLANES + tm, :] = jnp.where(last, 0.0, gh[SUBLANES:, :])
        cw = cw_ref[:, cs]
        gc = (cw[0:1, :] * gs_ref[SUBLANES - 1:SUBLANES - 1 + tm, :] + cw[1:2, :] * gm
              + cw[2:3, :] * gs_ref[SUBLANES + 1:SUBLANES + 1 + tm, :])
        act = (gc * _sigmoid(gc) * u).astype(BF16)
        acc = acc + jnp.dot(act, wd_ref[cs, :], preferred_element_type=F32)
    y = xm + gate_ref[...] * acc
    if final_norm:
        y = y * lax.rsqrt(jnp.mean(y * y, axis=-1, keepdims=True) + NORM_EPS) * og_ref[...]
    o_ref[...] = y


def _ffn(x, g, sh, sc, gate, w_g, w_u, conv_w, w_d, out_g, final_norm):
    bsz, t, d = x.shape
    tm = min(ROW_TILE, t)
    nb = tm // SUBLANES
    last_blk = t // SUBLANES - 1
    vec = pl.BlockSpec((None, 1, d), lambda b, i: (b, 0, 0))
    return pl.pallas_call(
        functools.partial(_ffn_kernel, final_norm=final_norm),
        grid=(bsz, t // tm),
        in_specs=[pl.BlockSpec((None, tm, d), lambda b, i: (b, i, 0)),
                  pl.BlockSpec((None, SUBLANES, d), lambda b, i: (b, jnp.maximum(i * nb - 1, 0), 0)),
                  pl.BlockSpec((None, SUBLANES, d),
                               lambda b, i: (b, jnp.minimum((i + 1) * nb, last_blk), 0)),
                  pl.BlockSpec((1, d), lambda b, i: (0, 0)), vec, vec, vec,
                  _resident(w_g.shape), _resident(w_u.shape), _resident(conv_w.shape),
                  _resident(w_d.shape), pl.BlockSpec((1, d), lambda b, i: (0, 0))],
        out_specs=pl.BlockSpec((None, tm, d), lambda b, i: (b, i, 0)),
        out_shape=jax.ShapeDtypeStruct(x.shape, F32),
        scratch_shapes=[pltpu.VMEM((tm + 2 * SUBLANES, FFN_CHUNK), F32)],
        compiler_params=_params("parallel", "parallel"),
        name="conv_ffn",
    )(x, x, x, g, sh, sc, gate, w_g, w_u, conv_w, w_d, out_g)


def _rec_inproj_kernel(x_ref, g_ref, sh_ref, sc_ref, w_ref, gl_ref, rl_ref, ul_ref):
    h = _modulated(x_ref[...], g_ref[...], sh_ref[...], sc_ref[...])
    z = jnp.dot(h, w_ref[...], preferred_element_type=F32)
    gl_ref[...] = z[:, 0:LRU_WIDTH]
    rl_ref[...] = z[:, LRU_WIDTH:2 * LRU_WIDTH]
    ul_ref[...] = z[:, 2 * LRU_WIDTH:]


def _rec_inproj(x, g, sh, sc, w_in):
    bsz, t, d = x.shape
    tm = min(ROW_TILE, t)
    row = lambda w: pl.BlockSpec((None, tm, w), lambda b, i: (b, i, 0))
    vec = pl.BlockSpec((None, 1, d), lambda b, i: (b, 0, 0))
    outs = [LRU_WIDTH, LRU_WIDTH, S5_WIDTH]
    return pl.pallas_call(
        _rec_inproj_kernel,
        grid=(bsz, t // tm),
        in_specs=[row(d), pl.BlockSpec((1, d), lambda b, i: (0, 0)), vec, vec, _resident(w_in.shape)],
        out_specs=[row(w) for w in outs],
        out_shape=[jax.ShapeDtypeStruct((bsz, t, w), F32) for w in outs],
        compiler_params=_params("parallel", "parallel"),
        name="rec_inproj",
    )(x, g, sh, sc, w_in)


def _lru_direction(xm_ref, xp_ref, xn_ref, at_start, at_end, cw_ref, cb_ref, w_ref, bias_ref, nsp,
                   out_ref, carry_ref, xs_ref, a_ref, b_ref, reverse):
    tm = xm_ref.shape[0]
    xs_ref[0:SUBLANES, :] = jnp.where(at_start, 0.0, xp_ref[...])
    xs_ref[SUBLANES:SUBLANES + tm, :] = xm_ref[...]
    xs_ref[SUBLANES + tm:2 * SUBLANES + tm, :] = jnp.where(at_end, 0.0, xn_ref[...])
    cw = cw_ref[...]
    rc = cb_ref[...] + cw[2:3, :] * xm_ref[...]
    for k in (0, 1, 3):
        rc = rc + cw[k:k + 1, :] * xs_ref[SUBLANES - 2 + k:SUBLANES - 2 + k + tm, :]
    gz = jnp.dot(rc.astype(BF16), w_ref[...], preferred_element_type=F32) + bias_ref[...]
    r = _sigmoid(gz[:, 0:LRU_WIDTH])
    ig = _sigmoid(gz[:, LRU_WIDTH:])
    a = jnp.exp(nsp * r)
    b = jnp.sqrt(1.0 - a * a) * (ig * rc)
    ngroups = tm // SUBLANES
    a = a.reshape(ngroups, SUBLANES, LRU_WIDTH)
    b = b.reshape(ngroups, SUBLANES, LRU_WIDTH)
    row8 = lax.broadcasted_iota(jnp.int32, a.shape, 1)
    for s in (1, 2, 4):
        if reverse:
            keep = row8 <= SUBLANES - 1 - s
            shift = SUBLANES - s
        else:
            keep = row8 >= s
            shift = s
        a_sh = jnp.where(keep, pltpu.roll(a, shift, 1), 1.0)
        b_sh = jnp.where(keep, pltpu.roll(b, shift, 1), 0.0)
        b = b + a * b_sh
        a = a * a_sh
    a_ref[...] = a.reshape(tm, LRU_WIDTH)
    b_ref[...] = b.reshape(tm, LRU_WIDTH)
    edge = 0 if reverse else SUBLANES - 1

    def body(j, carry):
        grp = (ngroups - 1 - j) if reverse else j
        off = pl.multiple_of(grp * SUBLANES, SUBLANES)
        h = b_ref[pl.ds(off, SUBLANES), :] + a_ref[pl.ds(off, SUBLANES), :] * carry
        out_ref[pl.ds(off, SUBLANES), :] = h
        return h[edge:edge + 1, :]

    carry_ref[...] = lax.fori_loop(0, ngroups, body, carry_ref[...])


def _lru_kernel(fm_ref, fp_ref, fn_ref, bm_ref, bp_ref, bn_ref, cw_ref, cb_ref, wf_ref, wb_ref,
                biasf_ref, biasb_ref, nsp_ref, h0_ref, hf_ref, hb_ref, hlast_ref,
                carry_f, carry_b, xs_ref, a_ref, b_ref):
    i = pl.program_id(1)
    first = i == 0
    last = i == pl.num_programs(1) - 1

    @pl.when(first)
    def _():
        carry_f[...] = h0_ref[0]
        carry_b[...] = h0_ref[1]

    _lru_direction(fm_ref, fp_ref, fn_ref, first, last, cw_ref, cb_ref, wf_ref, biasf_ref, nsp_ref[0],
                   hf_ref, carry_f, xs_ref, a_ref, b_ref, reverse=False)
    _lru_direction(bm_ref, bp_ref, bn_ref, last, first, cw_ref, cb_ref, wb_ref, biasb_ref, nsp_ref[1],
                   hb_ref, carry_b, xs_ref, a_ref, b_ref, reverse=True)
    hlast_ref[0] = carry_f[...]
    hlast_ref[1] = carry_b[...]


def _lru(rl, conv_w, conv_b, w_f, w_b, bias_f, bias_b, nsp, h0):
    bsz, t, c = rl.shape
    tm = min(ROW_TILE, t)
    nt = t // tm
    nb = tm // SUBLANES
    last_blk = t // SUBLANES - 1
    fwd = lambda b, i: (b, i, 0)
    bwd = lambda b, i: (b, nt - 1 - i, 0)
    fprev = lambda b, i: (b, jnp.maximum(i * nb - 1, 0), 0)
    fnext = lambda b, i: (b, jnp.minimum((i + 1) * nb, last_blk), 0)
    bprev = lambda b, i: (b, jnp.maximum((nt - 1 - i) * nb - 1, 0), 0)
    bnext = lambda b, i: (b, jnp.minimum((nt - i) * nb, last_blk), 0)
    main = lambda f: pl.BlockSpec((None, tm, c), f)
    halo = lambda f: pl.BlockSpec((None, SUBLANES, c), f)
    state = pl.BlockSpec((None, 2, 1, c), lambda b, i: (b, 0, 0, 0))
    return pl.pallas_call(
        _lru_kernel,
        grid=(bsz, nt),
        in_specs=[main(fwd), halo(fprev), halo(fnext), main(bwd), halo(bprev), halo(bnext),
                  _resident(conv_w.shape), _resident(conv_b.shape), _resident(w_f.shape),
                  _resident(w_b.shape), _resident(bias_f.shape), _resident(bias_b.shape),
                  _resident(nsp.shape), state],
        out_specs=[main(fwd), main(bwd), state],
        out_shape=[jax.ShapeDtypeStruct(rl.shape, F32), jax.ShapeDtypeStruct(rl.shape, F32),
                   jax.ShapeDtypeStruct((bsz, 2, 1, c), F32)],
        scratch_shapes=[pltpu.VMEM((1, c), F32), pltpu.VMEM((1, c), F32),
                        pltpu.VMEM((tm + 2 * SUBLANES, c), F32),
                        pltpu.VMEM((tm, c), F32), pltpu.VMEM((tm, c), F32)],
        compiler_params=_params("parallel", "arbitrary"),
        name="rglru",
    )(rl, rl, rl, rl, rl, rl, conv_w, conv_b, w_f, w_b, bias_f, bias_b, nsp, h0)


def _s5_in_kernel(u_ref, t_ref, bs_ref, yi_ref, xre_ref, xim_ref):
    x = jnp.zeros((u_ref.shape[1], 4 * LANES), F32)
    for q in range(2):
        ub = u_ref[q].astype(BF16)
        yi_ref[q] = jnp.dot(ub, t_ref[q], preferred_element_type=F32)
        x = x + jnp.dot(ub, bs_ref[q], preferred_element_type=F32)
    for d in range(2):
        xre_ref[d] = x[:, 2 * d * LANES:(2 * d + 1) * LANES]
        xim_ref[d] = x[:, (2 * d + 1) * LANES:(2 * d + 2) * LANES]


def _s5_scan_kernel(xre_ref, xim_ref, h0re_ref, h0im_ref, shr_ref, shi_ref, cwr_ref, cwi_ref,
                    sre_ref, sim_ref, lre_ref, lim_ref, pre_ref, pim_ref, *, reverse):
    nc, n = xre_ref.shape
    xr = xre_ref[...]
    xi = xim_ref[...]
    row8 = jnp.bitwise_and(lax.broadcasted_iota(jnp.int32, (nc, n), 0), SUBLANES - 1)
    for k, s in enumerate((1, 2, 4)):
        keep = (row8 <= SUBLANES - 1 - s) if reverse else (row8 >= s)
        shift = nc - s if reverse else s
        xr_sh = jnp.where(keep, pltpu.roll(xr, shift, 0), 0.0)
        xi_sh = jnp.where(keep, pltpu.roll(xi, shift, 0), 0.0)
        mr = shr_ref[k]
        mi = shi_ref[k]
        xr, xi = xr + (mr * xr_sh - mi * xi_sh), xi + (mr * xi_sh + mi * xr_sh)
    pre_ref[...] = xr
    pim_ref[...] = xi
    cwr = cwr_ref[...]
    cwi = cwi_ref[...]
    row = lax.broadcasted_iota(jnp.int32, (SUBLANES, n), 0)
    ngroups = nc // SUBLANES
    edge = 0 if reverse else SUBLANES - 1
    entry = SUBLANES - 1 - edge
    step = SUBLANES - 1 if reverse else 1

    def body(j, carry):
        cr, ci = carry
        grp = (ngroups - 1 - j) if reverse else j
        off = pl.multiple_of(grp * SUBLANES, SUBLANES)
        sr = pre_ref[pl.ds(off, SUBLANES), :] + (cwr * cr - cwi * ci)
        si = pim_ref[pl.ds(off, SUBLANES), :] + (cwr * ci + cwi * cr)
        pre_ref[pl.ds(off, SUBLANES), :] = jnp.where(row == entry, cr, pltpu.roll(sr, step, 0))
        pim_ref[pl.ds(off, SUBLANES), :] = jnp.where(row == entry, ci, pltpu.roll(si, step, 0))
        return sr[edge:edge + 1, :], si[edge:edge + 1, :]

    cr, ci = lax.fori_loop(0, ngroups, body, (h0re_ref[...], h0im_ref[...]))
    lre_ref[...] = cr
    lim_ref[...] = ci
    sre_ref[...] = pre_ref[...].astype(BF16)
    sim_ref[...] = pim_ref[...].astype(BF16)


def _s5_out_kernel(yi_ref, u_ref, sre_ref, sim_ref, cs_ref, d_ref, y_ref):
    s = jnp.concatenate([sre_ref[0], sim_ref[0], sre_ref[1], sim_ref[1]], axis=-1)
    for q in range(2):
        y = yi_ref[q] + d_ref[q] * u_ref[q]
        y_ref[q] = y + jnp.dot(s, cs_ref[q], preferred_element_type=F32)


def _s5(u, prm, h0re, h0im):
    bsz, t, _ = u.shape
    nc = t // S5_CHUNK
    ug = u.reshape(bsz, nc, S5_CHUNK, S5_GROUPS, S5_GROUP).transpose(0, 3, 1, 2, 4)
    ug = ug.reshape(bsz, S5_GROUPS, nc, 256)
    npair = S5_GROUPS // 2
    pair = pl.BlockSpec((None, 2, nc, 256), lambda b, p: (b, p, 0, 0))
    wpair = lambda r, w: pl.BlockSpec((2, r, w), lambda b, p: (p, 0, 0))
    lanes = pl.BlockSpec((None, 2, nc, LANES), lambda b, p: (b, 0, 0, p))
    state_shape = (bsz, 2, nc, S5_LANES)
    yi, xre, xim = pl.pallas_call(
        _s5_in_kernel,
        grid=(bsz, npair),
        in_specs=[pair, wpair(256, 256), wpair(256, 4 * LANES)],
        out_specs=[pair, lanes, lanes],
        out_shape=[jax.ShapeDtypeStruct(ug.shape, F32), jax.ShapeDtypeStruct(state_shape, F32),
                   jax.ShapeDtypeStruct(state_shape, F32)],
        compiler_params=_params("parallel", "parallel"),
        name="s5_in",
    )(ug, prm["toep"], prm["bs"])

    states = []
    for d in range(2):
        full = pl.BlockSpec((None, None, nc, S5_LANES), lambda b, d=d: (b, d, 0, 0))
        one = pl.BlockSpec((None, None, 1, S5_LANES), lambda b, d=d: (b, d, 0, 0))
        tab = pl.BlockSpec((None, 3, 1, S5_LANES), lambda b, d=d: (d, 0, 0, 0))
        tab8 = pl.BlockSpec((None, SUBLANES, S5_LANES), lambda b, d=d: (d, 0, 0))
        plain = pl.BlockSpec((None, nc, S5_LANES), lambda b: (b, 0, 0))
        plain1 = pl.BlockSpec((None, 1, S5_LANES), lambda b: (b, 0, 0))
        states.append(pl.pallas_call(
            functools.partial(_s5_scan_kernel, reverse=bool(d)),
            grid=(bsz,),
            in_specs=[full, full, one, one, tab, tab, tab8, tab8],
            out_specs=[plain, plain, plain1, plain1],
            out_shape=[jax.ShapeDtypeStruct((bsz, nc, S5_LANES), BF16),
                       jax.ShapeDtypeStruct((bsz, nc, S5_LANES), BF16),
                       jax.ShapeDtypeStruct((bsz, 1, S5_LANES), F32),
                       jax.ShapeDtypeStruct((bsz, 1, S5_LANES), F32)],
            scratch_shapes=[pltpu.VMEM((nc, S5_LANES), F32), pltpu.VMEM((nc, S5_LANES), F32)],
            compiler_params=_params("parallel"),
            name="s5_scan_bwd" if d else "s5_scan_fwd",
        )(xre, xim, h0re, h0im, prm["sh_r"], prm["sh_i"], prm["cw_r"], prm["cw_i"]))
    sre, sim, lre, lim = [jnp.stack([states[0][k], states[1][k]], axis=1) for k in range(4)]

    y = pl.pallas_call(
        _s5_out_kernel,
        grid=(bsz, npair),
        in_specs=[pair, pair, lanes, lanes, wpair(4 * LANES, 256), wpair(1, 256)],
        out_specs=pair,
        out_shape=jax.ShapeDtypeStruct(ug.shape, F32),
        compiler_params=_params("parallel", "parallel"),
        name="s5_out",
    )(yi, ug, sre, sim, prm["cs"], prm["dskip"])
    y = y.reshape(bsz, S5_GROUPS, nc, S5_CHUNK, S5_GROUP).transpose(0, 2, 3, 1, 4)
    return y.reshape(bsz, t, S5_WIDTH), lre, lim


def _s5_prepare(a_re, a_im, log_step, b_re, b_im, c_re, c_im, d_skip):
    a_re, a_im = a_re.astype(F32), a_im.astype(F32)
    step = jnp.exp(log_step.astype(F32))[..., None]
    mag = jnp.exp(a_re * step)
    abr, abi = mag * jnp.cos(a_im * step), mag * jnp.sin(a_im * step)
    den = a_re * a_re + a_im * a_im
    qr = ((abr - 1.0) * a_re + abi * a_im) / den
    qi = (abi * a_re - (abr - 1.0) * a_im) / den
    br_, bi_ = b_re.astype(F32), b_im.astype(F32)
    bbr = qr[..., None] * br_ - qi[..., None] * bi_
    bbi = qr[..., None] * bi_ + qi[..., None] * br_
    cr, ci = c_re.astype(F32), c_im.astype(F32)

    def power(n):
        n = jnp.asarray(n, F32)[..., None, None, None]
        m = jnp.exp(n * a_re * step)
        return m * jnp.cos(n * a_im * step), m * jnp.sin(n * a_im * step)

    lc = S5_CHUNK
    pr, pi = power(jnp.arange(lc + 1))
    mr = pr[..., None] * bbr - pi[..., None] * bbi
    mi = pr[..., None] * bbi + pi[..., None] * bbr
    hp = lax.Precision.HIGHEST
    kern = (jnp.einsum('dgop,jdgpi->jdgoi', cr, mr[:lc], precision=hp)
            - jnp.einsum('dgop,jdgpi->jdgoi', ci, mi[:lc], precision=hp))
    s_idx = jnp.arange(lc)[:, None]
    r_idx = jnp.arange(lc)[None, :]
    lag = jnp.clip(r_idx - s_idx, 0, lc - 1)
    toep = jnp.where((r_idx >= s_idx)[:, :, None, None, None, None], kern[lag], 0.0)
    toep = toep.transpose(2, 3, 0, 5, 1, 4).reshape(2, S5_GROUPS, lc * S5_GROUP, lc * S5_GROUP)
    rev = jnp.arange(lc - 1, -1, -1)
    bs_re = mr[rev].transpose(1, 2, 0, 4, 3).reshape(2, S5_GROUPS, lc * S5_GROUP, S5_STATE)
    bs_im = mi[rev].transpose(1, 2, 0, 4, 3).reshape(2, S5_GROUPS, lc * S5_GROUP, S5_STATE)
    pr1, pi1 = pr[1:], pi[1:]
    cs_re = cr[None] * pr1[:, :, :, None, :] - ci[None] * pi1[:, :, :, None, :]
    cs_im = -(cr[None] * pi1[:, :, :, None, :] + ci[None] * pr1[:, :, :, None, :])
    cs_re = cs_re.transpose(1, 2, 4, 0, 3).reshape(2, S5_GROUPS, S5_STATE, lc * S5_GROUP)
    cs_im = cs_im.transpose(1, 2, 4, 0, 3).reshape(2, S5_GROUPS, S5_STATE, lc * S5_GROUP)
    half = (jnp.arange(S5_GROUPS) % 2)[None, :, None, None]

    def pad_cols(m):
        z = jnp.zeros_like(m)
        return jnp.where(half == 0, jnp.concatenate([m, z], -1), jnp.concatenate([z, m], -1))

    def pad_rows(m):
        z = jnp.zeros_like(m)
        return jnp.where(half == 0, jnp.concatenate([m, z], -2), jnp.concatenate([z, m], -2))

    def mirror_rows(m):
        shp = m.shape
        return m.reshape(shp[0], lc, shp[1] // lc, shp[2])[:, ::-1].reshape(shp)

    def mirror_cols(m):
        shp = m.shape
        return m.reshape(shp[0], shp[1], lc, shp[2] // lc)[:, :, ::-1].reshape(shp)

    flat = lambda x: x.reshape(x.shape[:-2] + (S5_LANES,))
    shr, shi = power(jnp.asarray([lc, 2 * lc, 4 * lc]))
    cwr, cwi = power(lc * jnp.arange(1, SUBLANES + 1))
    cwr, cwi = flat(cwr).transpose(1, 0, 2), flat(cwi).transpose(1, 0, 2)
    bs_re, bs_im = pad_cols(bs_re), pad_cols(bs_im)
    cs_re, cs_im = pad_rows(cs_re), pad_rows(cs_im)
    return {
        "toep": (toep[0] + mirror_cols(mirror_rows(toep[1]))).astype(BF16),
        "bs": jnp.concatenate([bs_re[0], bs_im[0], mirror_rows(bs_re[1]), mirror_rows(bs_im[1])],
                              axis=-1).astype(BF16),
        "cs": jnp.concatenate([cs_re[0], cs_im[0], mirror_cols(cs_re[1]), mirror_cols(cs_im[1])],
                              axis=-2).astype(BF16),
        "sh_r": flat(shr).transpose(1, 0, 2)[:, :, None, :], "sh_i": flat(shi).transpose(1, 0, 2)[:, :, None, :],
        "cw_r": jnp.stack([cwr[0], cwr[1, ::-1]]), "cw_i": jnp.stack([cwi[0], cwi[1, ::-1]]),
        "dskip": jnp.tile(d_skip.astype(F32).reshape(S5_GROUPS, 1, S5_GROUP), (1, 1, lc)),
    }


def _rec_out_kernel(x_ref, gl_ref, hf_ref, hb_ref, y_ref, wglu_ref, w_ref, gate_ref, o_ref):
    t1 = ((hf_ref[...] + hb_ref[...]) * _gelu(gl_ref[...])).astype(BF16)
    yg = _gelu(y_ref[...])
    t2 = (yg * _sigmoid(jnp.dot(yg.astype(BF16), wglu_ref[...], preferred_element_type=F32))).astype(BF16)
    o = jnp.dot(t1, w_ref[0:LRU_WIDTH, :], preferred_element_type=F32)
    o = o + jnp.dot(t2, w_ref[LRU_WIDTH:, :], preferred_element_type=F32)
    o_ref[...] = x_ref[...] + gate_ref[...] * o


def _rec_out(x, gl, hf, hb, y, w_glu, w_out, gate):
    bsz, t, d = x.shape
    tm = min(ROW_TILE, t)
    row = lambda w: pl.BlockSpec((None, tm, w), lambda b, i: (b, i, 0))
    return pl.pallas_call(
        _rec_out_kernel,
        grid=(bsz, t // tm),
        in_specs=[row(d), row(LRU_WIDTH), row(LRU_WIDTH), row(LRU_WIDTH), row(S5_WIDTH),
                  _resident(w_glu.shape), _resident(w_out.shape),
                  pl.BlockSpec((None, 1, d), lambda b, i: (b, 0, 0))],
        out_specs=row(d),
        out_shape=jax.ShapeDtypeStruct(x.shape, F32),
        compiler_params=_params("parallel", "parallel"),
        name="rec_out",
    )(x, gl, hf, hb, y, w_glu, w_out, gate)


def _rope_tables(n, dim):
    rows = n // GRID_W
    row = jnp.repeat(jnp.arange(rows), GRID_W).astype(F32)
    col = jnp.tile(jnp.arange(GRID_W), rows).astype(F32)
    half = dim // 2
    freqs = ROPE_THETA ** (-jnp.arange(0, half, 2, dtype=F32) / half)

    def angles(pos):
        a = pos[:, None] * freqs[None, :]
        return jnp.concatenate([a, a], axis=-1)

    ang = jnp.concatenate([angles(row), angles(col)], axis=-1)
    reps = LANES // dim
    return jnp.tile(jnp.cos(ang), (1, reps)), jnp.tile(jnp.sin(ang), (1, reps))


def _rotated_columns(w, dim):
    quarter = dim // 4
    idx = jnp.arange(w.shape[1])
    low = (idx % (2 * quarter)) < quarter
    src = jnp.where(low, idx + quarter, idx - quarter)
    return jnp.where(low[None, :], -1.0, 1.0) * w[:, src]


def _att_weights(w_in):
    aq, ak = w_in[:, 0:512], w_in[:, 512:1024]
    bq, bk = w_in[:, 1536:2048], w_in[:, 2048:2176]
    ext = [w_in, _rotated_columns(aq, DA_DIM), _rotated_columns(ak, DA_DIM),
           _rotated_columns(bq, WA_DIM), _rotated_columns(bk, WA_DIM)]
    return jnp.concatenate(ext, axis=1).astype(BF16)


def _block_diag(w):
    eye = jnp.eye(LRU_BLOCKS, dtype=w.dtype)
    return jnp.einsum('nij,nm->nimj', w, eye).reshape(LRU_WIDTH, LRU_WIDTH)


def _vec3(v, bsz):
    return v.reshape(-1, 1, v.shape[-1]) if v.shape[0] == bsz else jnp.broadcast_to(v[None], (bsz, 1, v.shape[-1]))


def kernel(x, c, ctx, c_ctx, ada_w, ada_b, norm1_g, norm2_g, att_w_in, att_w_out, da_lam_q1, da_lam_k1, da_lam_q2, da_lam_k2, da_subln_g, wa_sink, rec_w_in, rec_w_out, lru_conv_w, lru_conv_b, lru_w_a, lru_b_a, lru_w_x, lru_b_x, lru_lam, s5_a_re, s5_a_im, s5_log_step, s5_b_re, s5_b_im, s5_c_re, s5_c_im, s5_d, s5_w_glu, ffn_w_g, ffn_w_u, ffn_conv_w, ffn_w_down, final_g):
    bsz, n_lat, d = x.shape
    n_ctx = ctx.shape[1]
    assert d == D_MODEL and bsz + 1 <= SUBLANES
    assert n_lat % ROW_TILE == 0 and n_ctx % DA_TK == 0 and n_ctx <= ROW_TILE
    x = x.astype(F32)
    ctx = ctx.astype(F32)

    cond = jnp.zeros((SUBLANES, d), F32).at[:bsz].set(c.astype(F32)).at[bsz].set(c_ctx.astype(F32))
    mods = _ada_all(cond, ada_w.astype(F32), ada_b.astype(F32))

    ca, sa = _rope_tables(n_lat, DA_DIM)
    cb, sb = _rope_tables(n_lat, WA_DIM)
    fg = final_g.reshape(1, d).astype(F32)
    ones = jnp.ones((n_ctx, LANES), F32)
    zeros = jnp.zeros((n_ctx, LANES), F32)

    for l in range(DEPTH):
        ctx_out = l < DEPTH - 1
        m = mods[l]
        sh1, sc1, g1, sh2, sc2, g2 = [_vec3(v, bsz) for v in jnp.split(m[:bsz], 6, axis=-1)]
        csh1, csc1, cg1, csh2, csc2, cg2 = [_vec3(v, bsz) for v in jnp.split(m[bsz:bsz + 1], 6, axis=-1)]
        n1 = norm1_g[l].reshape(1, d).astype(F32)
        n2 = norm2_g[l].reshape(1, d).astype(F32)
        j = l // 2
        if l % 2 == 0:
            lam_init = 0.8 - 0.6 * math.exp(-0.3 * l)
            w_ext = _att_weights(att_w_in[j].astype(F32))
            w_out = att_w_out[j].astype(BF16)
            lam = (jnp.exp(jnp.sum(da_lam_q1[j].astype(F32) * da_lam_k1[j].astype(F32)))
                   - jnp.exp(jnp.sum(da_lam_q2[j].astype(F32) * da_lam_k2[j].astype(F32))) + lam_init)
            aql, akl, avl, bql, bkl, bvl = _att_inproj(x, n1, sh1, sc1, w_ext, ca, sa, cb, sb)
            aqc, akc, avc, bqc, bkc, bvc = _att_inproj(ctx, n1, csh1, csc1, w_ext, ones, zeros, ones, zeros)
            k_all = jnp.concatenate([akc, akl], axis=1)
            v_all = jnp.concatenate([avc, avl], axis=-1)
            oal = _diff_attention(aql, k_all, v_all, lam, da_subln_g[j], 1.0 - lam_init)
            obl = _window_attention(bql, bkc, bvc, wa_sink[j], bkl, bvl)
            x = _att_out(x, oal, obl, w_out, g1)
            if ctx_out:
                oac = _diff_attention(aqc, akc, avc, lam, da_subln_g[j], 1.0 - lam_init)
                obc = _window_attention(bqc, bkc, bvc, wa_sink[j])
                ctx = _att_out(ctx, oac, obc, w_out, cg1)
        else:
            w_in = rec_w_in[j].astype(BF16)
            w_f = jnp.concatenate([_block_diag(lru_w_a[j, 0]), _block_diag(lru_w_x[j, 0])], axis=1).astype(BF16)
            w_b = jnp.concatenate([_block_diag(lru_w_a[j, 1]), _block_diag(lru_w_x[j, 1])], axis=1).astype(BF16)
            bias_f = jnp.concatenate([lru_b_a[j, 0], lru_b_x[j, 0]]).reshape(1, -1).astype(F32)
            bias_b = jnp.concatenate([lru_b_a[j, 1], lru_b_x[j, 1]]).reshape(1, -1).astype(F32)
            nsp = (-LRU_C * jax.nn.softplus(-lru_lam[j].astype(F32))).reshape(2, 1, LRU_WIDTH)
            conv_w = lru_conv_w[j].astype(F32)
            conv_b = lru_conv_b[j].reshape(1, -1).astype(F32)
            s5p = _s5_prepare(s5_a_re[j], s5_a_im[j], s5_log_step[j], s5_b_re[j], s5_b_im[j],
                              s5_c_re[j], s5_c_im[j], s5_d[j])
            gc_, rc_, uc_ = _rec_inproj(ctx, n1, csh1, csc1, w_in)
            gl_, rl_, ul_ = _rec_inproj(x, n1, sh1, sc1, w_in)
            h0 = jnp.zeros((bsz, 2, 1, LRU_WIDTH), F32)
            hfc, hbc, hlast = _lru(rc_, conv_w, conv_b, w_f, w_b, bias_f, bias_b, nsp, h0)
            hfl, hbl, _ = _lru(rl_, conv_w, conv_b, w_f, w_b, bias_f, bias_b, nsp, hlast)
            s0 = jnp.zeros((bsz, 2, 1, S5_LANES), F32)
            yc, lre, lim = _s5(uc_, s5p, s0, s0)
            yl, _, _ = _s5(ul_, s5p, lre, lim)
            w_glu = s5_w_glu[j].astype(BF16)
            w_out = rec_w_out[j].astype(BF16)
            x = _rec_out(x, gl_, hfl, hbl, yl, w_glu, w_out, g1)
            if ctx_out:
                ctx = _rec_out(ctx, gc_, hfc, hbc, yc, w_glu, w_out, cg1)
        wg, wu, wd = ffn_w_g[l].astype(BF16), ffn_w_u[l].astype(BF16), ffn_w_down[l].astype(BF16)
        cw = ffn_conv_w[l].astype(F32)
        x = _ffn(x, n2, sh2, sc2, g2, wg, wu, cw, wd, fg, final_norm=l == DEPTH - 1)
        if ctx_out:
            ctx = _ffn(ctx, n2, csh2, csc2, cg2, wg, wu, cw, wd, fg, final_norm=False)
    return x
```

```python
import functools
import math

import jax
import jax.numpy as jnp
from jax import lax
from jax.experimental import pallas as pl
from jax.experimental.pallas import tpu as pltpu

F32 = jnp.float32
BF16 = jnp.bfloat16

D_MODEL = 1024
DEPTH = 4
GRID_W = 64
ROPE_THETA = 10000.0
NORM_EPS = 1e-6
NEG_INF = -1e30
LOG2_E = math.log2(math.e)

DA_HEADS = 8
DA_DIM = 32
WA_HEADS = 8
WA_KV_HEADS = 2
WA_DIM = 64
WA_WINDOW = 128
WA_GROUP = WA_HEADS // WA_KV_HEADS
LRU_WIDTH = 768
LRU_BLOCKS = 8
LRU_BLOCK_DIM = LRU_WIDTH // LRU_BLOCKS
LRU_CONV = 4
LRU_C = 8.0
S5_WIDTH = 256
S5_GROUP = 16
S5_GROUPS = S5_WIDTH // S5_GROUP
S5_STATE = 64
S5_CHUNK = 16
S5_LANES = S5_GROUPS * S5_STATE
D_FF = 2816
FFN_CHUNK = 1408

DA_QK = DA_HEADS * 2 * DA_DIM
DA_V = DA_HEADS * 2 * DA_DIM
WA_Q = WA_HEADS * WA_DIM
WA_KV = WA_KV_HEADS * WA_DIM
ATT_IN = 2 * DA_QK + DA_V + WA_Q + 2 * WA_KV

SUBLANES = 8
LANES = 128
VMEM_LIMIT_BYTES = 56 * 1024 * 1024
ROW_TILE = 512
DA_TQ = 512
DA_TK = 256
DA_AHEAD = 2
DA_BOUND_LIMIT = 48.0
DA_UNROLL = 13
DA_BOUNDED_UNROLL = 65
DA_VROWS = 80


def _params(*sem):
    return pltpu.CompilerParams(dimension_semantics=sem, vmem_limit_bytes=VMEM_LIMIT_BYTES)


def _resident(shape):
    nd = len(shape)
    return pl.BlockSpec(shape, lambda *_: (0,) * nd, pipeline_mode=pl.Buffered(1))


def _sigmoid(x):
    return 1.0 / (1.0 + jnp.exp(-x))


def _gelu(x):
    return 0.5 * x * (1.0 + jnp.tanh(math.sqrt(2.0 / math.pi) * (x + 0.044715 * (x * x * x))))


def _modulated(x, g, shift, scale):
    y = x * lax.rsqrt(jnp.mean(x * x, axis=-1, keepdims=True) + NORM_EPS)
    return (y * g * (1.0 + scale) + shift).astype(BF16)


def _ada_kernel(c_ref, w_ref, b_ref, o_ref):
    c = c_ref[...]
    s = c * _sigmoid(c)
    o_ref[...] = jnp.dot(s, w_ref[...], precision=lax.Precision.HIGHEST,
                         preferred_element_type=F32) + b_ref[...]


def _ada_all(cond, ada_w, ada_b):
    depth, d, n = ada_w.shape
    tn = 1536
    return pl.pallas_call(
        _ada_kernel,
        grid=(depth, n // tn),
        in_specs=[pl.BlockSpec((SUBLANES, d), lambda l, j: (0, 0)),
                  pl.BlockSpec((None, d, tn), lambda l, j: (l, 0, j)),
                  pl.BlockSpec((None, 1, tn), lambda l, j: (l, 0, j))],
        out_specs=pl.BlockSpec((None, SUBLANES, tn), lambda l, j: (l, 0, j)),
        out_shape=jax.ShapeDtypeStruct((depth, SUBLANES, n), F32),
        compiler_params=_params("parallel", "parallel"),
        name="ada_mod",
    )(cond, ada_w, ada_b.reshape(depth, 1, n))


def _att_inproj_kernel(x_ref, g_ref, sh_ref, sc_ref, w_ref, ca_ref, sa_ref, cb_ref, sb_ref,
                       aq_ref, ak_ref, avt_ref, bq_ref, bk_ref, bv_ref):
    h = _modulated(x_ref[...], g_ref[...], sh_ref[...], sc_ref[...])
    z = jnp.dot(h, w_ref[...], preferred_element_type=F32)
    ca = jnp.tile(ca_ref[...], (1, 4))
    sa = jnp.tile(sa_ref[...], (1, 4))
    cb = cb_ref[...]
    sb = sb_ref[...]
    cb4 = jnp.tile(cb, (1, 4))
    sb4 = jnp.tile(sb, (1, 4))
    o = ATT_IN
    aq = z[:, 0:512] * ca + z[:, o:o + 512] * sa
    ak = z[:, 512:1024] * ca + z[:, o + 512:o + 1024] * sa
    bq = z[:, 1536:2048] * cb4 + z[:, o + 1024:o + 1536] * sb4
    bk = z[:, 2048:2176] * cb + z[:, o + 1536:o + 1664] * sb
    aq_ref[...] = (aq * (DA_DIM ** -0.5 * LOG2_E)).astype(BF16)
    ak_ref[...] = ak.astype(BF16)
    for hp in range(4):
        avt_ref[hp] = z[:, 1024 + LANES * hp:1024 + LANES * (hp + 1)].T.astype(BF16)
    bq_ref[...] = (bq * (WA_DIM ** -0.5)).astype(BF16)
    bk_ref[...] = bk.astype(BF16)
    bv_ref[...] = z[:, 2176:2304].astype(BF16)


def _att_inproj(x, g, sh, sc, w_ext, ca, sa, cb, sb):
    bsz, t, d = x.shape
    tm = min(ROW_TILE, t)
    n = w_ext.shape[1]
    row = lambda w: pl.BlockSpec((None, tm, w), lambda b, i: (b, i, 0))
    vec = pl.BlockSpec((None, 1, d), lambda b, i: (b, 0, 0))
    tab = pl.BlockSpec((tm, LANES), lambda b, i: (i, 0))
    rows = lambda w: (row(w), jax.ShapeDtypeStruct((bsz, t, w), BF16))
    outs = [rows(512), rows(512),
            (pl.BlockSpec((None, 4, LANES, tm), lambda b, i: (b, 0, 0, i)),
             jax.ShapeDtypeStruct((bsz, 4, LANES, t), BF16)),
            rows(512), rows(128), rows(128)]
    return pl.pallas_call(
        _att_inproj_kernel,
        grid=(bsz, t // tm),
        in_specs=[row(d), pl.BlockSpec((1, d), lambda b, i: (0, 0)), vec, vec, _resident((d, n)),
                  tab, tab, tab, tab],
        out_specs=[o[0] for o in outs],
        out_shape=[o[1] for o in outs],
        compiler_params=_params("parallel", "parallel"),
        name="att_inproj",
    )(x, g, sh, sc, w_ext, ca, sa, cb, sb)


def _da_kernel(lam_ref, q_ref, k_ref, vt_ref, g_ref, o_ref, acc_ref, s_ref, *, tk, nkt, out_scale):
    tq = q_ref.shape[0]
    qf = q_ref[...].astype(F32).T
    row = lax.broadcasted_iota(jnp.int32, qf.shape, 0)
    ones = jnp.where(lax.broadcasted_iota(jnp.int32, (DA_VROWS - 64, tk), 0) == 0, 1.0, 0.0).astype(BF16)
    w = [jnp.where(jnp.right_shift(row, 5) == s, qf, 0.0).astype(BF16) for s in range(4)]
    acc_ref[...] = jnp.zeros_like(acc_ref)

    def scores(kt, s):
        off = pl.multiple_of(kt * tk, tk)
        st = jnp.dot(k_ref[pl.ds(off, tk), :], w[s], preferred_element_type=F32)
        s_ref[s] = st
        return jnp.max(st, axis=0, keepdims=True)

    def absorb(s, vt, m, mt):
        m_new = jnp.maximum(m, mt)
        p = jnp.exp2(s_ref[s] - m_new).astype(BF16)
        pv = jnp.dot(vt[s // 2], p, preferred_element_type=F32)
        acc_ref[s] = jnp.exp2(m - m_new) * acc_ref[s] + pv
        return m_new

    def body(kt, carry):
        ms, mt0, mt1 = carry
        v2 = vt_ref[:, pl.ds(pl.multiple_of(kt * tk, tk), tk)]
        vt = [jnp.concatenate([v2[64 * h:64 * h + 64, :], ones], axis=0) for h in range(2)]
        nxt = jnp.minimum(kt + 1, nkt - 1)
        mt2 = scores(kt, 2)
        mt3 = scores(kt, 3)
        m0 = absorb(0, vt, ms[0], mt0)
        mt0 = scores(nxt, 0)
        m1 = absorb(1, vt, ms[1], mt1)
        mt1 = scores(nxt, 1)
        m2 = absorb(2, vt, ms[2], mt2)
        m3 = absorb(3, vt, ms[3], mt3)
        return (m0, m1, m2, m3), mt0, mt1

    m_init = tuple(jnp.full((1, tq), NEG_INF, F32) for _ in range(4))
    lax.fori_loop(0, nkt, body, (m_init, scores(0, 0), scores(0, 1)),
                  unroll=DA_UNROLL if nkt % DA_UNROLL == 0 else 1)
    lam = lam_ref[0]
    ys = []
    for h in range(2):
        a0 = acc_ref[2 * h]
        a1 = acc_ref[2 * h + 1]
        o = a0[0:64, :] / a0[64:65, :] - lam * (a1[0:64, :] / a1[64:65, :])
        y = o * lax.rsqrt(jnp.mean(o * o, axis=0, keepdims=True) + NORM_EPS)
        ys.append(y * g_ref[64 * h:64 * h + 64, :] * out_scale)
    o_ref[...] = jnp.concatenate(ys, axis=0).T.astype(o_ref.dtype)


def _da_bounded_kernel(lam_ref, kn_ref, q_ref, k_ref, vt_ref, g_ref, o_ref, acc_ref, l_ref, *, tk, nkt,
                       out_scale):
    b, hp = pl.program_id(0), pl.program_id(1)
    tq = q_ref.shape[0]
    qf = q_ref[...].astype(F32).T
    row = lax.broadcasted_iota(jnp.int32, qf.shape, 0)
    one_col = jnp.where(lax.broadcasted_iota(jnp.int32, (tk, LANES), 1) == 0, 1.0, 0.0).astype(BF16)
    w = []
    for s in range(4):
        qs = jnp.where(jnp.right_shift(row, 5) == s, qf, 0.0)
        bound = jnp.sqrt(jnp.sum(qs * qs, axis=0, keepdims=True)) * kn_ref[b, 4 * hp + s]
        w.append(jnp.concatenate([qs, jnp.where(row == 0, -bound, 0.0)], axis=0).astype(BF16))
    acc_ref[...] = jnp.zeros_like(acc_ref)
    l_ref[...] = jnp.zeros_like(l_ref)
    unroll = DA_BOUNDED_UNROLL if nkt % DA_BOUNDED_UNROLL == 0 else 1
    steps = [(u, s) for u in range(unroll) for s in range(4)]

    def trip(j, carry):
        def keys(u):
            off = pl.multiple_of((j * unroll + u) * tk, tk)
            return jnp.concatenate([k_ref[pl.ds(off, tk), :], one_col], axis=1)

        def values(u, h):
            off = pl.multiple_of((j * unroll + u) * tk, tk)
            return vt_ref[64 * h:64 * h + 64, pl.ds(off, tk)]

        def score(n):
            u, s = steps[n]
            return jnp.dot(keys(u), w[s], preferred_element_type=F32)

        queue = [score(n) for n in range(min(DA_AHEAD, len(steps)))]
        for n, (u, s) in enumerate(steps):
            if n + DA_AHEAD < len(steps):
                queue.append(score(n + DA_AHEAD))
            p = jnp.exp2(queue.pop(0))
            l_ref[s] = l_ref[s] + jnp.sum(p.reshape(tk // SUBLANES, SUBLANES, tq), axis=0)
            acc_ref[s] = acc_ref[s] + jnp.dot(values(u, s // 2), p.astype(BF16), preferred_element_type=F32)
        return carry

    lax.fori_loop(0, nkt // unroll, trip, 0)
    lam = lam_ref[0]
    ys = []
    for h in range(2):
        l0 = jnp.sum(l_ref[2 * h], axis=0, keepdims=True)
        l1 = jnp.sum(l_ref[2 * h + 1], axis=0, keepdims=True)
        o = acc_ref[2 * h] / l0 - lam * (acc_ref[2 * h + 1] / l1)
        y = o * lax.rsqrt(jnp.mean(o * o, axis=0, keepdims=True) + NORM_EPS)
        ys.append(y * g_ref[64 * h:64 * h + 64, :] * out_scale)
    o_ref[...] = jnp.concatenate(ys, axis=0).T.astype(o_ref.dtype)


def _diff_attention(q, k_all, vt_all, lam, subln_g, out_scale):
    bsz, tq_all, _ = q.shape
    tk_all = k_all.shape[1]
    tq = min(DA_TQ, tq_all)
    tk = DA_TK
    nkt = tk_all // tk
    gcol = jnp.tile(subln_g.astype(F32), 2).reshape(LANES, 1)
    lam1 = lam.reshape(1).astype(F32)
    smem = pl.BlockSpec(memory_space=pltpu.SMEM)
    specs = [pl.BlockSpec((None, tq, LANES), lambda b, hp, i: (b, i, hp)),
             pl.BlockSpec((None, tk_all, LANES), lambda b, hp, i: (b, 0, hp)),
             pl.BlockSpec((None, None, LANES, tk_all), lambda b, hp, i: (b, hp, 0, 0)),
             pl.BlockSpec((LANES, 1), lambda b, hp, i: (0, 0))]
    common = dict(grid=(bsz, 4, tq_all // tq),
                  out_specs=pl.BlockSpec((None, tq, LANES), lambda b, hp, i: (b, i, hp)),
                  out_shape=jax.ShapeDtypeStruct((bsz, tq_all, 512), BF16),
                  compiler_params=_params("parallel", "parallel", "arbitrary"))

    def running_max(_):
        return pl.pallas_call(
            functools.partial(_da_kernel, tk=tk, nkt=nkt, out_scale=out_scale),
            in_specs=[smem] + specs,
            scratch_shapes=[pltpu.VMEM((4, DA_VROWS, tq), F32), pltpu.VMEM((4, tk, tq), F32)],
            name="diff_attention", **common)(lam1, q, k_all, vt_all, gcol)

    def bounded(kn):
        return pl.pallas_call(
            functools.partial(_da_bounded_kernel, tk=tk, nkt=nkt, out_scale=out_scale),
            in_specs=[smem, smem] + specs,
            scratch_shapes=[pltpu.VMEM((4, 64, tq), F32), pltpu.VMEM((4, SUBLANES, tq), F32)],
            name="diff_attention_bounded", **common)(lam1, kn, q, k_all, vt_all, gcol)

    norms = lambda z: jnp.sqrt(jnp.max(jnp.sum(jnp.square(z.astype(F32).reshape(bsz, -1, 16, DA_DIM)), -1), 1))
    kn = norms(k_all)
    safe = jnp.max(norms(q) * kn) < DA_BOUND_LIMIT
    return lax.cond(safe, bounded, running_max, kn)


def _wa_kernel(*refs, has_window):
    if has_window:
        (q_ref, kc_ref, vct_ref, sink_ref, kp_ref, km_ref, kn_ref, vp_ref, vm_ref, vn_ref, o_ref) = refs
    else:
        (q_ref, kc_ref, vct_ref, sink_ref, o_ref) = refs
    i = pl.program_id(1)
    nblk = pl.num_programs(1)
    zeros = jnp.zeros((WA_DIM, 4 * LANES), BF16)
    qt = q_ref[...].astype(F32).T
    outs = []
    if has_window:
        vwin = [v_ref[...].astype(F32).T.astype(BF16) for v_ref in (vp_ref, vm_ref, vn_ref)]
        r = lax.broadcasted_iota(jnp.int32, (LANES, 4 * LANES), 0)
        c = jnp.bitwise_and(lax.broadcasted_iota(jnp.int32, (LANES, 4 * LANES), 1), LANES - 1)
        off_p = jnp.where(i > 0, 0, 2 * LANES)
        off_n = jnp.where(i < nblk - 1, 0, 2 * LANES)
        mask_p = r >= c + off_p
        mask_n = r + off_n <= c
    scores = []
    for h in range(WA_KV_HEADS):
        heads = [WA_DIM * (WA_GROUP * h + g) for g in range(WA_GROUP)]
        q = jnp.concatenate([qt[r0:r0 + WA_DIM, :] for r0 in heads], axis=1).astype(BF16)
        w = jnp.concatenate([q, zeros] if h == 0 else [zeros, q], axis=0)
        parts = [jnp.dot(kc_ref[...], w, preferred_element_type=F32)]
        if has_window:
            parts += [jnp.dot(k_ref[...], w, preferred_element_type=F32) for k_ref in (kp_ref, km_ref, kn_ref)]
        scores.append(parts)
    for h in range(WA_KV_HEADS):
        hs = slice(WA_DIM * h, WA_DIM * h + WA_DIM)
        sink = sink_ref[h]
        parts = scores[h]
        vts = [vct_ref[hs, :]]
        if has_window:
            parts = [parts[0], jnp.where(mask_p, parts[1], NEG_INF), parts[2],
                     jnp.where(mask_n, parts[3], NEG_INF)]
            vts += [v[hs, :] for v in vwin]
        m = sink
        for s in parts:
            m = jnp.maximum(m, jnp.max(s, axis=0, keepdims=True))
        den = jnp.exp(sink - m)
        o = jnp.zeros((WA_DIM, 4 * LANES), F32)
        for s, vt in zip(parts, vts):
            p = jnp.exp(s - m)
            den = den + jnp.sum(p, axis=0, keepdims=True)
            o = o + jnp.dot(vt, p.astype(BF16), preferred_element_type=F32)
        o = o / den
        outs += [o[:, LANES * g:LANES * (g + 1)] for g in range(WA_GROUP)]
    o_ref[...] = jnp.concatenate(outs, axis=0).T.astype(o_ref.dtype)


def _window_attention(q, kc, vc, sink, kl=None, vl=None):
    bsz, t, _ = q.shape
    nblk = t // LANES
    n_ctx = kc.shape[1]
    has_window = kl is not None
    vct = vc.transpose(0, 2, 1)
    sink_rows = jnp.repeat(sink.astype(F32).reshape(WA_KV_HEADS, 1, WA_GROUP), LANES, axis=-1)
    qspec = pl.BlockSpec((None, LANES, 512), lambda b, i: (b, i, 0))
    in_specs = [qspec,
                pl.BlockSpec((None, n_ctx, LANES), lambda b, i: (b, 0, 0)),
                pl.BlockSpec((None, LANES, n_ctx), lambda b, i: (b, 0, 0)),
                pl.BlockSpec((WA_KV_HEADS, 1, WA_GROUP * LANES), lambda b, i: (0, 0, 0))]
    args = [q, kc, vct, sink_rows]
    if has_window:
        prev = lambda b, i: (b, jnp.maximum(i - 1, 0), 0)
        here = lambda b, i: (b, i, 0)
        nxt = lambda b, i: (b, jnp.minimum(i + 1, nblk - 1), 0)
        in_specs += 2 * [pl.BlockSpec((None, LANES, LANES), f) for f in (prev, here, nxt)]
        args += [kl, kl, kl, vl, vl, vl]
    return pl.pallas_call(
        functools.partial(_wa_kernel, has_window=has_window),
        grid=(bsz, nblk),
        in_specs=in_specs,
        out_specs=qspec,
        out_shape=jax.ShapeDtypeStruct(q.shape, BF16),
        compiler_params=_params("parallel", "parallel"),
        name="window_attention" if has_window else "context_gqa",
    )(*args)


def _att_out_kernel(x_ref, oa_ref, ob_ref, w_ref, gate_ref, o_ref):
    o = jnp.dot(oa_ref[...], w_ref[0:512, :], preferred_element_type=F32)
    o = o + jnp.dot(ob_ref[...], w_ref[512:1024, :], preferred_element_type=F32)
    o_ref[...] = x_ref[...] + gate_ref[...] * o


def _att_out(x, oa, ob, w_out, gate):
    bsz, t, d = x.shape
    tm = min(ROW_TILE, t)
    row = lambda w: pl.BlockSpec((None, tm, w), lambda b, i: (b, i, 0))
    return pl.pallas_call(
        _att_out_kernel,
        grid=(bsz, t // tm),
        in_specs=[row(d), row(512), row(512), _resident(w_out.shape),
                  pl.BlockSpec((None, 1, d), lambda b, i: (b, 0, 0))],
        out_specs=row(d),
        out_shape=jax.ShapeDtypeStruct(x.shape, F32),
        compiler_params=_params("parallel", "parallel"),
        name="att_out",
    )(x, oa, ob, w_out, gate)


def _ffn_kernel(xm_ref, xp_ref, xn_ref, g_ref, sh_ref, sc_ref, gate_ref, wg_ref, wu_ref, cw_ref,
                wd_ref, og_ref, o_ref, gs_ref, *, final_norm):
    i = pl.program_id(1)
    first = i == 0
    last = i == pl.num_programs(1) - 1
    tm = xm_ref.shape[0]
    xm = xm_ref[...]
    g, sh, sc = g_ref[...], sh_ref[...], sc_ref[...]
    hm = _modulated(xm, g, sh, sc)
    hh = _modulated(jnp.concatenate([xp_ref[...], xn_ref[...]], axis=0), g, sh, sc)
    acc = jnp.zeros((tm, xm.shape[1]), F32)
    n_chunks = D_FF // FFN_CHUNK

    def up(c):
        cs = slice(c * FFN_CHUNK, (c + 1) * FFN_CHUNK)
        return (jnp.dot(hm, wg_ref[:, cs], preferred_element_type=F32),
                jnp.dot(hh, wg_ref[:, cs], preferred_element_type=F32),
                jnp.dot(hm, wu_ref[:, cs], preferred_element_type=F32))

    nxt = up(0)
    for c in range(n_chunks):
        cs = slice(c * FFN_CHUNK, (c + 1) * FFN_CHUNK)
        gm, gh, u = nxt
        if c + 1 < n_chunks:
            nxt = up(c + 1)
        gs_ref[0:SUBLANES, :] = jnp.where(first, 0.0, gh[0:SUBLANES, :])
        gs_ref[SUBLANES:SUBLANES + tm, :] = gm
        gs_ref[SUBLANES + tm:2 * SUBLANES + tm, :] = jnp.where(last, 0.0, gh[SUBLANES:, :])
        cw = cw_ref[:, cs]
        gc = (cw[0:1, :] * gs_ref[SUBLANES - 1:SUBLANES - 1 + tm, :] + cw[1:2, :] * gm
              + cw[2:3, :] * gs_ref[SUBLANES + 1:SUBLANES + 1 + tm, :])
        act = (gc * _sigmoid(gc) * u).astype(BF16)
        acc = acc + jnp.dot(act, wd_ref[cs, :], preferred_element_type=F32)
    y = xm + gate_ref[...] * acc
    if final_norm:
        y = y * lax.rsqrt(jnp.mean(y * y, axis=-1, keepdims=True) + NORM_EPS) * og_ref[...]
    o_ref[...] = y


def _ffn(x, g, sh, sc, gate, w_g, w_u, conv_w, w_d, out_g, final_norm):
    bsz, t, d = x.shape
    tm = min(ROW_TILE, t)
    nb = tm // SUBLANES
    last_blk = t // SUBLANES - 1
    vec = pl.BlockSpec((None, 1, d), lambda b, i: (b, 0, 0))
    return pl.pallas_call(
        functools.partial(_ffn_kernel, final_norm=final_norm),
        grid=(bsz, t // tm),
        in_specs=[pl.BlockSpec((None, tm, d), lambda b, i: (b, i, 0)),
                  pl.BlockSpec((None, SUBLANES, d), lambda b, i: (b, jnp.maximum(i * nb - 1, 0), 0)),
                  pl.BlockSpec((None, SUBLANES, d),
                               lambda b, i: (b, jnp.minimum((i + 1) * nb, last_blk), 0)),
                  pl.BlockSpec((1, d), lambda b, i: (0, 0)), vec, vec, vec,
                  _resident(w_g.shape), _resident(w_u.shape), _resident(conv_w.shape),
                  _resident(w_d.shape), pl.BlockSpec((1, d), lambda b, i: (0, 0))],
        out_specs=pl.BlockSpec((None, tm, d), lambda b, i: (b, i, 0)),
        out_shape=jax.ShapeDtypeStruct(x.shape, F32),
        scratch_shapes=[pltpu.VMEM((tm + 2 * SUBLANES, FFN_CHUNK), F32)],
        compiler_params=_params("parallel", "parallel"),
        name="conv_ffn",
    )(x, x, x, g, sh, sc, gate, w_g, w_u, conv_w, w_d, out_g)


def _rec_inproj_kernel(x_ref, g_ref, sh_ref, sc_ref, w_ref, gl_ref, rl_ref, ul_ref):
    h = _modulated(x_ref[...], g_ref[...], sh_ref[...], sc_ref[...])
    z = jnp.dot(h, w_ref[...], preferred_element_type=F32)
    gl_ref[...] = z[:, 0:LRU_WIDTH]
    rl_ref[...] = z[:, LRU_WIDTH:2 * LRU_WIDTH]
    ul_ref[...] = z[:, 2 * LRU_WIDTH:]


def _rec_inproj(x, g, sh, sc, w_in):
    bsz, t, d = x.shape
    tm = min(ROW_TILE, t)
    row = lambda w: pl.BlockSpec((None, tm, w), lambda b, i: (b, i, 0))
    vec = pl.BlockSpec((None, 1, d), lambda b, i: (b, 0, 0))
    outs = [LRU_WIDTH, LRU_WIDTH, S5_WIDTH]
    return pl.pallas_call(
        _rec_inproj_kernel,
        grid=(bsz, t // tm),
        in_specs=[row(d), pl.BlockSpec((1, d), lambda b, i: (0, 0)), vec, vec, _resident(w_in.shape)],
        out_specs=[row(w) for w in outs],
        out_shape=[jax.ShapeDtypeStruct((bsz, t, w), F32) for w in outs],
        compiler_params=_params("parallel", "parallel"),
        name="rec_inproj",
    )(x, g, sh, sc, w_in)


def _lru_direction(xm_ref, xp_ref, xn_ref, at_start, at_end, cw_ref, cb_ref, w_ref, bias_ref, nsp,
                   out_ref, carry_ref, xs_ref, a_ref, b_ref, reverse):
    tm = xm_ref.shape[0]
    xs_ref[0:SUBLANES, :] = jnp.where(at_start, 0.0, xp_ref[...])
    xs_ref[SUBLANES:SUBLANES + tm, :] = xm_ref[...]
    xs_ref[SUBLANES + tm:2 * SUBLANES + tm, :] = jnp.where(at_end, 0.0, xn_ref[...])
    cw = cw_ref[...]
    rc = cb_ref[...] + cw[2:3, :] * xm_ref[...]
    for k in (0, 1, 3):
        rc = rc + cw[k:k + 1, :] * xs_ref[SUBLANES - 2 + k:SUBLANES - 2 + k + tm, :]
    gz = jnp.dot(rc.astype(BF16), w_ref[...], preferred_element_type=F32) + bias_ref[...]
    r = _sigmoid(gz[:, 0:LRU_WIDTH])
    ig = _sigmoid(gz[:, LRU_WIDTH:])
    a = jnp.exp(nsp * r)
    b = jnp.sqrt(1.0 - a * a) * (ig * rc)
    ngroups = tm // SUBLANES
    a = a.reshape(ngroups, SUBLANES, LRU_WIDTH)
    b = b.reshape(ngroups, SUBLANES, LRU_WIDTH)
    row8 = lax.broadcasted_iota(jnp.int32, a.shape, 1)
    for s in (1, 2, 4):
        if reverse:
            keep = row8 <= SUBLANES - 1 - s
            shift = SUBLANES - s
        else:
            keep = row8 >= s
            shift = s
        a_sh = jnp.where(keep, pltpu.roll(a, shift, 1), 1.0)
        b_sh = jnp.where(keep, pltpu.roll(b, shift, 1), 0.0)
        b = b + a * b_sh
        a = a * a_sh
    a_ref[...] = a.reshape(tm, LRU_WIDTH)
    b_ref[...] = b.reshape(tm, LRU_WIDTH)
    edge = 0 if reverse else SUBLANES - 1

    def body(j, carry):
        grp = (ngroups - 1 - j) if reverse else j
        off = pl.multiple_of(grp * SUBLANES, SUBLANES)
        h = b_ref[pl.ds(off, SUBLANES), :] + a_ref[pl.ds(off, SUBLANES), :] * carry
        out_ref[pl.ds(off, SUBLANES), :] = h
        return h[edge:edge + 1, :]

    carry_ref[...] = lax.fori_loop(0, ngroups, body, carry_ref[...])


def _lru_kernel(fm_ref, fp_ref, fn_ref, bm_ref, bp_ref, bn_ref, cw_ref, cb_ref, wf_ref, wb_ref,
                biasf_ref, biasb_ref, nsp_ref, h0_ref, hf_ref, hb_ref, hlast_ref,
                carry_f, carry_b, xs_ref, a_ref, b_ref):
    i = pl.program_id(1)
    first = i == 0
    last = i == pl.num_programs(1) - 1

    @pl.when(first)
    def _():
        carry_f[...] = h0_ref[0]
        carry_b[...] = h0_ref[1]

    _lru_direction(fm_ref, fp_ref, fn_ref, first, last, cw_ref, cb_ref, wf_ref, biasf_ref, nsp_ref[0],
                   hf_ref, carry_f, xs_ref, a_ref, b_ref, reverse=False)
    _lru_direction(bm_ref, bp_ref, bn_ref, last, first, cw_ref, cb_ref, wb_ref, biasb_ref, nsp_ref[1],
                   hb_ref, carry_b, xs_ref, a_ref, b_ref, reverse=True)
    hlast_ref[0] = carry_f[...]
    hlast_ref[1] = carry_b[...]


def _lru(rl, conv_w, conv_b, w_f, w_b, bias_f, bias_b, nsp, h0):
    bsz, t, c = rl.shape
    tm = min(ROW_TILE, t)
    nt = t // tm
    nb = tm // SUBLANES
    last_blk = t // SUBLANES - 1
    fwd = lambda b, i: (b, i, 0)
    bwd = lambda b, i: (b, nt - 1 - i, 0)
    fprev = lambda b, i: (b, jnp.maximum(i * nb - 1, 0), 0)
    fnext = lambda b, i: (b, jnp.minimum((i + 1) * nb, last_blk), 0)
    bprev = lambda b, i: (b, jnp.maximum((nt - 1 - i) * nb - 1, 0), 0)
    bnext = lambda b, i: (b, jnp.minimum((nt - i) * nb, last_blk), 0)
    main = lambda f: pl.BlockSpec((None, tm, c), f)
    halo = lambda f: pl.BlockSpec((None, SUBLANES, c), f)
    state = pl.BlockSpec((None, 2, 1, c), lambda b, i: (b, 0, 0, 0))
    return pl.pallas_call(
        _lru_kernel,
        grid=(bsz, nt),
        in_specs=[main(fwd), halo(fprev), halo(fnext), main(bwd), halo(bprev), halo(bnext),
                  _resident(conv_w.shape), _resident(conv_b.shape), _resident(w_f.shape),
                  _resident(w_b.shape), _resident(bias_f.shape), _resident(bias_b.shape),
                  _resident(nsp.shape), state],
        out_specs=[main(fwd), main(bwd), state],
        out_shape=[jax.ShapeDtypeStruct(rl.shape, F32), jax.ShapeDtypeStruct(rl.shape, F32),
                   jax.ShapeDtypeStruct((bsz, 2, 1, c), F32)],
        scratch_shapes=[pltpu.VMEM((1, c), F32), pltpu.VMEM((1, c), F32),
                        pltpu.VMEM((tm + 2 * SUBLANES, c), F32),
                        pltpu.VMEM((tm, c), F32), pltpu.VMEM((tm, c), F32)],
        compiler_params=_params("parallel", "arbitrary"),
        name="rglru",
    )(rl, rl, rl, rl, rl, rl, conv_w, conv_b, w_f, w_b, bias_f, bias_b, nsp, h0)


def _s5_in_kernel(u_ref, t_ref, bs_ref, yi_ref, xre_ref, xim_ref):
    x = jnp.zeros((u_ref.shape[1], 4 * LANES), F32)
    for q in range(2):
        ub = u_ref[q].astype(BF16)
        yi_ref[q] = jnp.dot(ub, t_ref[q], preferred_element_type=F32)
        x = x + jnp.dot(ub, bs_ref[q], preferred_element_type=F32)
    for d in range(2):
        xre_ref[d] = x[:, 2 * d * LANES:(2 * d + 1) * LANES]
        xim_ref[d] = x[:, (2 * d + 1) * LANES:(2 * d + 2) * LANES]


def _s5_scan_kernel(xre_ref, xim_ref, h0re_ref, h0im_ref, shr_ref, shi_ref, cwr_ref, cwi_ref,
                    sre_ref, sim_ref, lre_ref, lim_ref, pre_ref, pim_ref, *, reverse):
    nc, n = xre_ref.shape
    xr = xre_ref[...]
    xi = xim_ref[...]
    row8 = jnp.bitwise_and(lax.broadcasted_iota(jnp.int32, (nc, n), 0), SUBLANES - 1)
    for k, s in enumerate((1, 2, 4)):
        keep = (row8 <= SUBLANES - 1 - s) if reverse else (row8 >= s)
        shift = nc - s if reverse else s
        xr_sh = jnp.where(keep, pltpu.roll(xr, shift, 0), 0.0)
        xi_sh = jnp.where(keep, pltpu.roll(xi, shift, 0), 0.0)
        mr = shr_ref[k]
        mi = shi_ref[k]
        xr, xi = xr + (mr * xr_sh - mi * xi_sh), xi + (mr * xi_sh + mi * xr_sh)
    pre_ref[...] = xr
    pim_ref[...] = xi
    cwr = cwr_ref[...]
    cwi = cwi_ref[...]
    row = lax.broadcasted_iota(jnp.int32, (SUBLANES, n), 0)
    ngroups = nc // SUBLANES
    edge = 0 if reverse else SUBLANES - 1
    entry = SUBLANES - 1 - edge
    step = SUBLANES - 1 if reverse else 1

    def body(j, carry):
        cr, ci = carry
        grp = (ngroups - 1 - j) if reverse else j
        off = pl.multiple_of(grp * SUBLANES, SUBLANES)
        sr = pre_ref[pl.ds(off, SUBLANES), :] + (cwr * cr - cwi * ci)
        si = pim_ref[pl.ds(off, SUBLANES), :] + (cwr * ci + cwi * cr)
        pre_ref[pl.ds(off, SUBLANES), :] = jnp.where(row == entry, cr, pltpu.roll(sr, step, 0))
        pim_ref[pl.ds(off, SUBLANES), :] = jnp.where(row == entry, ci, pltpu.roll(si, step, 0))
        return sr[edge:edge + 1, :], si[edge:edge + 1, :]

    cr, ci = lax.fori_loop(0, ngroups, body, (h0re_ref[...], h0im_ref[...]))
    lre_ref[...] = cr
    lim_ref[...] = ci
    sre_ref[...] = pre_ref[...].astype(BF16)
    sim_ref[...] = pim_ref[...].astype(BF16)


def _s5_out_kernel(yi_ref, u_ref, sre_ref, sim_ref, cs_ref, d_ref, y_ref):
    s = jnp.concatenate([sre_ref[0], sim_ref[0], sre_ref[1], sim_ref[1]], axis=-1)
    for q in range(2):
        y = yi_ref[q] + d_ref[q] * u_ref[q]
        y_ref[q] = y + jnp.dot(s, cs_ref[q], preferred_element_type=F32)


def _s5(u, prm, h0re, h0im):
    bsz, t, _ = u.shape
    nc = t // S5_CHUNK
    ug = u.reshape(bsz, nc, S5_CHUNK, S5_GROUPS, S5_GROUP).transpose(0, 3, 1, 2, 4)
    ug = ug.reshape(bsz, S5_GROUPS, nc, 256)
    npair = S5_GROUPS // 2
    pair = pl.BlockSpec((None, 2, nc, 256), lambda b, p: (b, p, 0, 0))
    wpair = lambda r, w: pl.BlockSpec((2, r, w), lambda b, p: (p, 0, 0))
    lanes = pl.BlockSpec((None, 2, nc, LANES), lambda b, p: (b, 0, 0, p))
    state_shape = (bsz, 2, nc, S5_LANES)
    yi, xre, xim = pl.pallas_call(
        _s5_in_kernel,
        grid=(bsz, npair),
        in_specs=[pair, wpair(256, 256), wpair(256, 4 * LANES)],
        out_specs=[pair, lanes, lanes],
        out_shape=[jax.ShapeDtypeStruct(ug.shape, F32), jax.ShapeDtypeStruct(state_shape, F32),
                   jax.ShapeDtypeStruct(state_shape, F32)],
        compiler_params=_params("parallel", "parallel"),
        name="s5_in",
    )(ug, prm["toep"], prm["bs"])

    states = []
    for d in range(2):
        full = pl.BlockSpec((None, None, nc, S5_LANES), lambda b, d=d: (b, d, 0, 0))
        one = pl.BlockSpec((None, None, 1, S5_LANES), lambda b, d=d: (b, d, 0, 0))
        tab = pl.BlockSpec((None, 3, 1, S5_LANES), lambda b, d=d: (d, 0, 0, 0))
        tab8 = pl.BlockSpec((None, SUBLANES, S5_LANES), lambda b, d=d: (d, 0, 0))
        plain = pl.BlockSpec((None, nc, S5_LANES), lambda b: (b, 0, 0))
        plain1 = pl.BlockSpec((None, 1, S5_LANES), lambda b: (b, 0, 0))
        states.append(pl.pallas_call(
            functools.partial(_s5_scan_kernel, reverse=bool(d)),
            grid=(bsz,),
            in_specs=[full, full, one, one, tab, tab, tab8, tab8],
            out_specs=[plain, plain, plain1, plain1],
            out_shape=[jax.ShapeDtypeStruct((bsz, nc, S5_LANES), BF16),
                       jax.ShapeDtypeStruct((bsz, nc, S5_LANES), BF16),
                       jax.ShapeDtypeStruct((bsz, 1, S5_LANES), F32),
                       jax.ShapeDtypeStruct((bsz, 1, S5_LANES), F32)],
            scratch_shapes=[pltpu.VMEM((nc, S5_LANES), F32), pltpu.VMEM((nc, S5_LANES), F32)],
            compiler_params=_params("parallel"),
            name="s5_scan_bwd" if d else "s5_scan_fwd",
        )(xre, xim, h0re, h0im, prm["sh_r"], prm["sh_i"], prm["cw_r"], prm["cw_i"]))
    sre, sim, lre, lim = [jnp.stack([states[0][k], states[1][k]], axis=1) for k in range(4)]

    y = pl.pallas_call(
        _s5_out_kernel,
        grid=(bsz, npair),
        in_specs=[pair, pair, lanes, lanes, wpair(4 * LANES, 256), wpair(1, 256)],
        out_specs=pair,
        out_shape=jax.ShapeDtypeStruct(ug.shape, F32),
        compiler_params=_params("parallel", "parallel"),
        name="s5_out",
    )(yi, ug, sre, sim, prm["cs"], prm["dskip"])
    y = y.reshape(bsz, S5_GROUPS, nc, S5_CHUNK, S5_GROUP).transpose(0, 2, 3, 1, 4)
    return y.reshape(bsz, t, S5_WIDTH), lre, lim


def _s5_prepare(a_re, a_im, log_step, b_re, b_im, c_re, c_im, d_skip):
    a_re, a_im = a_re.astype(F32), a_im.astype(F32)
    step = jnp.exp(log_step.astype(F32))[..., None]
    mag = jnp.exp(a_re * step)
    abr, abi = mag * jnp.cos(a_im * step), mag * jnp.sin(a_im * step)
    den = a_re * a_re + a_im * a_im
    qr = ((abr - 1.0) * a_re + abi * a_im) / den
    qi = (abi * a_re - (abr - 1.0) * a_im) / den
    br_, bi_ = b_re.astype(F32), b_im.astype(F32)
    bbr = qr[..., None] * br_ - qi[..., None] * bi_
    bbi = qr[..., None] * bi_ + qi[..., None] * br_
    cr, ci = c_re.astype(F32), c_im.astype(F32)

    def power(n):
        n = jnp.asarray(n, F32)[..., None, None, None]
        m = jnp.exp(n * a_re * step)
        return m * jnp.cos(n * a_im * step), m * jnp.sin(n * a_im * step)

    lc = S5_CHUNK
    pr, pi = power(jnp.arange(lc + 1))
    mr = pr[..., None] * bbr - pi[..., None] * bbi
    mi = pr[..., None] * bbi + pi[..., None] * bbr
    hp = lax.Precision.HIGHEST
    kern = (jnp.einsum('dgop,jdgpi->jdgoi', cr, mr[:lc], precision=hp)
            - jnp.einsum('dgop,jdgpi->jdgoi', ci, mi[:lc], precision=hp))
    s_idx = jnp.arange(lc)[:, None]
    r_idx = jnp.arange(lc)[None, :]
    lag = jnp.clip(r_idx - s_idx, 0, lc - 1)
    toep = jnp.where((r_idx >= s_idx)[:, :, None, None, None, None], kern[lag], 0.0)
    toep = toep.transpose(2, 3, 0, 5, 1, 4).reshape(2, S5_GROUPS, lc * S5_GROUP, lc * S5_GROUP)
    rev = jnp.arange(lc - 1, -1, -1)
    bs_re = mr[rev].transpose(1, 2, 0, 4, 3).reshape(2, S5_GROUPS, lc * S5_GROUP, S5_STATE)
    bs_im = mi[rev].transpose(1, 2, 0, 4, 3).reshape(2, S5_GROUPS, lc * S5_GROUP, S5_STATE)
    pr1, pi1 = pr[1:], pi[1:]
    cs_re = cr[None] * pr1[:, :, :, None, :] - ci[None] * pi1[:, :, :, None, :]
    cs_im = -(cr[None] * pi1[:, :, :, None, :] + ci[None] * pr1[:, :, :, None, :])
    cs_re = cs_re.transpose(1, 2, 4, 0, 3).reshape(2, S5_GROUPS, S5_STATE, lc * S5_GROUP)
    cs_im = cs_im.transpose(1, 2, 4, 0, 3).reshape(2, S5_GROUPS, S5_STATE, lc * S5_GROUP)
    half = (jnp.arange(S5_GROUPS) % 2)[None, :, None, None]

    def pad_cols(m):
        z = jnp.zeros_like(m)
        return jnp.where(half == 0, jnp.concatenate([m, z], -1), jnp.concatenate([z, m], -1))

    def pad_rows(m):
        z = jnp.zeros_like(m)
        return jnp.where(half == 0, jnp.concatenate([m, z], -2), jnp.concatenate([z, m], -2))

    def mirror_rows(m):
        shp = m.shape
        return m.reshape(shp[0], lc, shp[1] // lc, shp[2])[:, ::-1].reshape(shp)

    def mirror_cols(m):
        shp = m.shape
        return m.reshape(shp[0], shp[1], lc, shp[2] // lc)[:, :, ::-1].reshape(shp)

    flat = lambda x: x.reshape(x.shape[:-2] + (S5_LANES,))
    shr, shi = power(jnp.asarray([lc, 2 * lc, 4 * lc]))
    cwr, cwi = power(lc * jnp.arange(1, SUBLANES + 1))
    cwr, cwi = flat(cwr).transpose(1, 0, 2), flat(cwi).transpose(1, 0, 2)
    bs_re, bs_im = pad_cols(bs_re), pad_cols(bs_im)
    cs_re, cs_im = pad_rows(cs_re), pad_rows(cs_im)
    return {
        "toep": (toep[0] + mirror_cols(mirror_rows(toep[1]))).astype(BF16),
        "bs": jnp.concatenate([bs_re[0], bs_im[0], mirror_rows(bs_re[1]), mirror_rows(bs_im[1])],
                              axis=-1).astype(BF16),
        "cs": jnp.concatenate([cs_re[0], cs_im[0], mirror_cols(cs_re[1]), mirror_cols(cs_im[1])],
                              axis=-2).astype(BF16),
        "sh_r": flat(shr).transpose(1, 0, 2)[:, :, None, :], "sh_i": flat(shi).transpose(1, 0, 2)[:, :, None, :],
        "cw_r": jnp.stack([cwr[0], cwr[1, ::-1]]), "cw_i": jnp.stack([cwi[0], cwi[1, ::-1]]),
        "dskip": jnp.tile(d_skip.astype(F32).reshape(S5_GROUPS, 1, S5_GROUP), (1, 1, lc)),
    }


def _rec_out_kernel(x_ref, gl_ref, hf_ref, hb_ref, y_ref, wglu_ref, w_ref, gate_ref, o_ref):
    t1 = ((hf_ref[...] + hb_ref[...]) * _gelu(gl_ref[...])).astype(BF16)
    yg = _gelu(y_ref[...])
    t2 = (yg * _sigmoid(jnp.dot(yg.astype(BF16), wglu_ref[...], preferred_element_type=F32))).astype(BF16)
    o = jnp.dot(t1, w_ref[0:LRU_WIDTH, :], preferred_element_type=F32)
    o = o + jnp.dot(t2, w_ref[LRU_WIDTH:, :], preferred_element_type=F32)
    o_ref[...] = x_ref[...] + gate_ref[...] * o


def _rec_out(x, gl, hf, hb, y, w_glu, w_out, gate):
    bsz, t, d = x.shape
    tm = min(ROW_TILE, t)
    row = lambda w: pl.BlockSpec((None, tm, w), lambda b, i: (b, i, 0))
    return pl.pallas_call(
        _rec_out_kernel,
        grid=(bsz, t // tm),
        in_specs=[row(d), row(LRU_WIDTH), row(LRU_WIDTH), row(LRU_WIDTH), row(S5_WIDTH),
                  _resident(w_glu.shape), _resident(w_out.shape),
                  pl.BlockSpec((None, 1, d), lambda b, i: (b, 0, 0))],
        out_specs=row(d),
        out_shape=jax.ShapeDtypeStruct(x.shape, F32),
        compiler_params=_params("parallel", "parallel"),
        name="rec_out",
    )(x, gl, hf, hb, y, w_glu, w_out, gate)


def _rope_tables(n, dim):
    rows = n // GRID_W
    row = jnp.repeat(jnp.arange(rows), GRID_W).astype(F32)
    col = jnp.tile(jnp.arange(GRID_W), rows).astype(F32)
    half = dim // 2
    freqs = ROPE_THETA ** (-jnp.arange(0, half, 2, dtype=F32) / half)

    def angles(pos):
        a = pos[:, None] * freqs[None, :]
        return jnp.concatenate([a, a], axis=-1)

    ang = jnp.concatenate([angles(row), angles(col)], axis=-1)
    reps = LANES // dim
    return jnp.tile(jnp.cos(ang), (1, reps)), jnp.tile(jnp.sin(ang), (1, reps))


def _rotated_columns(w, dim):
    quarter = dim // 4
    idx = jnp.arange(w.shape[1])
    low = (idx % (2 * quarter)) < quarter
    src = jnp.where(low, idx + quarter, idx - quarter)
    return jnp.where(low[None, :], -1.0, 1.0) * w[:, src]


def _att_weights(w_in):
    aq, ak = w_in[:, 0:512], w_in[:, 512:1024]
    bq, bk = w_in[:, 1536:2048], w_in[:, 2048:2176]
    ext = [w_in, _rotated_columns(aq, DA_DIM), _rotated_columns(ak, DA_DIM),
           _rotated_columns(bq, WA_DIM), _rotated_columns(bk, WA_DIM)]
    return jnp.concatenate(ext, axis=1).astype(BF16)


def _block_diag(w):
    eye = jnp.eye(LRU_BLOCKS, dtype=w.dtype)
    return jnp.einsum('nij,nm->nimj', w, eye).reshape(LRU_WIDTH, LRU_WIDTH)


def _vec3(v, bsz):
    return v.reshape(-1, 1, v.shape[-1]) if v.shape[0] == bsz else jnp.broadcast_to(v[None], (bsz, 1, v.shape[-1]))


def kernel(x, c, ctx, c_ctx, ada_w, ada_b, norm1_g, norm2_g, att_w_in, att_w_out, da_lam_q1, da_lam_k1, da_lam_q2, da_lam_k2, da_subln_g, wa_sink, rec_w_in, rec_w_out, lru_conv_w, lru_conv_b, lru_w_a, lru_b_a, lru_w_x, lru_b_x, lru_lam, s5_a_re, s5_a_im, s5_log_step, s5_b_re, s5_b_im, s5_c_re, s5_c_im, s5_d, s5_w_glu, ffn_w_g, ffn_w_u, ffn_conv_w, ffn_w_down, final_g):
    bsz, n_lat, d = x.shape
    n_ctx = ctx.shape[1]
    assert d == D_MODEL and bsz + 1 <= SUBLANES
    assert n_lat % ROW_TILE == 0 and n_ctx % DA_TK == 0 and n_ctx <= ROW_TILE
    x = x.astype(F32)
    ctx = ctx.astype(F32)

    cond = jnp.zeros((SUBLANES, d), F32).at[:bsz].set(c.astype(F32)).at[bsz].set(c_ctx.astype(F32))
    mods = _ada_all(cond, ada_w.astype(F32), ada_b.astype(F32))

    ca, sa = _rope_tables(n_lat, DA_DIM)
    cb, sb = _rope_tables(n_lat, WA_DIM)
    fg = final_g.reshape(1, d).astype(F32)
    ones = jnp.ones((n_ctx, LANES), F32)
    zeros = jnp.zeros((n_ctx, LANES), F32)

    for l in range(DEPTH):
        ctx_out = l < DEPTH - 1
        m = mods[l]
        sh1, sc1, g1, sh2, sc2, g2 = [_vec3(v, bsz) for v in jnp.split(m[:bsz], 6, axis=-1)]
        csh1, csc1, cg1, csh2, csc2, cg2 = [_vec3(v, bsz) for v in jnp.split(m[bsz:bsz + 1], 6, axis=-1)]
        n1 = norm1_g[l].reshape(1, d).astype(F32)
        n2 = norm2_g[l].reshape(1, d).astype(F32)
        j = l // 2
        if l % 2 == 0:
            lam_init = 0.8 - 0.6 * math.exp(-0.3 * l)
            w_ext = _att_weights(att_w_in[j].astype(F32))
            w_out = att_w_out[j].astype(BF16)
            lam = (jnp.exp(jnp.sum(da_lam_q1[j].astype(F32) * da_lam_k1[j].astype(F32)))
                   - jnp.exp(jnp.sum(da_lam_q2[j].astype(F32) * da_lam_k2[j].astype(F32))) + lam_init)
            aql, akl, avl, bql, bkl, bvl = _att_inproj(x, n1, sh1, sc1, w_ext, ca, sa, cb, sb)
            aqc, akc, avc, bqc, bkc, bvc = _att_inproj(ctx, n1, csh1, csc1, w_ext, ones, zeros, ones, zeros)
            k_all = jnp.concatenate([akc, akl], axis=1)
            v_all = jnp.concatenate([avc, avl], axis=-1)
            oal = _diff_attention(aql, k_all, v_all, lam, da_subln_g[j], 1.0 - lam_init)
            obl = _window_attention(bql, bkc, bvc, wa_sink[j], bkl, bvl)
            x = _att_out(x, oal, obl, w_out, g1)
            if ctx_out:
                oac = _diff_attention(aqc, akc, avc, lam, da_subln_g[j], 1.0 - lam_init)
                obc = _window_attention(bqc, bkc, bvc, wa_sink[j])
                ctx = _att_out(ctx, oac, obc, w_out, cg1)
        else:
            w_in = rec_w_in[j].astype(BF16)
            w_f = jnp.concatenate([_block_diag(lru_w_a[j, 0]), _block_diag(lru_w_x[j, 0])], axis=1).astype(BF16)
            w_b = jnp.concatenate([_block_diag(lru_w_a[j, 1]), _block_diag(lru_w_x[j, 1])], axis=1).astype(BF16)
            bias_f = jnp.concatenate([lru_b_a[j, 0], lru_b_x[j, 0]]).reshape(1, -1).astype(F32)
            bias_b = jnp.concatenate([lru_b_a[j, 1], lru_b_x[j, 1]]).reshape(1, -1).astype(F32)
            nsp = (-LRU_C * jax.nn.softplus(-lru_lam[j].astype(F32))).reshape(2, 1, LRU_WIDTH)
            conv_w = lru_conv_w[j].astype(F32)
            conv_b = lru_conv_b[j].reshape(1, -1).astype(F32)
            s5p = _s5_prepare(s5_a_re[j], s5_a_im[j], s5_log_step[j], s5_b_re[j], s5_b_im[j],
                              s5_c_re[j], s5_c_im[j], s5_d[j])
            gc_, rc_, uc_ = _rec_inproj(ctx, n1, csh1, csc1, w_in)
            gl_, rl_, ul_ = _rec_inproj(x, n1, sh1, sc1, w_in)
            h0 = jnp.zeros((bsz, 2, 1, LRU_WIDTH), F32)
            hfc, hbc, hlast = _lru(rc_, conv_w, conv_b, w_f, w_b, bias_f, bias_b, nsp, h0)
            hfl, hbl, _ = _lru(rl_, conv_w, conv_b, w_f, w_b, bias_f, bias_b, nsp, hlast)
            s0 = jnp.zeros((bsz, 2, 1, S5_LANES), F32)
            yc, lre, lim = _s5(uc_, s5p, s0, s0)
            yl, _, _ = _s5(ul_, s5p, lre, lim)
            w_glu = s5_w_glu[j].astype(BF16)
            w_out = rec_w_out[j].astype(BF16)
            x = _rec_out(x, gl_, hfl, hbl, yl, w_glu, w_out, g1)
            if ctx_out:
                ctx = _rec_out(ctx, gc_, hfc, hbc, yc, w_glu, w_out, cg1)
        wg, wu, wd = ffn_w_g[l].astype(BF16), ffn_w_u[l].astype(BF16), ffn_w_down[l].astype(BF16)
        cw = ffn_conv_w[l].astype(F32)
        x = _ffn(x, n2, sh2, sc2, g2, wg, wu, cw, wd, fg, final_norm=l == DEPTH - 1)
        if ctx_out:
            ctx = _ffn(ctx, n2, csh2, csc2, cg2, wg, wu, cw, wd, fg, final_norm=False)
    return x
```

```python
import functools
import math

import jax
import jax.numpy as jnp
from jax import lax
from jax.experimental import pallas as pl
from jax.experimental.pallas import tpu as pltpu

F32 = jnp.float32
BF16 = jnp.bfloat16

D_MODEL = 1024
DEPTH = 4
GRID_W = 64
ROPE_THETA = 10000.0
NORM_EPS = 1e-6
NEG_INF = -1e30
LOG2_E = math.log2(math.e)

DA_HEADS = 8
DA_DIM = 32
WA_HEADS = 8
WA_KV_HEADS = 2
WA_DIM = 64
WA_WINDOW = 128
WA_GROUP = WA_HEADS // WA_KV_HEADS
LRU_WIDTH = 768
LRU_BLOCKS = 8
LRU_BLOCK_DIM = LRU_WIDTH // LRU_BLOCKS
LRU_CONV = 4
LRU_C = 8.0
S5_WIDTH = 256
S5_GROUP = 16
S5_GROUPS = S5_WIDTH // S5_GROUP
S5_STATE = 64
S5_CHUNK = 16
S5_LANES = S5_GROUPS * S5_STATE
D_FF = 2816
FFN_CHUNK = 1408

DA_QK = DA_HEADS * 2 * DA_DIM
DA_V = DA_HEADS * 2 * DA_DIM
WA_Q = WA_HEADS * WA_DIM
WA_KV = WA_KV_HEADS * WA_DIM
ATT_IN = 2 * DA_QK + DA_V + WA_Q + 2 * WA_KV

SUBLANES = 8
LANES = 128
VMEM_LIMIT_BYTES = 56 * 1024 * 1024
ROW_TILE = 512
DA_TQ = 512
DA_TK = 256
DA_AHEAD = 3
DA_BOUND_LIMIT = 48.0
DA_UNROLL = 13
DA_BOUNDED_UNROLL = 65
DA_VROWS = 80


def _params(*sem):
    return pltpu.CompilerParams(dimension_semantics=sem, vmem_limit_bytes=VMEM_LIMIT_BYTES)


def _resident(shape):
    nd = len(shape)
    return pl.BlockSpec(shape, lambda *_: (0,) * nd, pipeline_mode=pl.Buffered(1))


def _sigmoid(x):
    return 1.0 / (1.0 + jnp.exp(-x))


def _gelu(x):
    return 0.5 * x * (1.0 + jnp.tanh(math.sqrt(2.0 / math.pi) * (x + 0.044715 * (x * x * x))))


def _modulated(x, g, shift, scale):
    y = x * lax.rsqrt(jnp.mean(x * x, axis=-1, keepdims=True) + NORM_EPS)
    return (y * g * (1.0 + scale) + shift).astype(BF16)


def _ada_kernel(c_ref, w_ref, b_ref, o_ref):
    c = c_ref[...]
    s = c * _sigmoid(c)
    o_ref[...] = jnp.dot(s, w_ref[...], precision=lax.Precision.HIGHEST,
                         preferred_element_type=F32) + b_ref[...]


def _ada_all(cond, ada_w, ada_b):
    depth, d, n = ada_w.shape
    tn = 1536
    return pl.pallas_call(
        _ada_kernel,
        grid=(depth, n // tn),
        in_specs=[pl.BlockSpec((SUBLANES, d), lambda l, j: (0, 0)),
                  pl.BlockSpec((None, d, tn), lambda l, j: (l, 0, j)),
                  pl.BlockSpec((None, 1, tn), lambda l, j: (l, 0, j))],
        out_specs=pl.BlockSpec((None, SUBLANES, tn), lambda l, j: (l, 0, j)),
        out_shape=jax.ShapeDtypeStruct((depth, SUBLANES, n), F32),
        compiler_params=_params("parallel", "parallel"),
        name="ada_mod",
    )(cond, ada_w, ada_b.reshape(depth, 1, n))


def _att_inproj_kernel(x_ref, g_ref, sh_ref, sc_ref, w_ref, ca_ref, sa_ref, cb_ref, sb_ref,
                       aq_ref, ak_ref, avt_ref, bq_ref, bk_ref, bv_ref):
    h = _modulated(x_ref[...], g_ref[...], sh_ref[...], sc_ref[...])
    z = jnp.dot(h, w_ref[...], preferred_element_type=F32)
    ca = jnp.tile(ca_ref[...], (1, 4))
    sa = jnp.tile(sa_ref[...], (1, 4))
    cb = cb_ref[...]
    sb = sb_ref[...]
    cb4 = jnp.tile(cb, (1, 4))
    sb4 = jnp.tile(sb, (1, 4))
    o = ATT_IN
    aq = z[:, 0:512] * ca + z[:, o:o + 512] * sa
    ak = z[:, 512:1024] * ca + z[:, o + 512:o + 1024] * sa
    bq = z[:, 1536:2048] * cb4 + z[:, o + 1024:o + 1536] * sb4
    bk = z[:, 2048:2176] * cb + z[:, o + 1536:o + 1664] * sb
    aq_ref[...] = (aq * (DA_DIM ** -0.5 * LOG2_E)).astype(BF16)
    ak_ref[...] = ak.astype(BF16)
    for hp in range(4):
        avt_ref[hp] = z[:, 1024 + LANES * hp:1024 + LANES * (hp + 1)].T.astype(BF16)
    bq_ref[...] = (bq * (WA_DIM ** -0.5)).astype(BF16)
    bk_ref[...] = bk.astype(BF16)
    bv_ref[...] = z[:, 2176:2304].astype(BF16)


def _att_inproj(x, g, sh, sc, w_ext, ca, sa, cb, sb):
    bsz, t, d = x.shape
    tm = min(ROW_TILE, t)
    n = w_ext.shape[1]
    row = lambda w: pl.BlockSpec((None, tm, w), lambda b, i: (b, i, 0))
    vec = pl.BlockSpec((None, 1, d), lambda b, i: (b, 0, 0))
    tab = pl.BlockSpec((tm, LANES), lambda b, i: (i, 0))
    rows = lambda w: (row(w), jax.ShapeDtypeStruct((bsz, t, w), BF16))
    outs = [rows(512), rows(512),
            (pl.BlockSpec((None, 4, LANES, tm), lambda b, i: (b, 0, 0, i)),
             jax.ShapeDtypeStruct((bsz, 4, LANES, t), BF16)),
            rows(512), rows(128), rows(128)]
    return pl.pallas_call(
        _att_inproj_kernel,
        grid=(bsz, t // tm),
        in_specs=[row(d), pl.BlockSpec((1, d), lambda b, i: (0, 0)), vec, vec, _resident((d, n)),
                  tab, tab, tab, tab],
        out_specs=[o[0] for o in outs],
        out_shape=[o[1] for o in outs],
        compiler_params=_params("parallel", "parallel"),
        name="att_inproj",
    )(x, g, sh, sc, w_ext, ca, sa, cb, sb)


def _da_kernel(lam_ref, q_ref, k_ref, vt_ref, g_ref, o_ref, acc_ref, s_ref, *, tk, nkt, out_scale):
    tq = q_ref.shape[0]
    qf = q_ref[...].astype(F32).T
    row = lax.broadcasted_iota(jnp.int32, qf.shape, 0)
    ones = jnp.where(lax.broadcasted_iota(jnp.int32, (DA_VROWS - 64, tk), 0) == 0, 1.0, 0.0).astype(BF16)
    w = [jnp.where(jnp.right_shift(row, 5) == s, qf, 0.0).astype(BF16) for s in range(4)]
    acc_ref[...] = jnp.zeros_like(acc_ref)

    def scores(kt, s):
        off = pl.multiple_of(kt * tk, tk)
        st = jnp.dot(k_ref[pl.ds(off, tk), :], w[s], preferred_element_type=F32)
        s_ref[s] = st
        return jnp.max(st, axis=0, keepdims=True)

    def absorb(s, vt, m, mt):
        m_new = jnp.maximum(m, mt)
        p = jnp.exp2(s_ref[s] - m_new).astype(BF16)
        pv = jnp.dot(vt[s // 2], p, preferred_element_type=F32)
        acc_ref[s] = jnp.exp2(m - m_new) * acc_ref[s] + pv
        return m_new

    def body(kt, carry):
        ms, mt0, mt1 = carry
        v2 = vt_ref[:, pl.ds(pl.multiple_of(kt * tk, tk), tk)]
        vt = [jnp.concatenate([v2[64 * h:64 * h + 64, :], ones], axis=0) for h in range(2)]
        nxt = jnp.minimum(kt + 1, nkt - 1)
        mt2 = scores(kt, 2)
        mt3 = scores(kt, 3)
        m0 = absorb(0, vt, ms[0], mt0)
        mt0 = scores(nxt, 0)
        m1 = absorb(1, vt, ms[1], mt1)
        mt1 = scores(nxt, 1)
        m2 = absorb(2, vt, ms[2], mt2)
        m3 = absorb(3, vt, ms[3], mt3)
        return (m0, m1, m2, m3), mt0, mt1

    m_init = tuple(jnp.full((1, tq), NEG_INF, F32) for _ in range(4))
    lax.fori_loop(0, nkt, body, (m_init, scores(0, 0), scores(0, 1)),
                  unroll=DA_UNROLL if nkt % DA_UNROLL == 0 else 1)
    lam = lam_ref[0]
    ys = []
    for h in range(2):
        a0 = acc_ref[2 * h]
        a1 = acc_ref[2 * h + 1]
        o = a0[0:64, :] / a0[64:65, :] - lam * (a1[0:64, :] / a1[64:65, :])
        y = o * lax.rsqrt(jnp.mean(o * o, axis=0, keepdims=True) + NORM_EPS)
        ys.append(y * g_ref[64 * h:64 * h + 64, :] * out_scale)
    o_ref[...] = jnp.concatenate(ys, axis=0).T.astype(o_ref.dtype)


def _da_bounded_kernel(lam_ref, kn_ref, q_ref, k_ref, vt_ref, g_ref, o_ref, acc_ref, l_ref, *, tk, nkt,
                       out_scale):
    b, hp = pl.program_id(0), pl.program_id(1)
    tq = q_ref.shape[0]
    qf = q_ref[...].astype(F32).T
    row = lax.broadcasted_iota(jnp.int32, qf.shape, 0)
    one_col = jnp.where(lax.broadcasted_iota(jnp.int32, (tk, LANES), 1) == 0, 1.0, 0.0).astype(BF16)
    w = []
    for s in range(4):
        qs = jnp.where(jnp.right_shift(row, 5) == s, qf, 0.0)
        bound = jnp.sqrt(jnp.sum(qs * qs, axis=0, keepdims=True)) * kn_ref[b, 4 * hp + s]
        w.append(jnp.concatenate([qs, jnp.where(row == 0, -bound, 0.0)], axis=0).astype(BF16))
    acc_ref[...] = jnp.zeros_like(acc_ref)
    l_ref[...] = jnp.zeros_like(l_ref)
    unroll = DA_BOUNDED_UNROLL if nkt % DA_BOUNDED_UNROLL == 0 else 1
    steps = [(u, s) for u in range(unroll) for s in range(4)]

    def trip(j, carry):
        def keys(u):
            off = pl.multiple_of((j * unroll + u) * tk, tk)
            return jnp.concatenate([k_ref[pl.ds(off, tk), :], one_col], axis=1)

        def values(u, h):
            off = pl.multiple_of((j * unroll + u) * tk, tk)
            return vt_ref[64 * h:64 * h + 64, pl.ds(off, tk)]

        def score(n):
            u, s = steps[n]
            return jnp.dot(keys(u), w[s], preferred_element_type=F32)

        queue = [score(n) for n in range(min(DA_AHEAD, len(steps)))]
        for n, (u, s) in enumerate(steps):
            if n + DA_AHEAD < len(steps):
                queue.append(score(n + DA_AHEAD))
            p = jnp.exp2(queue.pop(0))
            l_ref[s] = l_ref[s] + jnp.sum(p.reshape(tk // SUBLANES, SUBLANES, tq), axis=0)
            acc_ref[s] = acc_ref[s] + jnp.dot(values(u, s // 2), p.astype(BF16), preferred_element_type=F32)
        return carry

    lax.fori_loop(0, nkt // unroll, trip, 0)
    lam = lam_ref[0]
    ys = []
    for h in range(2):
        l0 = jnp.sum(l_ref[2 * h], axis=0, keepdims=True)
        l1 = jnp.sum(l_ref[2 * h + 1], axis=0, keepdims=True)
        o = acc_ref[2 * h] / l0 - lam * (acc_ref[2 * h + 1] / l1)
        y = o * lax.rsqrt(jnp.mean(o * o, axis=0, keepdims=True) + NORM_EPS)
        ys.append(y * g_ref[64 * h:64 * h + 64, :] * out_scale)
    o_ref[...] = jnp.concatenate(ys, axis=0).T.astype(o_ref.dtype)


def _diff_attention(q, k_all, vt_all, lam, subln_g, out_scale):
    bsz, tq_all, _ = q.shape
    tk_all = k_all.shape[1]
    tq = min(DA_TQ, tq_all)
    tk = DA_TK
    nkt = tk_all // tk
    gcol = jnp.tile(subln_g.astype(F32), 2).reshape(LANES, 1)
    lam1 = lam.reshape(1).astype(F32)
    smem = pl.BlockSpec(memory_space=pltpu.SMEM)
    specs = [pl.BlockSpec((None, tq, LANES), lambda b, hp, i: (b, i, hp)),
             pl.BlockSpec((None, tk_all, LANES), lambda b, hp, i: (b, 0, hp)),
             pl.BlockSpec((None, None, LANES, tk_all), lambda b, hp, i: (b, hp, 0, 0)),
             pl.BlockSpec((LANES, 1), lambda b, hp, i: (0, 0))]
    common = dict(grid=(bsz, 4, tq_all // tq),
                  out_specs=pl.BlockSpec((None, tq, LANES), lambda b, hp, i: (b, i, hp)),
                  out_shape=jax.ShapeDtypeStruct((bsz, tq_all, 512), BF16),
                  compiler_params=_params("parallel", "parallel", "arbitrary"))

    def running_max(_):
        return pl.pallas_call(
            functools.partial(_da_kernel, tk=tk, nkt=nkt, out_scale=out_scale),
            in_specs=[smem] + specs,
            scratch_shapes=[pltpu.VMEM((4, DA_VROWS, tq), F32), pltpu.VMEM((4, tk, tq), F32)],
            name="diff_attention", **common)(lam1, q, k_all, vt_all, gcol)

    def bounded(kn):
        return pl.pallas_call(
            functools.partial(_da_bounded_kernel, tk=tk, nkt=nkt, out_scale=out_scale),
            in_specs=[smem, smem] + specs,
            scratch_shapes=[pltpu.VMEM((4, 64, tq), F32), pltpu.VMEM((4, SUBLANES, tq), F32)],
            name="diff_attention_bounded", **common)(lam1, kn, q, k_all, vt_all, gcol)

    norms = lambda z: jnp.sqrt(jnp.max(jnp.sum(jnp.square(z.astype(F32).reshape(bsz, -1, 16, DA_DIM)), -1), 1))
    kn = norms(k_all)
    safe = jnp.max(norms(q) * kn) < DA_BOUND_LIMIT
    return lax.cond(safe, bounded, running_max, kn)


def _wa_kernel(*refs, has_window):
    if has_window:
        (q_ref, kc_ref, vct_ref, sink_ref, kp_ref, km_ref, kn_ref, vp_ref, vm_ref, vn_ref, o_ref) = refs
    else:
        (q_ref, kc_ref, vct_ref, sink_ref, o_ref) = refs
    i = pl.program_id(1)
    nblk = pl.num_programs(1)
    zeros = jnp.zeros((WA_DIM, 4 * LANES), BF16)
    qt = q_ref[...].astype(F32).T
    outs = []
    if has_window:
        vwin = [v_ref[...].astype(F32).T.astype(BF16) for v_ref in (vp_ref, vm_ref, vn_ref)]
        r = lax.broadcasted_iota(jnp.int32, (LANES, 4 * LANES), 0)
        c = jnp.bitwise_and(lax.broadcasted_iota(jnp.int32, (LANES, 4 * LANES), 1), LANES - 1)
        off_p = jnp.where(i > 0, 0, 2 * LANES)
        off_n = jnp.where(i < nblk - 1, 0, 2 * LANES)
        mask_p = r >= c + off_p
        mask_n = r + off_n <= c
    scores = []
    for h in range(WA_KV_HEADS):
        heads = [WA_DIM * (WA_GROUP * h + g) for g in range(WA_GROUP)]
        q = jnp.concatenate([qt[r0:r0 + WA_DIM, :] for r0 in heads], axis=1).astype(BF16)
        w = jnp.concatenate([q, zeros] if h == 0 else [zeros, q], axis=0)
        parts = [jnp.dot(kc_ref[...], w, preferred_element_type=F32)]
        if has_window:
            parts += [jnp.dot(k_ref[...], w, preferred_element_type=F32) for k_ref in (kp_ref, km_ref, kn_ref)]
        scores.append(parts)
    for h in range(WA_KV_HEADS):
        hs = slice(WA_DIM * h, WA_DIM * h + WA_DIM)
        sink = sink_ref[h]
        parts = scores[h]
        vts = [vct_ref[hs, :]]
        if has_window:
            parts = [parts[0], jnp.where(mask_p, parts[1], NEG_INF), parts[2],
                     jnp.where(mask_n, parts[3], NEG_INF)]
            vts += [v[hs, :] for v in vwin]
        m = sink
        for s in parts:
            m = jnp.maximum(m, jnp.max(s, axis=0, keepdims=True))
        den = jnp.exp(sink - m)
        o = jnp.zeros((WA_DIM, 4 * LANES), F32)
        for s, vt in zip(parts, vts):
            p = jnp.exp(s - m)
            den = den + jnp.sum(p, axis=0, keepdims=True)
            o = o + jnp.dot(vt, p.astype(BF16), preferred_element_type=F32)
        o = o / den
        outs += [o[:, LANES * g:LANES * (g + 1)] for g in range(WA_GROUP)]
    o_ref[...] = jnp.concatenate(outs, axis=0).T.astype(o_ref.dtype)


def _window_attention(q, kc, vc, sink, kl=None, vl=None):
    bsz, t, _ = q.shape
    nblk = t // LANES
    n_ctx = kc.shape[1]
    has_window = kl is not None
    vct = vc.transpose(0, 2, 1)
    sink_rows = jnp.repeat(sink.astype(F32).reshape(WA_KV_HEADS, 1, WA_GROUP), LANES, axis=-1)
    qspec = pl.BlockSpec((None, LANES, 512), lambda b, i: (b, i, 0))
    in_specs = [qspec,
                pl.BlockSpec((None, n_ctx, LANES), lambda b, i: (b, 0, 0)),
                pl.BlockSpec((None, LANES, n_ctx), lambda b, i: (b, 0, 0)),
                pl.BlockSpec((WA_KV_HEADS, 1, WA_GROUP * LANES), lambda b, i: (0, 0, 0))]
    args = [q, kc, vct, sink_rows]
    if has_window:
        prev = lambda b, i: (b, jnp.maximum(i - 1, 0), 0)
        here = lambda b, i: (b, i, 0)
        nxt = lambda b, i: (b, jnp.minimum(i + 1, nblk - 1), 0)
        in_specs += 2 * [pl.BlockSpec((None, LANES, LANES), f) for f in (prev, here, nxt)]
        args += [kl, kl, kl, vl, vl, vl]
    return pl.pallas_call(
        functools.partial(_wa_kernel, has_window=has_window),
        grid=(bsz, nblk),
        in_specs=in_specs,
        out_specs=qspec,
        out_shape=jax.ShapeDtypeStruct(q.shape, BF16),
        compiler_params=_params("parallel", "parallel"),
        name="window_attention" if has_window else "context_gqa",
    )(*args)


def _att_out_kernel(x_ref, oa_ref, ob_ref, w_ref, gate_ref, o_ref):
    o = jnp.dot(oa_ref[...], w_ref[0:512, :], preferred_element_type=F32)
    o = o + jnp.dot(ob_ref[...], w_ref[512:1024, :], preferred_element_type=F32)
    o_ref[...] = x_ref[...] + gate_ref[...] * o


def _att_out(x, oa, ob, w_out, gate):
    bsz, t, d = x.shape
    tm = min(ROW_TILE, t)
    row = lambda w: pl.BlockSpec((None, tm, w), lambda b, i: (b, i, 0))
    return pl.pallas_call(
        _att_out_kernel,
        grid=(bsz, t // tm),
        in_specs=[row(d), row(512), row(512), _resident(w_out.shape),
                  pl.BlockSpec((None, 1, d), lambda b, i: (b, 0, 0))],
        out_specs=row(d),
        out_shape=jax.ShapeDtypeStruct(x.shape, F32),
        compiler_params=_params("parallel", "parallel"),
        name="att_out",
    )(x, oa, ob, w_out, gate)


def _ffn_kernel(xm_ref, xp_ref, xn_ref, g_ref, sh_ref, sc_ref, gate_ref, wg_ref, wu_ref, cw_ref,
                wd_ref, og_ref, o_ref, gs_ref, *, final_norm):
    i = pl.program_id(1)
    first = i == 0
    last = i == pl.num_programs(1) - 1
    tm = xm_ref.shape[0]
    xm = xm_ref[...]
    g, sh, sc = g_ref[...], sh_ref[...], sc_ref[...]
    hm = _modulated(xm, g, sh, sc)
    hh = _modulated(jnp.concatenate([xp_ref[...], xn_ref[...]], axis=0), g, sh, sc)
    acc = jnp.zeros((tm, xm.shape[1]), F32)
    n_chunks = D_FF // FFN_CHUNK

    def up(c):
        cs = slice(c * FFN_CHUNK, (c + 1) * FFN_CHUNK)
        return (jnp.dot(hm, wg_ref[:, cs], preferred_element_type=F32),
                jnp.dot(hh, wg_ref[:, cs], preferred_element_type=F32),
                jnp.dot(hm, wu_ref[:, cs], preferred_element_type=F32))

    nxt = up(0)
    for c in range(n_chunks):
        cs = slice(c * FFN_CHUNK, (c + 1) * FFN_CHUNK)
        gm, gh, u = nxt
        if c + 1 < n_chunks:
            nxt = up(c + 1)
        gs_ref[0:SUBLANES, :] = jnp.where(first, 0.0, gh[0:SUBLANES, :])
        gs_ref[SUBLANES:SUBLANES + tm, :] = gm
        gs_ref[SUBLANES + tm:2 * SUBLANES + tm, :] = jnp.where(last, 0.0, gh[SUBLANES:, :])
        cw = cw_ref[:, cs]
        gc = (cw[0:1, :] * gs_ref[SUBLANES - 1:SUBLANES - 1 + tm, :] + cw[1:2, :] * gm
              + cw[2:3, :] * gs_ref[SUBLANES + 1:SUBLANES + 1 + tm, :])
        act = (gc * _sigmoid(gc) * u).astype(BF16)
        acc = acc + jnp.dot(act, wd_ref[cs, :], preferred_element_type=F32)
    y = xm + gate_ref[...] * acc
    if final_norm:
        y = y * lax.rsqrt(jnp.mean(y * y, axis=-1, keepdims=True) + NORM_EPS) * og_ref[...]
    o_ref[...] = y


def _ffn(x, g, sh, sc, gate, w_g, w_u, conv_w, w_d, out_g, final_norm):
    bsz, t, d = x.shape
    tm = min(ROW_TILE, t)
    nb = tm // SUBLANES
    last_blk = t // SUBLANES - 1
    vec = pl.BlockSpec((None, 1, d), lambda b, i: (b, 0, 0))
    return pl.pallas_call(
        functools.partial(_ffn_kernel, final_norm=final_norm),
        grid=(bsz, t // tm),
        in_specs=[pl.BlockSpec((None, tm, d), lambda b, i: (b, i, 0)),
                  pl.BlockSpec((None, SUBLANES, d), lambda b, i: (b, jnp.maximum(i * nb - 1, 0), 0)),
                  pl.BlockSpec((None, SUBLANES, d),
                               lambda b, i: (b, jnp.minimum((i + 1) * nb, last_blk), 0)),
                  pl.BlockSpec((1, d), lambda b, i: (0, 0)), vec, vec, vec,
                  _resident(w_g.shape), _resident(w_u.shape), _resident(conv_w.shape),
                  _resident(w_d.shape), pl.BlockSpec((1, d), lambda b, i: (0, 0))],
        out_specs=pl.BlockSpec((None, tm, d), lambda b, i: (b, i, 0)),
        out_shape=jax.ShapeDtypeStruct(x.shape, F32),
        scratch_shapes=[pltpu.VMEM((tm + 2 * SUBLANES, FFN_CHUNK), F32)],
        compiler_params=_params("parallel", "parallel"),
        name="conv_ffn",
    )(x, x, x, g, sh, sc, gate, w_g, w_u, conv_w, w_d, out_g)


def _rec_inproj_kernel(x_ref, g_ref, sh_ref, sc_ref, w_ref, gl_ref, rl_ref, ul_ref):
    h = _modulated(x_ref[...], g_ref[...], sh_ref[...], sc_ref[...])
    z = jnp.dot(h, w_ref[...], preferred_element_type=F32)
    gl_ref[...] = z[:, 0:LRU_WIDTH]
    rl_ref[...] = z[:, LRU_WIDTH:2 * LRU_WIDTH]
    ul_ref[...] = z[:, 2 * LRU_WIDTH:]


def _rec_inproj(x, g, sh, sc, w_in):
    bsz, t, d = x.shape
    tm = min(ROW_TILE, t)
    row = lambda w: pl.BlockSpec((None, tm, w), lambda b, i: (b, i, 0))
    vec = pl.BlockSpec((None, 1, d), lambda b, i: (b, 0, 0))
    outs = [LRU_WIDTH, LRU_WIDTH, S5_WIDTH]
    return pl.pallas_call(
        _rec_inproj_kernel,
        grid=(bsz, t // tm),
        in_specs=[row(d), pl.BlockSpec((1, d), lambda b, i: (0, 0)), vec, vec, _resident(w_in.shape)],
        out_specs=[row(w) for w in outs],
        out_shape=[jax.ShapeDtypeStruct((bsz, t, w), F32) for w in outs],
        compiler_params=_params("parallel", "parallel"),
        name="rec_inproj",
    )(x, g, sh, sc, w_in)


def _lru_direction(xm_ref, xp_ref, xn_ref, at_start, at_end, cw_ref, cb_ref, w_ref, bias_ref, nsp,
                   out_ref, carry_ref, xs_ref, a_ref, b_ref, reverse):
    tm = xm_ref.shape[0]
    xs_ref[0:SUBLANES, :] = jnp.where(at_start, 0.0, xp_ref[...])
    xs_ref[SUBLANES:SUBLANES + tm, :] = xm_ref[...]
    xs_ref[SUBLANES + tm:2 * SUBLANES + tm, :] = jnp.where(at_end, 0.0, xn_ref[...])
    cw = cw_ref[...]
    rc = cb_ref[...] + cw[2:3, :] * xm_ref[...]
    for k in (0, 1, 3):
        rc = rc + cw[k:k + 1, :] * xs_ref[SUBLANES - 2 + k:SUBLANES - 2 + k + tm, :]
    gz = jnp.dot(rc.astype(BF16), w_ref[...], preferred_element_type=F32) + bias_ref[...]
    r = _sigmoid(gz[:, 0:LRU_WIDTH])
    ig = _sigmoid(gz[:, LRU_WIDTH:])
    a = jnp.exp(nsp * r)
    b = jnp.sqrt(1.0 - a * a) * (ig * rc)
    ngroups = tm // SUBLANES
    a = a.reshape(ngroups, SUBLANES, LRU_WIDTH)
    b = b.reshape(ngroups, SUBLANES, LRU_WIDTH)
    row8 = lax.broadcasted_iota(jnp.int32, a.shape, 1)
    for s in (1, 2, 4):
        if reverse:
            keep = row8 <= SUBLANES - 1 - s
            shift = SUBLANES - s
        else:
            keep = row8 >= s
            shift = s
        a_sh = jnp.where(keep, pltpu.roll(a, shift, 1), 1.0)
        b_sh = jnp.where(keep, pltpu.roll(b, shift, 1), 0.0)
        b = b + a * b_sh
        a = a * a_sh
    a_ref[...] = a.reshape(tm, LRU_WIDTH)
    b_ref[...] = b.reshape(tm, LRU_WIDTH)
    edge = 0 if reverse else SUBLANES - 1

    def body(j, carry):
        grp = (ngroups - 1 - j) if reverse else j
        off = pl.multiple_of(grp * SUBLANES, SUBLANES)
        h = b_ref[pl.ds(off, SUBLANES), :] + a_ref[pl.ds(off, SUBLANES), :] * carry
        out_ref[pl.ds(off, SUBLANES), :] = h
        return h[edge:edge + 1, :]

    carry_ref[...] = lax.fori_loop(0, ngroups, body, carry_ref[...])


def _lru_kernel(fm_ref, fp_ref, fn_ref, bm_ref, bp_ref, bn_ref, cw_ref, cb_ref, wf_ref, wb_ref,
                biasf_ref, biasb_ref, nsp_ref, h0_ref, hf_ref, hb_ref, hlast_ref,
                carry_f, carry_b, xs_ref, a_ref, b_ref):
    i = pl.program_id(1)
    first = i == 0
    last = i == pl.num_programs(1) - 1

    @pl.when(first)
    def _():
        carry_f[...] = h0_ref[0]
        carry_b[...] = h0_ref[1]

    _lru_direction(fm_ref, fp_ref, fn_ref, first, last, cw_ref, cb_ref, wf_ref, biasf_ref, nsp_ref[0],
                   hf_ref, carry_f, xs_ref, a_ref, b_ref, reverse=False)
    _lru_direction(bm_ref, bp_ref, bn_ref, last, first, cw_ref, cb_ref, wb_ref, biasb_ref, nsp_ref[1],
                   hb_ref, carry_b, xs_ref, a_ref, b_ref, reverse=True)
    hlast_ref[0] = carry_f[...]
    hlast_ref[1] = carry_b[...]


def _lru(rl, conv_w, conv_b, w_f, w_b, bias_f, bias_b, nsp, h0):
    bsz, t, c = rl.shape
    tm = min(ROW_TILE, t)
    nt = t // tm
    nb = tm // SUBLANES
    last_blk = t // SUBLANES - 1
    fwd = lambda b, i: (b, i, 0)
    bwd = lambda b, i: (b, nt - 1 - i, 0)
    fprev = lambda b, i: (b, jnp.maximum(i * nb - 1, 0), 0)
    fnext = lambda b, i: (b, jnp.minimum((i + 1) * nb, last_blk), 0)
    bprev = lambda b, i: (b, jnp.maximum((nt - 1 - i) * nb - 1, 0), 0)
    bnext = lambda b, i: (b, jnp.minimum((nt - i) * nb, last_blk), 0)
    main = lambda f: pl.BlockSpec((None, tm, c), f)
    halo = lambda f: pl.BlockSpec((None, SUBLANES, c), f)
    state = pl.BlockSpec((None, 2, 1, c), lambda b, i: (b, 0, 0, 0))
    return pl.pallas_call(
        _lru_kernel,
        grid=(bsz, nt),
        in_specs=[main(fwd), halo(fprev), halo(fnext), main(bwd), halo(bprev), halo(bnext),
                  _resident(conv_w.shape), _resident(conv_b.shape), _resident(w_f.shape),
                  _resident(w_b.shape), _resident(bias_f.shape), _resident(bias_b.shape),
                  _resident(nsp.shape), state],
        out_specs=[main(fwd), main(bwd), state],
        out_shape=[jax.ShapeDtypeStruct(rl.shape, F32), jax.ShapeDtypeStruct(rl.shape, F32),
                   jax.ShapeDtypeStruct((bsz, 2, 1, c), F32)],
        scratch_shapes=[pltpu.VMEM((1, c), F32), pltpu.VMEM((1, c), F32),
                        pltpu.VMEM((tm + 2 * SUBLANES, c), F32),
                        pltpu.VMEM((tm, c), F32), pltpu.VMEM((tm, c), F32)],
        compiler_params=_params("parallel", "arbitrary"),
        name="rglru",
    )(rl, rl, rl, rl, rl, rl, conv_w, conv_b, w_f, w_b, bias_f, bias_b, nsp, h0)


def _s5_in_kernel(u_ref, t_ref, bs_ref, yi_ref, xre_ref, xim_ref):
    x = jnp.zeros((u_ref.shape[1], 4 * LANES), F32)
    for q in range(2):
        ub = u_ref[q].astype(BF16)
        yi_ref[q] = jnp.dot(ub, t_ref[q], preferred_element_type=F32)
        x = x + jnp.dot(ub, bs_ref[q], preferred_element_type=F32)
    for d in range(2):
        xre_ref[d] = x[:, 2 * d * LANES:(2 * d + 1) * LANES]
        xim_ref[d] = x[:, (2 * d + 1) * LANES:(2 * d + 2) * LANES]


def _s5_scan_kernel(xre_ref, xim_ref, h0re_ref, h0im_ref, shr_ref, shi_ref, cwr_ref, cwi_ref,
                    sre_ref, sim_ref, lre_ref, lim_ref, pre_ref, pim_ref, *, reverse):
    nc, n = xre_ref.shape
    xr = xre_ref[...]
    xi = xim_ref[...]
    row8 = jnp.bitwise_and(lax.broadcasted_iota(jnp.int32, (nc, n), 0), SUBLANES - 1)
    for k, s in enumerate((1, 2, 4)):
        keep = (row8 <= SUBLANES - 1 - s) if reverse else (row8 >= s)
        shift = nc - s if reverse else s
        xr_sh = jnp.where(keep, pltpu.roll(xr, shift, 0), 0.0)
        xi_sh = jnp.where(keep, pltpu.roll(xi, shift, 0), 0.0)
        mr = shr_ref[k]
        mi = shi_ref[k]
        xr, xi = xr + (mr * xr_sh - mi * xi_sh), xi + (mr * xi_sh + mi * xr_sh)
    pre_ref[...] = xr
    pim_ref[...] = xi
    cwr = cwr_ref[...]
    cwi = cwi_ref[...]
    row = lax.broadcasted_iota(jnp.int32, (SUBLANES, n), 0)
    ngroups = nc // SUBLANES
    edge = 0 if reverse else SUBLANES - 1
    entry = SUBLANES - 1 - edge
    step = SUBLANES - 1 if reverse else 1

    def body(j, carry):
        cr, ci = carry
        grp = (ngroups - 1 - j) if reverse else j
        off = pl.multiple_of(grp * SUBLANES, SUBLANES)
        sr = pre_ref[pl.ds(off, SUBLANES), :] + (cwr * cr - cwi * ci)
        si = pim_ref[pl.ds(off, SUBLANES), :] + (cwr * ci + cwi * cr)
        pre_ref[pl.ds(off, SUBLANES), :] = jnp.where(row == entry, cr, pltpu.roll(sr, step, 0))
        pim_ref[pl.ds(off, SUBLANES), :] = jnp.where(row == entry, ci, pltpu.roll(si, step, 0))
        return sr[edge:edge + 1, :], si[edge:edge + 1, :]

    cr, ci = lax.fori_loop(0, ngroups, body, (h0re_ref[...], h0im_ref[...]))
    lre_ref[...] = cr
    lim_ref[...] = ci
    sre_ref[...] = pre_ref[...].astype(BF16)
    sim_ref[...] = pim_ref[...].astype(BF16)


def _s5_out_kernel(yi_ref, u_ref, sre_ref, sim_ref, cs_ref, d_ref, y_ref):
    s = jnp.concatenate([sre_ref[0], sim_ref[0], sre_ref[1], sim_ref[1]], axis=-1)
    for q in range(2):
        y = yi_ref[q] + d_ref[q] * u_ref[q]
        y_ref[q] = y + jnp.dot(s, cs_ref[q], preferred_element_type=F32)


def _s5(u, prm, h0re, h0im):
    bsz, t, _ = u.shape
    nc = t // S5_CHUNK
    ug = u.reshape(bsz, nc, S5_CHUNK, S5_GROUPS, S5_GROUP).transpose(0, 3, 1, 2, 4)
    ug = ug.reshape(bsz, S5_GROUPS, nc, 256)
    npair = S5_GROUPS // 2
    pair = pl.BlockSpec((None, 2, nc, 256), lambda b, p: (b, p, 0, 0))
    wpair = lambda r, w: pl.BlockSpec((2, r, w), lambda b, p: (p, 0, 0))
    lanes = pl.BlockSpec((None, 2, nc, LANES), lambda b, p: (b, 0, 0, p))
    state_shape = (bsz, 2, nc, S5_LANES)
    yi, xre, xim = pl.pallas_call(
        _s5_in_kernel,
        grid=(bsz, npair),
        in_specs=[pair, wpair(256, 256), wpair(256, 4 * LANES)],
        out_specs=[pair, lanes, lanes],
        out_shape=[jax.ShapeDtypeStruct(ug.shape, F32), jax.ShapeDtypeStruct(state_shape, F32),
                   jax.ShapeDtypeStruct(state_shape, F32)],
        compiler_params=_params("parallel", "parallel"),
        name="s5_in",
    )(ug, prm["toep"], prm["bs"])

    states = []
    for d in range(2):
        full = pl.BlockSpec((None, None, nc, S5_LANES), lambda b, d=d: (b, d, 0, 0))
        one = pl.BlockSpec((None, None, 1, S5_LANES), lambda b, d=d: (b, d, 0, 0))
        tab = pl.BlockSpec((None, 3, 1, S5_LANES), lambda b, d=d: (d, 0, 0, 0))
        tab8 = pl.BlockSpec((None, SUBLANES, S5_LANES), lambda b, d=d: (d, 0, 0))
        plain = pl.BlockSpec((None, nc, S5_LANES), lambda b: (b, 0, 0))
        plain1 = pl.BlockSpec((None, 1, S5_LANES), lambda b: (b, 0, 0))
        states.append(pl.pallas_call(
            functools.partial(_s5_scan_kernel, reverse=bool(d)),
            grid=(bsz,),
            in_specs=[full, full, one, one, tab, tab, tab8, tab8],
            out_specs=[plain, plain, plain1, plain1],
            out_shape=[jax.ShapeDtypeStruct((bsz, nc, S5_LANES), BF16),
                       jax.ShapeDtypeStruct((bsz, nc, S5_LANES), BF16),
                       jax.ShapeDtypeStruct((bsz, 1, S5_LANES), F32),
                       jax.ShapeDtypeStruct((bsz, 1, S5_LANES), F32)],
            scratch_shapes=[pltpu.VMEM((nc, S5_LANES), F32), pltpu.VMEM((nc, S5_LANES), F32)],
            compiler_params=_params("parallel"),
            name="s5_scan_bwd" if d else "s5_scan_fwd",
        )(xre, xim, h0re, h0im, prm["sh_r"], prm["sh_i"], prm["cw_r"], prm["cw_i"]))
    sre, sim, lre, lim = [jnp.stack([states[0][k], states[1][k]], axis=1) for k in range(4)]

    y = pl.pallas_call(
        _s5_out_kernel,
        grid=(bsz, npair),
        in_specs=[pair, pair, lanes, lanes, wpair(4 * LANES, 256), wpair(1, 256)],
        out_specs=pair,
        out_shape=jax.ShapeDtypeStruct(ug.shape, F32),
        compiler_params=_params("parallel", "parallel"),
        name="s5_out",
    )(yi, ug, sre, sim, prm["cs"], prm["dskip"])
    y = y.reshape(bsz, S5_GROUPS, nc, S5_CHUNK, S5_GROUP).transpose(0, 2, 3, 1, 4)
    return y.reshape(bsz, t, S5_WIDTH), lre, lim


def _s5_prepare(a_re, a_im, log_step, b_re, b_im, c_re, c_im, d_skip):
    a_re, a_im = a_re.astype(F32), a_im.astype(F32)
    step = jnp.exp(log_step.astype(F32))[..., None]
    mag = jnp.exp(a_re * step)
    abr, abi = mag * jnp.cos(a_im * step), mag * jnp.sin(a_im * step)
    den = a_re * a_re + a_im * a_im
    qr = ((abr - 1.0) * a_re + abi * a_im) / den
    qi = (abi * a_re - (abr - 1.0) * a_im) / den
    br_, bi_ = b_re.astype(F32), b_im.astype(F32)
    bbr = qr[..., None] * br_ - qi[..., None] * bi_
    bbi = qr[..., None] * bi_ + qi[..., None] * br_
    cr, ci = c_re.astype(F32), c_im.astype(F32)

    def power(n):
        n = jnp.asarray(n, F32)[..., None, None, None]
        m = jnp.exp(n * a_re * step)
        return m * jnp.cos(n * a_im * step), m * jnp.sin(n * a_im * step)

    lc = S5_CHUNK
    pr, pi = power(jnp.arange(lc + 1))
    mr = pr[..., None] * bbr - pi[..., None] * bbi
    mi = pr[..., None] * bbi + pi[..., None] * bbr
    hp = lax.Precision.HIGHEST
    kern = (jnp.einsum('dgop,jdgpi->jdgoi', cr, mr[:lc], precision=hp)
            - jnp.einsum('dgop,jdgpi->jdgoi', ci, mi[:lc], precision=hp))
    s_idx = jnp.arange(lc)[:, None]
    r_idx = jnp.arange(lc)[None, :]
    lag = jnp.clip(r_idx - s_idx, 0, lc - 1)
    toep = jnp.where((r_idx >= s_idx)[:, :, None, None, None, None], kern[lag], 0.0)
    toep = toep.transpose(2, 3, 0, 5, 1, 4).reshape(2, S5_GROUPS, lc * S5_GROUP, lc * S5_GROUP)
    rev = jnp.arange(lc - 1, -1, -1)
    bs_re = mr[rev].transpose(1, 2, 0, 4, 3).reshape(2, S5_GROUPS, lc * S5_GROUP, S5_STATE)
    bs_im = mi[rev].transpose(1, 2, 0, 4, 3).reshape(2, S5_GROUPS, lc * S5_GROUP, S5_STATE)
    pr1, pi1 = pr[1:], pi[1:]
    cs_re = cr[None] * pr1[:, :, :, None, :] - ci[None] * pi1[:, :, :, None, :]
    cs_im = -(cr[None] * pi1[:, :, :, None, :] + ci[None] * pr1[:, :, :, None, :])
    cs_re = cs_re.transpose(1, 2, 4, 0, 3).reshape(2, S5_GROUPS, S5_STATE, lc * S5_GROUP)
    cs_im = cs_im.transpose(1, 2, 4, 0, 3).reshape(2, S5_GROUPS, S5_STATE, lc * S5_GROUP)
    half = (jnp.arange(S5_GROUPS) % 2)[None, :, None, None]

    def pad_cols(m):
        z = jnp.zeros_like(m)
        return jnp.where(half == 0, jnp.concatenate([m, z], -1), jnp.concatenate([z, m], -1))

    def pad_rows(m):
        z = jnp.zeros_like(m)
        return jnp.where(half == 0, jnp.concatenate([m, z], -2), jnp.concatenate([z, m], -2))

    def mirror_rows(m):
        shp = m.shape
        return m.reshape(shp[0], lc, shp[1] // lc, shp[2])[:, ::-1].reshape(shp)

    def mirror_cols(m):
        shp = m.shape
        return m.reshape(shp[0], shp[1], lc, shp[2] // lc)[:, :, ::-1].reshape(shp)

    flat = lambda x: x.reshape(x.shape[:-2] + (S5_LANES,))
    shr, shi = power(jnp.asarray([lc, 2 * lc, 4 * lc]))
    cwr, cwi = power(lc * jnp.arange(1, SUBLANES + 1))
    cwr, cwi = flat(cwr).transpose(1, 0, 2), flat(cwi).transpose(1, 0, 2)
    bs_re, bs_im = pad_cols(bs_re), pad_cols(bs_im)
    cs_re, cs_im = pad_rows(cs_re), pad_rows(cs_im)
    return {
        "toep": (toep[0] + mirror_cols(mirror_rows(toep[1]))).astype(BF16),
        "bs": jnp.concatenate([bs_re[0], bs_im[0], mirror_rows(bs_re[1]), mirror_rows(bs_im[1])],
                              axis=-1).astype(BF16),
        "cs": jnp.concatenate([cs_re[0], cs_im[0], mirror_cols(cs_re[1]), mirror_cols(cs_im[1])],
                              axis=-2).astype(BF16),
        "sh_r": flat(shr).transpose(1, 0, 2)[:, :, None, :], "sh_i": flat(shi).transpose(1, 0, 2)[:, :, None, :],
        "cw_r": jnp.stack([cwr[0], cwr[1, ::-1]]), "cw_i": jnp.stack([cwi[0], cwi[1, ::-1]]),
        "dskip": jnp.tile(d_skip.astype(F32).reshape(S5_GROUPS, 1, S5_GROUP), (1, 1, lc)),
    }


def _rec_out_kernel(x_ref, gl_ref, hf_ref, hb_ref, y_ref, wglu_ref, w_ref, gate_ref, o_ref):
    t1 = ((hf_ref[...] + hb_ref[...]) * _gelu(gl_ref[...])).astype(BF16)
    yg = _gelu(y_ref[...])
    t2 = (yg * _sigmoid(jnp.dot(yg.astype(BF16), wglu_ref[...], preferred_element_type=F32))).astype(BF16)
    o = jnp.dot(t1, w_ref[0:LRU_WIDTH, :], preferred_element_type=F32)
    o = o + jnp.dot(t2, w_ref[LRU_WIDTH:, :], preferred_element_type=F32)
    o_ref[...] = x_ref[...] + gate_ref[...] * o


def _rec_out(x, gl, hf, hb, y, w_glu, w_out, gate):
    bsz, t, d = x.shape
    tm = min(ROW_TILE, t)
    row = lambda w: pl.BlockSpec((None, tm, w), lambda b, i: (b, i, 0))
    return pl.pallas_call(
        _rec_out_kernel,
        grid=(bsz, t // tm),
        in_specs=[row(d), row(LRU_WIDTH), row(LRU_WIDTH), row(LRU_WIDTH), row(S5_WIDTH),
                  _resident(w_glu.shape), _resident(w_out.shape),
                  pl.BlockSpec((None, 1, d), lambda b, i: (b, 0, 0))],
        out_specs=row(d),
        out_shape=jax.ShapeDtypeStruct(x.shape, F32),
        compiler_params=_params("parallel", "parallel"),
        name="rec_out",
    )(x, gl, hf, hb, y, w_glu, w_out, gate)


def _rope_tables(n, dim):
    rows = n // GRID_W
    row = jnp.repeat(jnp.arange(rows), GRID_W).astype(F32)
    col = jnp.tile(jnp.arange(GRID_W), rows).astype(F32)
    half = dim // 2
    freqs = ROPE_THETA ** (-jnp.arange(0, half, 2, dtype=F32) / half)

    def angles(pos):
        a = pos[:, None] * freqs[None, :]
        return jnp.concatenate([a, a], axis=-1)

    ang = jnp.concatenate([angles(row), angles(col)], axis=-1)
    reps = LANES // dim
    return jnp.tile(jnp.cos(ang), (1, reps)), jnp.tile(jnp.sin(ang), (1, reps))


def _rotated_columns(w, dim):
    quarter = dim // 4
    idx = jnp.arange(w.shape[1])
    low = (idx % (2 * quarter)) < quarter
    src = jnp.where(low, idx + quarter, idx - quarter)
    return jnp.where(low[None, :], -1.0, 1.0) * w[:, src]


def _att_weights(w_in):
    aq, ak = w_in[:, 0:512], w_in[:, 512:1024]
    bq, bk = w_in[:, 1536:2048], w_in[:, 2048:2176]
    ext = [w_in, _rotated_columns(aq, DA_DIM), _rotated_columns(ak, DA_DIM),
           _rotated_columns(bq, WA_DIM), _rotated_columns(bk, WA_DIM)]
    return jnp.concatenate(ext, axis=1).astype(BF16)


def _block_diag(w):
    eye = jnp.eye(LRU_BLOCKS, dtype=w.dtype)
    return jnp.einsum('nij,nm->nimj', w, eye).reshape(LRU_WIDTH, LRU_WIDTH)


def _vec3(v, bsz):
    return v.reshape(-1, 1, v.shape[-1]) if v.shape[0] == bsz else jnp.broadcast_to(v[None], (bsz, 1, v.shape[-1]))


def kernel(x, c, ctx, c_ctx, ada_w, ada_b, norm1_g, norm2_g, att_w_in, att_w_out, da_lam_q1, da_lam_k1, da_lam_q2, da_lam_k2, da_subln_g, wa_sink, rec_w_in, rec_w_out, lru_conv_w, lru_conv_b, lru_w_a, lru_b_a, lru_w_x, lru_b_x, lru_lam, s5_a_re, s5_a_im, s5_log_step, s5_b_re, s5_b_im, s5_c_re, s5_c_im, s5_d, s5_w_glu, ffn_w_g, ffn_w_u, ffn_conv_w, ffn_w_down, final_g):
    bsz, n_lat, d = x.shape
    n_ctx = ctx.shape[1]
    assert d == D_MODEL and bsz + 1 <= SUBLANES
    assert n_lat % ROW_TILE == 0 and n_ctx % DA_TK == 0 and n_ctx <= ROW_TILE
    x = x.astype(F32)
    ctx = ctx.astype(F32)

    cond = jnp.zeros((SUBLANES, d), F32).at[:bsz].set(c.astype(F32)).at[bsz].set(c_ctx.astype(F32))
    mods = _ada_all(cond, ada_w.astype(F32), ada_b.astype(F32))

    ca, sa = _rope_tables(n_lat, DA_DIM)
    cb, sb = _rope_tables(n_lat, WA_DIM)
    fg = final_g.reshape(1, d).astype(F32)
    ones = jnp.ones((n_ctx, LANES), F32)
    zeros = jnp.zeros((n_ctx, LANES), F32)

    for l in range(DEPTH):
        ctx_out = l < DEPTH - 1
        m = mods[l]
        sh1, sc1, g1, sh2, sc2, g2 = [_vec3(v, bsz) for v in jnp.split(m[:bsz], 6, axis=-1)]
        csh1, csc1, cg1, csh2, csc2, cg2 = [_vec3(v, bsz) for v in jnp.split(m[bsz:bsz + 1], 6, axis=-1)]
        n1 = norm1_g[l].reshape(1, d).astype(F32)
        n2 = norm2_g[l].reshape(1, d).astype(F32)
        j = l // 2
        if l % 2 == 0:
            lam_init = 0.8 - 0.6 * math.exp(-0.3 * l)
            w_ext = _att_weights(att_w_in[j].astype(F32))
            w_out = att_w_out[j].astype(BF16)
            lam = (jnp.exp(jnp.sum(da_lam_q1[j].astype(F32) * da_lam_k1[j].astype(F32)))
                   - jnp.exp(jnp.sum(da_lam_q2[j].astype(F32) * da_lam_k2[j].astype(F32))) + lam_init)
            aql, akl, avl, bql, bkl, bvl = _att_inproj(x, n1, sh1, sc1, w_ext, ca, sa, cb, sb)
            aqc, akc, avc, bqc, bkc, bvc = _att_inproj(ctx, n1, csh1, csc1, w_ext, ones, zeros, ones, zeros)
            k_all = jnp.concatenate([akc, akl], axis=1)
            v_all = jnp.concatenate([avc, avl], axis=-1)
            oal = _diff_attention(aql, k_all, v_all, lam, da_subln_g[j], 1.0 - lam_init)
            obl = _window_attention(bql, bkc, bvc, wa_sink[j], bkl, bvl)
            x = _att_out(x, oal, obl, w_out, g1)
            if ctx_out:
                oac = _diff_attention(aqc, akc, avc, lam, da_subln_g[j], 1.0 - lam_init)
                obc = _window_attention(bqc, bkc, bvc, wa_sink[j])
                ctx = _att_out(ctx, oac, obc, w_out, cg1)
        else:
            w_in = rec_w_in[j].astype(BF16)
            w_f = jnp.concatenate([_block_diag(lru_w_a[j, 0]), _block_diag(lru_w_x[j, 0])], axis=1).astype(BF16)
            w_b = jnp.concatenate([_block_diag(lru_w_a[j, 1]), _block_diag(lru_w_x[j, 1])], axis=1).astype(BF16)
            bias_f = jnp.concatenate([lru_b_a[j, 0], lru_b_x[j, 0]]).reshape(1, -1).astype(F32)
            bias_b = jnp.concatenate([lru_b_a[j, 1], lru_b_x[j, 1]]).reshape(1, -1).astype(F32)
            nsp = (-LRU_C * jax.nn.softplus(-lru_lam[j].astype(F32))).reshape(2, 1, LRU_WIDTH)
            conv_w = lru_conv_w[j].astype(F32)
            conv_b = lru_conv_b[j].reshape(1, -1).astype(F32)
            s5p = _s5_prepare(s5_a_re[j], s5_a_im[j], s5_log_step[j], s5_b_re[j], s5_b_im[j],
                              s5_c_re[j], s5_c_im[j], s5_d[j])
            gc_, rc_, uc_ = _rec_inproj(ctx, n1, csh1, csc1, w_in)
            gl_, rl_, ul_ = _rec_inproj(x, n1, sh1, sc1, w_in)
            h0 = jnp.zeros((bsz, 2, 1, LRU_WIDTH), F32)
            hfc, hbc, hlast = _lru(rc_, conv_w, conv_b, w_f, w_b, bias_f, bias_b, nsp, h0)
            hfl, hbl, _ = _lru(rl_, conv_w, conv_b, w_f, w_b, bias_f, bias_b, nsp, hlast)
            s0 = jnp.zeros((bsz, 2, 1, S5_LANES), F32)
            yc, lre, lim = _s5(uc_, s5p, s0, s0)
            yl, _, _ = _s5(ul_, s5p, lre, lim)
            w_glu = s5_w_glu[j].astype(BF16)
            w_out = rec_w_out[j].astype(BF16)
            x = _rec_out(x, gl_, hfl, hbl, yl, w_glu, w_out, g1)
            if ctx_out:
                ctx = _rec_out(ctx, gc_, hfc, hbc, yc, w_glu, w_out, cg1)
        wg, wu, wd = ffn_w_g[l].astype(BF16), ffn_w_u[l].astype(BF16), ffn_w_down[l].astype(BF16)
        cw = ffn_conv_w[l].astype(F32)
        x = _ffn(x, n2, sh2, sc2, g2, wg, wu, cw, wd, fg, final_norm=l == DEPTH - 1)
        if ctx_out:
            ctx = _ffn(ctx, n2, csh2, csc2, cg2, wg, wu, cw, wd, fg, final_norm=False)
    return x
```
